```python
import math
import jax
import jax.numpy as jnp
from jax import lax
import numpy as np

D_MODEL = 1024
BATCH = 8
SEQ = 2048
DEPTH = 4

N_MIXERS = 3
RMS_EPS = 1e-6
LRU_WIDTH = D_MODEL
LRU_BLOCKS = 8
LRU_BLOCK_W = LRU_WIDTH // LRU_BLOCKS
CONV_WIDTH = 4
LRU_C = 8.0
POOL_WINDOWS = (2, 4, 8, 16)
POOL_GROUPS = len(POOL_WINDOWS)
POOL_GROUP_W = D_MODEL // POOL_GROUPS
SB_HEADS = 16
SB_HEAD_DIM = D_MODEL // SB_HEADS
SB_BLOCK = 128
N_EXPERTS = 32
TOP_K = 4
D_EXPERT = D_MODEL
SWIGLU_LIMIT = 7.0
SWIGLU_ALPHA = 1.702
MOE_BLOCK = 256
N_LRU_LAYERS = len(range(0, DEPTH, N_MIXERS))
N_POOL_LAYERS = len(range(1, DEPTH, N_MIXERS))
N_SB_LAYERS = len(range(2, DEPTH, N_MIXERS))

kernel_name = 'hybrid_lru_pool_stickbreak_moe'


def rms_norm(x, g):
    xf = x.astype(jnp.float32)
    y = xf * lax.rsqrt(jnp.mean(xf * xf, axis=-1, keepdims=True) + RMS_EPS)
    return (y * g.astype(jnp.float32)).astype(x.dtype)


def causal_depthwise_conv(x, w, b):
    k_w = w.shape[0]
    s = x.shape[1]
    xp = jnp.pad(x, ((0, 0), (k_w - 1, 0), (0, 0)))
    y = b
    for k in range(k_w):
        y = y + xp[:, k:k + s] * w[k]
    return y


def block_diag_linear(x, w, b):
    g, cg, _ = w.shape
    xg = x.reshape(x.shape[:-1] + (g, cg))
    return jnp.einsum('bsgi,gio->bsgo', xg, w).reshape(x.shape) + b


def rg_lru(x, w_a, b_a, w_x, b_x, a_param):
    xf = x.astype(jnp.float32)
    r = jax.nn.sigmoid(block_diag_linear(x, w_a, b_a).astype(jnp.float32))
    i = jax.nn.sigmoid(block_diag_linear(x, w_x, b_x).astype(jnp.float32))
    log_a = -LRU_C * r * jax.nn.softplus(-a_param.astype(jnp.float32))
    a = jnp.exp(log_a)
    b_in = jnp.sqrt(-jnp.expm1(2.0 * log_a)) * (i * xf)

    def combine(left, right):
        a_l, b_l = left
        a_r, b_r = right
        return a_l * a_r, a_r * b_l + b_r

    _, h = lax.associative_scan(combine, (a, b_in), axis=1)
    return h.astype(x.dtype)


def lru_mixer(u, w_in, conv_w, conv_b, w_a, b_a, w_x, b_x, a_param, w_out):
    proj = u @ w_in
    gate, xb = proj[..., :LRU_WIDTH], proj[..., LRU_WIDTH:]
    xb = causal_depthwise_conv(xb, conv_w, conv_b)
    y = rg_lru(xb, w_a, b_a, w_x, b_x, a_param)
    y = y * jax.nn.gelu(gate)
    return y @ w_out


def pool_mixer(u, w_in, w_group, scale, w_out):
    v = u @ w_in
    vf = v.astype(jnp.float32)
    s = v.shape[1]
    csum = jnp.cumsum(vf, axis=1)
    n_avail = jnp.arange(1, s + 1, dtype=jnp.float32)[None, :, None]
    outs = []
    for g, win in enumerate(POOL_WINDOWS):
        lo, hi = g * POOL_GROUP_W, (g + 1) * POOL_GROUP_W
        c = csum[..., lo:hi]
        lag = jnp.pad(c, ((0, 0), (win, 0), (0, 0)))[:, :s]
        mean = (c - lag) / jnp.minimum(n_avail, float(win))
        outs.append(mean - vf[..., lo:hi])
    pooled = jnp.concatenate(outs, axis=-1).astype(v.dtype)
    pg = pooled.reshape(pooled.shape[:-1] + (POOL_GROUPS, POOL_GROUP_W))
    mixed = jnp.einsum('bsgi,gio->bsgo', pg, w_group).reshape(pooled.shape)
    return (mixed * scale) @ w_out


def stick_breaking_attention(q, k, v):
    s = q.shape[2]
    scale = 1.0 / math.sqrt(q.shape[-1])
    outs = []
    for blk in range(s // SB_BLOCK):
        q0 = blk * SB_BLOCK
        kv_len = q0 + SB_BLOCK
        q_b = q[:, :, q0:q0 + SB_BLOCK]
        k_b = k[:, :, :kv_len]
        v_b = v[:, :, :kv_len]
        z = jnp.einsum('bhqd,bhkd->bhqk', q_b, k_b).astype(jnp.float32) * scale
        t_pos = q0 + jnp.arange(SB_BLOCK)[:, None]
        s_pos = jnp.arange(kv_len)[None, :]
        mask = s_pos < t_pos
        log_1m = jnp.where(mask, jax.nn.log_sigmoid(-z), 0.0)
        suffix = lax.cumsum(log_1m, axis=3, reverse=True) - log_1m
        a = jnp.where(mask, jnp.exp(jax.nn.log_sigmoid(z) + suffix), 0.0)
        outs.append(jnp.einsum('bhqk,bhkd->bhqd', a.astype(v.dtype), v_b))
    return jnp.concatenate(outs, axis=2)


def sb_mixer(u, w_qkv, w_out):
    b, s, d = u.shape
    qkv = (u @ w_qkv).reshape(b, s, 3, SB_HEADS, SB_HEAD_DIM)
    qkv = qkv.transpose(2, 0, 3, 1, 4)
    o = stick_breaking_attention(qkv[0], qkv[1], qkv[2])
    o = o.transpose(0, 2, 1, 3).reshape(b, s, d)
    return o @ w_out


def moe(x2d, w_router, b_router, w_gate_up, b_gate_up, w_down, b_down):
    t, d = x2d.shape
    logits = (x2d @ w_router + b_router).astype(jnp.float32)
    top_vals, top_idx = lax.top_k(logits, TOP_K)
    gates = jax.nn.softmax(top_vals, axis=-1)
    tk = t * TOP_K
    flat_e = top_idx.reshape(-1).astype(jnp.int32)
    flat_tok = jnp.repeat(jnp.arange(t, dtype=jnp.int32), TOP_K)
    flat_gate = gates.reshape(-1)
    order = jnp.argsort(flat_e)
    sorted_e = flat_e[order]
    sorted_tok = flat_tok[order]
    sorted_gate = flat_gate[order]
    counts = jnp.bincount(flat_e, length=N_EXPERTS).astype(jnp.int32)
    starts = jnp.cumsum(counts) - counts
    padded_counts = (counts + MOE_BLOCK - 1) // MOE_BLOCK * MOE_BLOCK
    padded_ends = jnp.cumsum(padded_counts)
    padded_starts = padded_ends - padded_counts
    dest = padded_starts[sorted_e] + (jnp.arange(tk, dtype=jnp.int32) - starts[sorted_e])
    n_blocks = -(-tk // MOE_BLOCK) + N_EXPERTS
    n_slots = n_blocks * MOE_BLOCK
    slot_tok = jnp.zeros((n_slots,), jnp.int32).at[dest].set(sorted_tok)
    slot_gate = jnp.zeros((n_slots,), jnp.float32).at[dest].set(sorted_gate)
    block_starts = jnp.arange(n_blocks, dtype=jnp.int32) * MOE_BLOCK
    block_expert = jnp.minimum(jnp.searchsorted(padded_ends, block_starts, side='right'),
                               N_EXPERTS - 1).astype(jnp.int32)
    xs = x2d[slot_tok].reshape(n_blocks, MOE_BLOCK, d)

    def expert_block(args):
        xb, e = args
        gu = xb @ w_gate_up[e] + b_gate_up[e]
        gate = jnp.minimum(gu[..., :D_EXPERT], SWIGLU_LIMIT)
        up = jnp.clip(gu[..., D_EXPERT:], -SWIGLU_LIMIT, SWIGLU_LIMIT)
        glu = gate * jax.nn.sigmoid(gate * SWIGLU_ALPHA)
        return ((up + 1.0) * glu) @ w_down[e] + b_down[e]

    ys = lax.map(expert_block, (xs, block_expert)).reshape(n_slots, d)
    out = jnp.zeros((t, d), jnp.float32).at[slot_tok].add(ys.astype(jnp.float32) * slot_gate[:, None])
    return out.astype(x2d.dtype)


def setup_inputs(seed: int = 0) -> dict:
    key = jax.random.key(seed)
    ks = jax.random.split(key, 32)

    def nrm(i, shape, scale):
        return jax.random.normal(ks[i], shape, jnp.float32) * scale

    d, w, f, e = D_MODEL, LRU_WIDTH, D_EXPERT, N_EXPERTS
    out_scale = (2 * DEPTH) ** -0.5
    a0 = jax.random.uniform(ks[31], (N_LRU_LAYERS, w), jnp.float32, 0.9, 0.999)
    return {
        'x': nrm(0, (BATCH, SEQ, d), 1.0),
        'mix_norm': 1.0 + nrm(1, (DEPTH, d), 0.02),
        'ffn_norm': 1.0 + nrm(2, (DEPTH, d), 0.02),
        'final_norm': 1.0 + nrm(3, (d,), 0.02),
        'lru_w_in': nrm(4, (N_LRU_LAYERS, d, 2 * w), d ** -0.5),
        'lru_conv_w': nrm(5, (N_LRU_LAYERS, CONV_WIDTH, w), CONV_WIDTH ** -0.5),
        'lru_conv_b': nrm(6, (N_LRU_LAYERS, w), 0.01),
        'lru_w_a': nrm(7, (N_LRU_LAYERS, LRU_BLOCKS, LRU_BLOCK_W, LRU_BLOCK_W), LRU_BLOCK_W ** -0.5),
        'lru_b_a': nrm(8, (N_LRU_LAYERS, w), 0.01),
        'lru_w_x': nrm(9, (N_LRU_LAYERS, LRU_BLOCKS, LRU_BLOCK_W, LRU_BLOCK_W), LRU_BLOCK_W ** -0.5),
        'lru_b_x': nrm(10, (N_LRU_LAYERS, w), 0.01),
        'lru_a_param': jnp.log(a0) - jnp.log1p(-a0),
        'lru_w_out': nrm(11, (N_LRU_LAYERS, w, d), w ** -0.5 * out_scale),
        'pool_w_in': nrm(12, (N_POOL_LAYERS, d, d), d ** -0.5),
        'pool_w_group': nrm(13, (N_POOL_LAYERS, POOL_GROUPS, POOL_GROUP_W, POOL_GROUP_W), POOL_GROUP_W ** -0.5),
        'pool_scale': 1.0 + nrm(14, (N_POOL_LAYERS, d), 0.1),
        'pool_w_out': nrm(15, (N_POOL_LAYERS, d, d), d ** -0.5 * out_scale),
        'sb_w_qkv': nrm(16, (N_SB_LAYERS, d, 3 * d), d ** -0.5),
        'sb_w_out': nrm(17, (N_SB_LAYERS, d, d), d ** -0.5 * out_scale),
        'moe_w_router': nrm(18, (DEPTH, d, e), d ** -0.5),
        'moe_b_router': nrm(19, (DEPTH, e), 0.01),
        'moe_w_gate_up': nrm(20, (DEPTH, e, d, 2 * f), d ** -0.5),
        'moe_b_gate_up': nrm(21, (DEPTH, e, 2 * f), 0.01),
        'moe_w_down': nrm(22, (DEPTH, e, f, d), f ** -0.5 * out_scale),
        'moe_b_down': nrm(23, (DEPTH, e, d), 0.01),
    }


def reference(x, mix_norm, ffn_norm, final_norm,
              lru_w_in, lru_conv_w, lru_conv_b, lru_w_a, lru_b_a, lru_w_x, lru_b_x,
              lru_a_param, lru_w_out,
              pool_w_in, pool_w_group, pool_scale, pool_w_out,
              sb_w_qkv, sb_w_out,
              moe_w_router, moe_b_router, moe_w_gate_up, moe_b_gate_up, moe_w_down, moe_b_down):
    b, s, d = x.shape
    h = x
    for layer in range(DEPTH):
        kind = layer % N_MIXERS
        slot = layer // N_MIXERS
        u = rms_norm(h, mix_norm[layer])
        if kind == 0:
            m = lru_mixer(u, lru_w_in[slot], lru_conv_w[slot], lru_conv_b[slot],
                          lru_w_a[slot], lru_b_a[slot], lru_w_x[slot], lru_b_x[slot],
                          lru_a_param[slot], lru_w_out[slot])
        elif kind == 1:
            m = pool_mixer(u, pool_w_in[slot], pool_w_group[slot], pool_scale[slot], pool_w_out[slot])
        else:
            m = sb_mixer(u, sb_w_qkv[slot], sb_w_out[slot])
        h = h + m
        u = rms_norm(h, ffn_norm[layer]).reshape(b * s, d)
        h = h + moe(u, moe_w_router[layer], moe_b_router[layer], moe_w_gate_up[layer],
                    moe_b_gate_up[layer], moe_w_down[layer], moe_b_down[layer]).reshape(b, s, d)
    return rms_norm(h, final_norm)
```

```python
import functools
import math

import jax
import jax.numpy as jnp
from jax import lax
from jax.experimental import pallas as pl
from jax.experimental.pallas import tpu as pltpu

F32 = jnp.float32
BF16 = jnp.bfloat16
I32 = jnp.int32

RMS_EPS = 1e-6
LRU_C = 8.0
POOL_WINDOWS = (2, 4, 8, 16)
SB_HEADS = 16
TOP_K = 4
SWIGLU_LIMIT = 7.0
SWIGLU_ALPHA = 1.702
N_MIXERS = 3

V7X_VMEM_BYTES = 64 * 1024 * 1024
VMEM_LIMIT_BYTES = V7X_VMEM_BYTES - 8 * 1024 * 1024
SUBLANES = 8
LANES = 128


def _params(semantics, vmem=VMEM_LIMIT_BYTES):
    return pltpu.CompilerParams(dimension_semantics=semantics, vmem_limit_bytes=vmem)


def _pick(n, pref):
    t = min(n, pref)
    while n % t:
        t //= 2
    return t


def _rms(h, g):
    ms = jnp.mean(h * h, axis=-1, keepdims=True)
    return h * lax.rsqrt(ms + RMS_EPS) * g


def _sigmoid(x):
    return 1.0 / (1.0 + jnp.exp(-x))


def _softplus(x):
    return jnp.maximum(x, 0.0) + jnp.log(1.0 + jnp.exp(-jnp.abs(x)))


def _gelu_tanh(x):
    c = math.sqrt(2.0 / math.pi)
    return 0.5 * x * (1.0 + jnp.tanh(c * (x + 0.044715 * (x * x * x))))


def _full(shape):
    n = len(shape)
    return pl.BlockSpec(shape, lambda *_: (0,) * n)


def _lru_kernel(h_ref, g_ref, win_ref, cw_ref, cb_ref, wax_ref, ba_ref, bx_ref, ap_ref, wout_ref,
                o_ref, hcar_ref, tail_ref, y_ref, *, ts, w, bw):
    s = pl.program_id(1)

    @pl.when(s == 0)
    def _():
        hcar_ref[...] = jnp.zeros_like(hcar_ref)
        tail_ref[...] = jnp.zeros_like(tail_ref)

    h = h_ref[...]
    u = _rms(h, g_ref[...]).astype(BF16)
    proj = jnp.dot(u, win_ref[...], preferred_element_type=F32)
    row8 = lax.broadcasted_iota(I32, (SUBLANES, bw), 0)
    row = lax.broadcasted_iota(I32, (ts, bw), 0)
    kw = cw_ref.shape[0]

    for c in range(w // bw):
        lo, hi = c * bw, (c + 1) * bw
        gate = proj[:, lo:hi]
        xb = proj[:, w + lo:w + hi]
        tail = tail_ref[:, lo:hi]
        xc = cb_ref[:, lo:hi] + xb * cw_ref[kw - 1:kw, lo:hi]
        for j in range(1, kw):
            rolled = pltpu.roll(xb, j, axis=0)
            first = jnp.where(row8 < j, pltpu.roll(tail, j, axis=0), rolled[0:SUBLANES])
            shifted = jnp.concatenate([first, rolled[SUBLANES:]], axis=0)
            xc = xc + shifted * cw_ref[kw - 1 - j:kw - j, lo:hi]
        tail_ref[:, lo:hi] = xb[ts - SUBLANES:ts]

        res = jnp.dot(xc.astype(BF16), wax_ref[c], preferred_element_type=F32)
        r = _sigmoid(res[:, :bw] + ba_ref[:, lo:hi])
        i = _sigmoid(res[:, bw:] + bx_ref[:, lo:hi])
        log_a = (-LRU_C * _softplus(-ap_ref[:, lo:hi])) * r
        a = jnp.exp(log_a)
        b = jnp.sqrt(-jnp.tanh(log_a) * (a * a + 1.0)) * (i * xc)
        d = 1
        while d < ts:
            keep = row >= d
            a_sh = jnp.where(keep, pltpu.roll(a, d, axis=0), 1.0)
            b_sh = jnp.where(keep, pltpu.roll(b, d, axis=0), 0.0)
            b = a * b_sh + b
            a = a * a_sh
            d *= 2
        hs = a * hcar_ref[:, lo:hi] + b
        hcar_ref[:, lo:hi] = hs[ts - 1:ts]
        y_ref[:, lo:hi] = (hs * _gelu_tanh(gate)).astype(BF16)

    o_ref[...] = h + jnp.dot(y_ref[...], wout_ref[...], preferred_element_type=F32)


def _lru_layer(h, batch, seq, g, w_in, conv_w, conv_b, w_a, b_a, w_x, b_x, a_param, w_out):
    t, d = h.shape
    w = w_in.shape[1] // 2
    nblk, bw, _ = w_a.shape
    ts = _pick(seq, 256)
    ns = seq // ts
    wax = jnp.concatenate([w_a, w_x], axis=-1).astype(BF16)
    row = lambda v: v.reshape(1, -1)
    kern = functools.partial(_lru_kernel, ts=ts, w=w, bw=bw)
    return pl.pallas_call(
        kern,
        grid=(batch, ns),
        in_specs=[
            pl.BlockSpec((ts, d), lambda b, s: (b * ns + s, 0)),
            _full((1, d)), _full((d, 2 * w)), _full(conv_w.shape), _full((1, w)),
            _full(wax.shape), _full((1, w)), _full((1, w)), _full((1, w)), _full((w, d)),
        ],
        out_specs=pl.BlockSpec((ts, d), lambda b, s: (b * ns + s, 0)),
        out_shape=jax.ShapeDtypeStruct((t, d), F32),
        scratch_shapes=[pltpu.VMEM((1, w), F32), pltpu.VMEM((SUBLANES, w), F32), pltpu.VMEM((ts, w), BF16)],
        compiler_params=_params(("arbitrary", "arbitrary")),
        name="lru_mixer",
    )(h, row(g), w_in.astype(BF16), conv_w, row(conv_b), wax, row(b_a), row(b_x), row(a_param),
      w_out.astype(BF16))


def _pool_kernel(h_ref, g_ref, win_ref, wg_ref, sc_ref, wout_ref, o_ref, ext_ref, m_ref, *, ts, d, wins):
    s = pl.program_id(1)
    pad = wins[-1]
    gw = d // len(wins)

    @pl.when(s == 0)
    def _():
        ext_ref[0:pad, :] = jnp.zeros((pad, d), F32)

    h = h_ref[...]
    u = _rms(h, g_ref[...]).astype(BF16)
    v = jnp.dot(u, win_ref[...], preferred_element_type=F32)
    ext_ref[pad:pad + ts, :] = v
    n_avail = (s * ts + lax.broadcasted_iota(I32, (ts, 1), 0) + 1).astype(F32)

    cur = ext_ref[...]
    width = 1
    for gi, win in enumerate(wins):
        while width < win:
            cur = cur + pltpu.roll(cur, width, axis=0)
            width *= 2
        lo, hi = gi * gw, (gi + 1) * gw
        mean = cur[pad:, 0:gw] / jnp.minimum(n_avail, float(win))
        pooled = mean - v[:, lo:hi]
        mixed = jnp.dot(pooled.astype(BF16), wg_ref[gi], preferred_element_type=F32)
        m_ref[:, lo:hi] = (mixed * sc_ref[:, lo:hi]).astype(BF16)
        cur = cur[:, gw:]
    ext_ref[0:pad, :] = v[ts - pad:ts]
    o_ref[...] = h + jnp.dot(m_ref[...], wout_ref[...], preferred_element_type=F32)


def _pool_layer(h, batch, seq, g, w_in, w_group, scale, w_out):
    t, d = h.shape
    ts = _pick(seq, 256)
    ns = seq // ts
    pad = POOL_WINDOWS[-1]
    kern = functools.partial(_pool_kernel, ts=ts, d=d, wins=POOL_WINDOWS)
    return pl.pallas_call(
        kern,
        grid=(batch, ns),
        in_specs=[
            pl.BlockSpec((ts, d), lambda b, s: (b * ns + s, 0)),
            _full((1, d)), _full((d, d)), _full(w_group.shape), _full((1, d)), _full((d, d)),
        ],
        out_specs=pl.BlockSpec((ts, d), lambda b, s: (b * ns + s, 0)),
        out_shape=jax.ShapeDtypeStruct((t, d), F32),
        scratch_shapes=[pltpu.VMEM((ts + pad, d), F32), pltpu.VMEM((ts, d), BF16)],
        compiler_params=_params(("arbitrary", "arbitrary")),
        name="pool_mixer",
    )(h, g.reshape(1, d), w_in.astype(BF16), w_group.astype(BF16), scale.reshape(1, d), w_out.astype(BF16))


def _sb_proj_kernel(h_ref, g_ref, wqt_ref, wk_ref, wvt_ref, qt_ref, k_ref, vt_ref):
    u = _rms(h_ref[...], g_ref[...]).astype(BF16)
    nt = (((1,), (1,)), ((), ()))
    qt_ref[...] = lax.dot_general(wqt_ref[...], u, nt, preferred_element_type=F32).astype(BF16)
    k_ref[...] = jnp.dot(u, wk_ref[...], preferred_element_type=F32).astype(BF16)
    vt_ref[...] = lax.dot_general(wvt_ref[...], u, nt, preferred_element_type=F32).astype(BF16)


def _sb_attn_kernel(qt_ref, k_ref, vt_ref, o_ref, *, bq, dh):
    qb = pl.program_id(2)
    krow = lax.broadcasted_iota(I32, (bq, bq), 0)
    qcol = lax.broadcasted_iota(I32, (bq, bq), 1)
    upper = (qcol > krow).astype(BF16)
    causal = krow < qcol

    def block(hh, j, masked, acc, carry):
        qh = qt_ref[hh * dh:(hh + 1) * dh, :]
        kh = k_ref[pl.ds(j * bq, bq), hh * dh:(hh + 1) * dh]
        vth = vt_ref[hh * dh:(hh + 1) * dh, pl.ds(j * bq, bq)]
        z = jnp.dot(kh, qh, preferred_element_type=F32)
        sp = _softplus(z)
        lm = jnp.where(causal, -sp, 0.0) if masked else -sp
        l_hi = lm.astype(BF16)
        l_lo = (lm - l_hi.astype(F32)).astype(BF16)
        suffix = (jnp.dot(upper, l_hi, preferred_element_type=F32)
                  + jnp.dot(upper, l_lo, preferred_element_type=F32))
        a = jnp.exp((z - sp) + suffix + carry)
        if masked:
            a = jnp.where(causal, a, 0.0)
        acc = acc + jnp.dot(vth, a.astype(BF16), preferred_element_type=F32)
        carry = carry + suffix[0:1, :] + lm[0:1, :]
        return acc, carry

    outs = []
    for hh in range(2):
        acc = jnp.zeros((dh, bq), F32)
        carry = jnp.zeros((1, bq), F32)
        acc, carry = block(hh, qb, True, acc, carry)

        def body(i, ac, hh=hh):
            return block(hh, qb - 1 - i, False, *ac)

        acc, carry = lax.fori_loop(0, qb, body, (acc, carry))
        outs.append(acc.T)
    o_ref[...] = jnp.concatenate(outs, axis=1).astype(BF16)


def _sb_out_kernel(h_ref, o_ref, w_ref, out_ref):
    out_ref[...] = h_ref[...] + jnp.dot(o_ref[...], w_ref[...], preferred_element_type=F32)


def _sb_layer(h, batch, seq, g, w_qkv, w_out):
    t, d = h.shape
    dh = d // SB_HEADS
    scale = 1.0 / math.sqrt(dh)
    wqt = (w_qkv[:, :d] * scale).T.astype(BF16)
    wk = w_qkv[:, d:2 * d].astype(BF16)
    wvt = w_qkv[:, 2 * d:].T.astype(BF16)
    tp = _pick(t, 512)
    qt, k, vt = pl.pallas_call(
        _sb_proj_kernel,
        grid=(t // tp,),
        in_specs=[pl.BlockSpec((tp, d), lambda i: (i, 0)), _full((1, d)), _full((d, d)), _full((d, d)),
                  _full((d, d))],
        out_specs=[pl.BlockSpec((d, tp), lambda i: (0, i)), pl.BlockSpec((tp, d), lambda i: (i, 0)),
                   pl.BlockSpec((d, tp), lambda i: (0, i))],
        out_shape=[jax.ShapeDtypeStruct((d, t), BF16), jax.ShapeDtypeStruct((t, d), BF16),
                   jax.ShapeDtypeStruct((d, t), BF16)],
        compiler_params=_params(("arbitrary",)),
        name="sb_qkv_proj",
    )(h, g.reshape(1, d), wqt, wk, wvt)

    bq = _pick(seq, 256)
    nq = seq // bq
    pair = 2 * dh
    o = pl.pallas_call(
        functools.partial(_sb_attn_kernel, bq=bq, dh=dh),
        grid=(batch, d // pair, nq),
        in_specs=[
            pl.BlockSpec((pair, bq), lambda b, p, q: (p, b * nq + q)),
            pl.BlockSpec((seq, pair), lambda b, p, q: (b, p)),
            pl.BlockSpec((pair, seq), lambda b, p, q: (p, b)),
        ],
        out_specs=pl.BlockSpec((bq, pair), lambda b, p, q: (b * nq + q, p)),
        out_shape=jax.ShapeDtypeStruct((t, d), BF16),
        compiler_params=_params(("arbitrary", "arbitrary", "arbitrary")),
        name="sb_attention",
    )(qt, k, vt)

    return pl.pallas_call(
        _sb_out_kernel,
        grid=(t // tp,),
        in_specs=[pl.BlockSpec((tp, d), lambda i: (i, 0)), pl.BlockSpec((tp, d), lambda i: (i, 0)),
                  _full((d, d))],
        out_specs=pl.BlockSpec((tp, d), lambda i: (i, 0)),
        out_shape=jax.ShapeDtypeStruct((t, d), F32),
        compiler_params=_params(("arbitrary",)),
        name="sb_out_proj",
    )(h, o, w_out.astype(BF16))


def _route_kernel(h_ref, g_ref, wr_hi_ref, wr_lo_ref, br_ref, idx_ref, gate_ref, *, n_exp, top_k):
    u = _rms(h_ref[...], g_ref[...])
    u_hi = u.astype(BF16)
    u_lo = (u - u_hi.astype(F32)).astype(BF16)
    logits = (jnp.dot(u_hi, wr_hi_ref[...], preferred_element_type=F32)
              + jnp.dot(u_hi, wr_lo_ref[...], preferred_element_type=F32)
              + jnp.dot(u_lo, wr_hi_ref[...], preferred_element_type=F32)) + br_ref[...]
    tm = logits.shape[0]
    lane = lax.broadcasted_iota(I32, (tm, n_exp), 1)
    kcol = lax.broadcasted_iota(I32, (tm, top_k), 1)
    vals = logits
    top_v, top_i = [], []
    for _ in range(top_k):
        m = jnp.max(vals, axis=-1, keepdims=True)
        sel = jnp.min(jnp.where(vals == m, lane, n_exp), axis=-1, keepdims=True)
        top_v.append(m)
        top_i.append(sel)
        vals = jnp.where(lane == sel, -jnp.inf, vals)
    exps = [jnp.exp(v - top_v[0]) for v in top_v]
    denom = exps[0]
    for e in exps[1:]:
        denom = denom + e
    idx = jnp.zeros((tm, top_k), I32)
    gates = jnp.zeros((tm, top_k), F32)
    for k in range(top_k):
        idx = jnp.where(kcol == k, top_i[k], idx)
        gates = jnp.where(kcol == k, exps[k] / denom, gates)
    idx_ref[...] = idx
    gate_ref[...] = gates


def _rank_kernel(idx_ref, dest_ref, bexp_ref, meta_ref, run_ref, pstart_ref, *, n_exp, top_k, bm, nbp):
    p = pl.program_id(0)
    i = pl.program_id(1)
    nt = pl.num_programs(1)
    tm = idx_ref.shape[0]
    idx = idx_ref[...]
    lane = lax.broadcasted_iota(I32, (tm, n_exp), 1)
    ohs = [(idx[:, k:k + 1] == lane).astype(F32) for k in range(top_k)]
    oh = ohs[0]
    for o in ohs[1:]:
        oh = oh + o

    @pl.when((p == 0) & (i == 0))
    def _():
        run_ref[...] = jnp.zeros_like(run_ref)

    @pl.when(p == 0)
    def _():
        run_ref[...] = run_ref[...] + jnp.sum(oh, axis=0, keepdims=True)

    @pl.when((p == 0) & (i == nt - 1))
    def _():
        cnt = run_ref[...]
        nb = jnp.floor((cnt + (bm - 1)) * (1.0 / bm))
        nb8 = jnp.broadcast_to(nb, (SUBLANES, n_exp)).astype(BF16)
        er = lax.broadcasted_iota(I32, (n_exp, n_exp), 0)
        ec = lax.broadcasted_iota(I32, (n_exp, n_exp), 1)
        before = (er < ec).astype(BF16)
        bstart = jnp.dot(nb8, before, preferred_element_type=F32)[0:1, :]
        bend = bstart + nb
        pstart_ref[...] = bstart * bm
        blk = lax.broadcasted_iota(I32, (nbp, n_exp), 0).astype(F32)
        be = jnp.sum((bend <= blk).astype(F32), axis=-1, keepdims=True)
        bexp_ref[...] = jnp.minimum(be, n_exp - 1).astype(I32)
        n_used = jnp.sum(nb, axis=-1, keepdims=True)
        last = jnp.where(nb > 0, bend - 1.0, -1.0)
        mrow = lax.broadcasted_iota(I32, (SUBLANES, n_exp), 0)
        meta = jnp.where(mrow == 0, jnp.broadcast_to(n_used, (SUBLANES, n_exp)),
                         jnp.where(mrow == 1, jnp.broadcast_to(last, (SUBLANES, n_exp)), 0.0))
        meta_ref[...] = meta.astype(I32)
        run_ref[...] = jnp.zeros_like(run_ref)

    @pl.when(p == 1)
    def _():
        tr = lax.broadcasted_iota(I32, (tm, tm), 0)
        tc = lax.broadcasted_iota(I32, (tm, tm), 1)
        earlier = (tc < tr).astype(BF16)
        excl = jnp.dot(earlier, oh.astype(BF16), preferred_element_type=F32)
        base = excl + run_ref[...] + pstart_ref[...]
        kcol = lax.broadcasted_iota(I32, (tm, top_k), 1)
        dest = jnp.zeros((tm, top_k), I32)
        for k in range(top_k):
            dk = jnp.sum(ohs[k] * base, axis=-1, keepdims=True).astype(I32)
            dest = jnp.where(kcol == k, dk, dest)
        dest_ref[...] = dest
        run_ref[...] = run_ref[...] + jnp.sum(oh, axis=0, keepdims=True)


def _dispatch_kernel(last_ref, dest_ref, h_ref, g_ref, xs_ref, ubuf, zbuf, sem, *, n_exp, top_k, bm):
    i = pl.program_id(0)
    td = h_ref.shape[0]

    @pl.when(i == 0)
    def _():
        zbuf[...] = jnp.zeros_like(zbuf)

        def zero_copy(e):
            return pltpu.make_async_copy(zbuf, xs_ref.at[pl.ds(last_ref[e] * bm, bm)], sem)

        def start(e, c):
            @pl.when(last_ref[e] >= 0)
            def _():
                zero_copy(e).start()
            return c

        def wait(e, c):
            @pl.when(last_ref[e] >= 0)
            def _():
                zero_copy(e).wait()
            return c

        lax.fori_loop(0, n_exp, start, 0)
        lax.fori_loop(0, n_exp, wait, 0)

    ubuf[...] = _rms(h_ref[...], g_ref[...])

    def issue(t, c):
        for k in range(top_k):
            d = dest_ref[0, 0, t * top_k + k]
            pltpu.make_async_copy(ubuf.at[pl.ds(t, 1)], xs_ref.at[pl.ds(d, 1)], sem).start()
        return c

    lax.fori_loop(0, td, issue, 0, unroll=8)
    for _ in range(top_k):
        pltpu.make_async_copy(ubuf, xs_ref.at[pl.ds(0, td)], sem).wait()


def _ffn_kernel(bexp_ref, nused_ref, x_ref, wgu_ref, bgu_ref, wd_ref, bd_ref, y_ref, wgu_bf, wd_bf, *, f):
    b = pl.program_id(0)
    prev = bexp_ref[jnp.maximum(b - 1, 0)]
    active = b < nused_ref[0]

    @pl.when(active & ((b == 0) | (bexp_ref[b] != prev)))
    def _():
        wgu_bf[...] = wgu_ref[0, 0].astype(BF16)
        wd_bf[...] = wd_ref[0, 0].astype(BF16)

    @pl.when(active)
    def _():
        x = x_ref[...].astype(BF16)
        gu = jnp.dot(x, wgu_bf[...], preferred_element_type=F32) + bgu_ref[0]
        gate = jnp.minimum(gu[:, :f], SWIGLU_LIMIT)
        up = jnp.clip(gu[:, f:], -SWIGLU_LIMIT, SWIGLU_LIMIT)
        glu = gate * _sigmoid(gate * SWIGLU_ALPHA)
        act = ((up + 1.0) * glu).astype(BF16)
        y_ref[...] = jnp.dot(act, wd_bf[...], preferred_element_type=F32) + bd_ref[0]


def _combine_kernel(dest_ref, h_ref, gate_ref, ys_ref, fg_ref, o_ref, ybuf, sem, *, top_k, final):
    tc = h_ref.shape[0]

    def issue(t, c):
        for k in range(top_k):
            d = dest_ref[0, 0, t * top_k + k]
            pltpu.make_async_copy(ys_ref.at[pl.ds(d, 1)], ybuf.at[k, pl.ds(t, 1)], sem).start()
        return c

    lax.fori_loop(0, tc, issue, 0, unroll=8)
    for k in range(top_k):
        pltpu.make_async_copy(ys_ref.at[pl.ds(0, tc)], ybuf.at[k], sem).wait()
    gates = gate_ref[...]
    moe = ybuf[0] * gates[:, 0:1]
    for k in range(1, top_k):
        moe = moe + ybuf[k] * gates[:, k:k + 1]
    out = h_ref[...] + moe
    if final:
        out = _rms(out, fg_ref[...])
    o_ref[...] = out


def _moe_layer(h, g, w_router, b_router, w_gate_up, b_gate_up, w_down, b_down, final_g, final, layer):
    t, d = h.shape
    n_exp = w_router.shape[1]
    f = w_down.shape[2]
    top_k = TOP_K
    bm = 256
    n_blocks = -(-(t * top_k) // bm) + n_exp
    n_slots = n_blocks * bm
    nbp = -(-n_blocks // SUBLANES) * SUBLANES

    tm = _pick(t, 512)
    wr_hi = w_router.astype(BF16)
    wr_lo = (w_router - wr_hi.astype(F32)).astype(BF16)
    idx, gates = pl.pallas_call(
        functools.partial(_route_kernel, n_exp=n_exp, top_k=top_k),
        grid=(t // tm,),
        in_specs=[pl.BlockSpec((tm, d), lambda i: (i, 0)), _full((1, d)), _full((d, n_exp)),
                  _full((d, n_exp)), _full((1, n_exp))],
        out_specs=[pl.BlockSpec((tm, top_k), lambda i: (i, 0)), pl.BlockSpec((tm, top_k), lambda i: (i, 0))],
        out_shape=[jax.ShapeDtypeStruct((t, top_k), I32), jax.ShapeDtypeStruct((t, top_k), F32)],
        compiler_params=_params(("arbitrary",)),
        name="moe_route",
    )(h, g.reshape(1, d), wr_hi, wr_lo, b_router.reshape(1, n_exp))

    dest, bexp, meta = pl.pallas_call(
        functools.partial(_rank_kernel, n_exp=n_exp, top_k=top_k, bm=bm, nbp=nbp),
        grid=(2, t // tm),
        in_specs=[pl.BlockSpec((tm, top_k), lambda p, i: (i, 0))],
        out_specs=[pl.BlockSpec((tm, top_k), lambda p, i: (i * p, 0)), _full((nbp, 1)),
                   _full((SUBLANES, n_exp))],
        out_shape=[jax.ShapeDtypeStruct((t, top_k), I32), jax.ShapeDtypeStruct((nbp, 1), I32),
                   jax.ShapeDtypeStruct((SUBLANES, n_exp), I32)],
        scratch_shapes=[pltpu.VMEM((1, n_exp), F32), pltpu.VMEM((1, n_exp), F32)],
        compiler_params=_params(("arbitrary", "arbitrary")),
        name="moe_rank",
    )(idx)
    bexp = bexp.reshape(nbp)
    n_used = meta[0, 0:1]
    last_blk = meta[1]

    td = _pick(t, 256)
    dest_tiles = dest.reshape(t // td, 1, td * top_k)
    xs = pl.pallas_call(
        functools.partial(_dispatch_kernel, n_exp=n_exp, top_k=top_k, bm=bm),
        grid_spec=pltpu.PrefetchScalarGridSpec(
            num_scalar_prefetch=1,
            grid=(t // td,),
            in_specs=[
                pl.BlockSpec((1, 1, td * top_k), lambda i, last: (i, 0, 0), memory_space=pltpu.SMEM),
                pl.BlockSpec((td, d), lambda i, last: (i, 0)),
                pl.BlockSpec((1, d), lambda i, last: (0, 0)),
            ],
            out_specs=pl.BlockSpec(memory_space=pl.ANY),
            scratch_shapes=[pltpu.VMEM((td, d), F32), pltpu.VMEM((bm, d), F32), pltpu.SemaphoreType.DMA(())],
        ),
        out_shape=jax.ShapeDtypeStruct((n_slots, d), F32),
        compiler_params=_params(("arbitrary",)),
        name="moe_dispatch",
    )(last_blk, dest_tiles, h, g.reshape(1, d))

    def blk(b, bexp_ref, nused_ref):
        return jnp.minimum(b, nused_ref[0] - 1)

    ys = pl.pallas_call(
        functools.partial(_ffn_kernel, f=f),
        grid_spec=pltpu.PrefetchScalarGridSpec(
            num_scalar_prefetch=2,
            grid=(n_blocks,),
            in_specs=[
                pl.BlockSpec((bm, d), lambda b, be, nu: (blk(b, be, nu), 0)),
                pl.BlockSpec((1, 1, d, 2 * f), lambda b, be, nu: (layer, be[blk(b, be, nu)], 0, 0)),
                pl.BlockSpec((1, 1, 2 * f), lambda b, be, nu: (be[blk(b, be, nu)], 0, 0)),
                pl.BlockSpec((1, 1, f, d), lambda b, be, nu: (layer, be[blk(b, be, nu)], 0, 0)),
                pl.BlockSpec((1, 1, d), lambda b, be, nu: (be[blk(b, be, nu)], 0, 0)),
            ],
            out_specs=pl.BlockSpec((bm, d), lambda b, be, nu: (blk(b, be, nu), 0)),
            scratch_shapes=[pltpu.VMEM((d, 2 * f), BF16), pltpu.VMEM((f, d), BF16)],
        ),
        out_shape=jax.ShapeDtypeStruct((n_slots, d), F32),
        compiler_params=_params(("arbitrary",)),
        name="moe_ffn",
    )(bexp, n_used, xs, w_gate_up, b_gate_up.reshape(n_exp, 1, 2 * f), w_down, b_down.reshape(n_exp, 1, d))

    return pl.pallas_call(
        functools.partial(_combine_kernel, top_k=top_k, final=final),
        grid=(t // td,),
        in_specs=[
            pl.BlockSpec((1, 1, td * top_k), lambda i: (i, 0, 0), memory_space=pltpu.SMEM),
            pl.BlockSpec((td, d), lambda i: (i, 0)),
            pl.BlockSpec((td, top_k), lambda i: (i, 0)),
            pl.BlockSpec(memory_space=pl.ANY),
            _full((1, d)),
        ],
        out_specs=pl.BlockSpec((td, d), lambda i: (i, 0)),
        out_shape=jax.ShapeDtypeStruct((t, d), F32),
        scratch_shapes=[pltpu.VMEM((top_k, td, d), F32), pltpu.SemaphoreType.DMA(())],
        compiler_params=_params(("arbitrary",)),
        name="moe_combine",
    )(dest_tiles, h, gates, ys, final_g.reshape(1, d))


def kernel(x, mix_norm, ffn_norm, final_norm, lru_w_in, lru_conv_w, lru_conv_b, lru_w_a, lru_b_a, lru_w_x, lru_b_x, lru_a_param, lru_w_out, pool_w_in, pool_w_group, pool_scale, pool_w_out, sb_w_qkv, sb_w_out, moe_w_router, moe_b_router, moe_w_gate_up, moe_b_gate_up, moe_w_down, moe_b_down):
    batch, seq, d = x.shape
    depth = mix_norm.shape[0]
    h = x.reshape(batch * seq, d)
    for layer in range(depth):
        kind = layer % N_MIXERS
        slot = layer // N_MIXERS
        if kind == 0:
            h = _lru_layer(h, batch, seq, mix_norm[layer], lru_w_in[slot], lru_conv_w[slot], lru_conv_b[slot],
                           lru_w_a[slot], lru_b_a[slot], lru_w_x[slot], lru_b_x[slot], lru_a_param[slot],
                           lru_w_out[slot])
        elif kind == 1:
            h = _pool_layer(h, batch, seq, mix_norm[layer], pool_w_in[slot], pool_w_group[slot], pool_scale[slot],
                            pool_w_out[slot])
        else:
            h = _sb_layer(h, batch, seq, mix_norm[layer], sb_w_qkv[slot], sb_w_out[slot])
        h = _moe_layer(h, ffn_norm[layer], moe_w_router[layer], moe_b_router[layer], moe_w_gate_up,
                       moe_b_gate_up[layer], moe_w_down, moe_b_down[layer], final_norm,
                       final=(layer == depth - 1), layer=layer)
    return h.reshape(batch, seq, d)
```

```python
import functools
import math

import jax
import jax.numpy as jnp
from jax import lax
from jax.experimental import pallas as pl
from jax.experimental.pallas import tpu as pltpu

F32 = jnp.float32
BF16 = jnp.bfloat16
I32 = jnp.int32

RMS_EPS = 1e-6
LRU_C = 8.0
POOL_WINDOWS = (2, 4, 8, 16)
SB_HEADS = 16
SB_HEADS_PER_STEP = 4
LOG2E = 1.4426950408889634
TOP_K = 4
SWIGLU_LIMIT = 7.0
SWIGLU_ALPHA = 1.702
N_MIXERS = 3

V7X_VMEM_BYTES = 64 * 1024 * 1024
VMEM_LIMIT_BYTES = V7X_VMEM_BYTES - 8 * 1024 * 1024
SUBLANES = 8
LANES = 128

MOE_BLOCK = 256
MOE_SORT_TILE = 256
RUN_ALIGN = SUBLANES


def _params(semantics, vmem=VMEM_LIMIT_BYTES):
    return pltpu.CompilerParams(dimension_semantics=semantics, vmem_limit_bytes=vmem)


def _pick(n, pref):
    t = min(n, pref)
    while n % t:
        t //= 2
    return t


def _rms(h, g):
    ms = jnp.mean(h * h, axis=-1, keepdims=True)
    return h * lax.rsqrt(ms + RMS_EPS) * g


def _sigmoid(x):
    return 1.0 / (1.0 + jnp.exp(-x))


def _softplus(x):
    return jnp.maximum(x, 0.0) + jnp.log(1.0 + jnp.exp(-jnp.abs(x)))


def _gelu_tanh(x):
    c = math.sqrt(2.0 / math.pi)
    return 0.5 * x * (1.0 + jnp.tanh(c * (x + 0.044715 * (x * x * x))))


def _full(shape):
    n = len(shape)
    return pl.BlockSpec(shape, lambda *_: (0,) * n)


def _lru_kernel(h_ref, g_ref, win_ref, cw_ref, cb_ref, wax_ref, ba_ref, bx_ref, ap_ref, wout_ref,
                o_ref, hcar_ref, tail_ref, y_ref, *, ts, w, bw):
    s = pl.program_id(1)

    @pl.when(s == 0)
    def _():
        hcar_ref[...] = jnp.zeros_like(hcar_ref)
        tail_ref[...] = jnp.zeros_like(tail_ref)

    h = h_ref[...]
    u = _rms(h, g_ref[...]).astype(BF16)
    proj = jnp.dot(u, win_ref[...], preferred_element_type=F32)
    row8 = lax.broadcasted_iota(I32, (SUBLANES, bw), 0)
    row = lax.broadcasted_iota(I32, (ts, bw), 0)
    kw = cw_ref.shape[0]

    for c in range(w // bw):
        lo, hi = c * bw, (c + 1) * bw
        gate = proj[:, lo:hi]
        xb = proj[:, w + lo:w + hi]
        tail = tail_ref[:, lo:hi]
        xc = cb_ref[:, lo:hi] + xb * cw_ref[kw - 1:kw, lo:hi]
        for j in range(1, kw):
            rolled = pltpu.roll(xb, j, axis=0)
            first = jnp.where(row8 < j, pltpu.roll(tail, j, axis=0), rolled[0:SUBLANES])
            shifted = jnp.concatenate([first, rolled[SUBLANES:]], axis=0)
            xc = xc + shifted * cw_ref[kw - 1 - j:kw - j, lo:hi]
        tail_ref[:, lo:hi] = xb[ts - SUBLANES:ts]

        res = jnp.dot(xc.astype(BF16), wax_ref[c], preferred_element_type=F32)
        r = _sigmoid(res[:, :bw] + ba_ref[:, lo:hi])
        i = _sigmoid(res[:, bw:] + bx_ref[:, lo:hi])
        log_a = (-LRU_C * _softplus(-ap_ref[:, lo:hi])) * r
        a = jnp.exp(log_a)
        b = jnp.sqrt(-jnp.tanh(log_a) * (a * a + 1.0)) * (i * xc)
        d = 1
        while d < ts:
            keep = row >= d
            a_sh = jnp.where(keep, pltpu.roll(a, d, axis=0), 1.0)
            b_sh = jnp.where(keep, pltpu.roll(b, d, axis=0), 0.0)
            b = a * b_sh + b
            a = a * a_sh
            d *= 2
        hs = a * hcar_ref[:, lo:hi] + b
        hcar_ref[:, lo:hi] = hs[ts - 1:ts]
        y_ref[:, lo:hi] = (hs * _gelu_tanh(gate)).astype(BF16)

    o_ref[...] = h + jnp.dot(y_ref[...], wout_ref[...], preferred_element_type=F32)


def _lru_layer(h, batch, seq, g, w_in, conv_w, conv_b, w_a, b_a, w_x, b_x, a_param, w_out):
    t, d = h.shape
    w = w_in.shape[1] // 2
    nblk, bw, _ = w_a.shape
    ts = _pick(seq, 256)
    ns = seq // ts
    wax = jnp.concatenate([w_a, w_x], axis=-1).astype(BF16)
    row = lambda v: v.reshape(1, -1)
    kern = functools.partial(_lru_kernel, ts=ts, w=w, bw=bw)
    return pl.pallas_call(
        kern,
        grid=(batch, ns),
        in_specs=[
            pl.BlockSpec((ts, d), lambda b, s: (b * ns + s, 0)),
            _full((1, d)), _full((d, 2 * w)), _full(conv_w.shape), _full((1, w)),
            _full(wax.shape), _full((1, w)), _full((1, w)), _full((1, w)), _full((w, d)),
        ],
        out_specs=pl.BlockSpec((ts, d), lambda b, s: (b * ns + s, 0)),
        out_shape=jax.ShapeDtypeStruct((t, d), F32),
        scratch_shapes=[pltpu.VMEM((1, w), F32), pltpu.VMEM((SUBLANES, w), F32), pltpu.VMEM((ts, w), BF16)],
        compiler_params=_params(("arbitrary", "arbitrary")),
        name="lru_mixer",
    )(h, row(g), w_in.astype(BF16), conv_w, row(conv_b), wax, row(b_a), row(b_x), row(a_param),
      w_out.astype(BF16))


def _pool_kernel(h_ref, g_ref, win_ref, wg_ref, sc_ref, wout_ref, o_ref, ext_ref, m_ref, *, ts, d, wins):
    s = pl.program_id(1)
    pad = wins[-1]
    gw = d // len(wins)

    @pl.when(s == 0)
    def _():
        ext_ref[0:pad, :] = jnp.zeros((pad, d), F32)

    h = h_ref[...]
    u = _rms(h, g_ref[...]).astype(BF16)
    v = jnp.dot(u, win_ref[...], preferred_element_type=F32)
    ext_ref[pad:pad + ts, :] = v
    n_avail = (s * ts + lax.broadcasted_iota(I32, (ts, 1), 0) + 1).astype(F32)

    cur = ext_ref[...]
    width = 1
    for gi, win in enumerate(wins):
        while width < win:
            cur = cur + pltpu.roll(cur, width, axis=0)
            width *= 2
        lo, hi = gi * gw, (gi + 1) * gw
        mean = cur[pad:, 0:gw] / jnp.minimum(n_avail, float(win))
        pooled = mean - v[:, lo:hi]
        mixed = jnp.dot(pooled.astype(BF16), wg_ref[gi], preferred_element_type=F32)
        m_ref[:, lo:hi] = (mixed * sc_ref[:, lo:hi]).astype(BF16)
        cur = cur[:, gw:]
    ext_ref[0:pad, :] = v[ts - pad:ts]
    o_ref[...] = h + jnp.dot(m_ref[...], wout_ref[...], preferred_element_type=F32)


def _pool_layer(h, batch, seq, g, w_in, w_group, scale, w_out):
    t, d = h.shape
    ts = _pick(seq, 256)
    ns = seq // ts
    pad = POOL_WINDOWS[-1]
    kern = functools.partial(_pool_kernel, ts=ts, d=d, wins=POOL_WINDOWS)
    return pl.pallas_call(
        kern,
        grid=(batch, ns),
        in_specs=[
            pl.BlockSpec((ts, d), lambda b, s: (b * ns + s, 0)),
            _full((1, d)), _full((d, d)), _full(w_group.shape), _full((1, d)), _full((d, d)),
        ],
        out_specs=pl.BlockSpec((ts, d), lambda b, s: (b * ns + s, 0)),
        out_shape=jax.ShapeDtypeStruct((t, d), F32),
        scratch_shapes=[pltpu.VMEM((ts + pad, d), F32), pltpu.VMEM((ts, d), BF16)],
        compiler_params=_params(("arbitrary", "arbitrary")),
        name="pool_mixer",
    )(h, g.reshape(1, d), w_in.astype(BF16), w_group.astype(BF16), scale.reshape(1, d), w_out.astype(BF16))


def _sb_proj_kernel(h_ref, g_ref, wqt_ref, wk_ref, wvt_ref, qt_ref, k_ref, vt_ref):
    u = _rms(h_ref[...], g_ref[...]).astype(BF16)
    nt = (((1,), (1,)), ((), ()))
    qt_ref[...] = lax.dot_general(wqt_ref[...], u, nt, preferred_element_type=F32).astype(BF16)
    k_ref[...] = jnp.dot(u, wk_ref[...], preferred_element_type=F32).astype(BF16)
    vt_ref[...] = lax.dot_general(wvt_ref[...], u, nt, preferred_element_type=F32).astype(BF16)


def _sb_attn_kernel(qt_ref, k_ref, vt_ref, o_ref, lhi_ref, llo_ref, zs_ref, a_ref, *, bq, dh, nh):
    qb = pl.program_id(2)
    krow = lax.broadcasted_iota(I32, (bq, bq), 0)
    qcol = lax.broadcasted_iota(I32, (bq, bq), 1)
    upper = (qcol > krow).astype(BF16)
    causal = krow < qcol

    def step(chains, masked, st):
        totals = []
        for c, (hh, j) in enumerate(chains):
            qh = qt_ref[hh * dh:(hh + 1) * dh, :]
            kh = k_ref[pl.ds(j * bq, bq), hh * dh:(hh + 1) * dh]
            z = jnp.dot(kh, qh, preferred_element_type=F32)
            t = jnp.exp2(jnp.abs(z) * (-LOG2E))
            l1m = jnp.minimum(-z, 0.0) - jnp.log(1.0 + t)
            zs_ref[c] = z + l1m
            lm = jnp.where(causal, l1m, 0.0) if masked else l1m
            l_hi = lm.astype(BF16)
            lhi_ref[c] = l_hi
            llo_ref[c] = (lm - l_hi.astype(F32)).astype(BF16)
            totals.append(jnp.sum(lm, axis=0, keepdims=True))
        st = list(st)
        for c, (hh, j) in enumerate(chains):
            acc, carry = st[hh]
            suffix = (jnp.dot(upper, lhi_ref[c], preferred_element_type=F32)
                      + jnp.dot(upper, llo_ref[c], preferred_element_type=F32))
            a = jnp.exp(zs_ref[c] + suffix + carry)
            if masked:
                a = jnp.where(causal, a, 0.0)
            a_ref[c] = a.astype(BF16)
            st[hh] = (acc, carry + totals[c])
        for c, (hh, j) in enumerate(chains):
            acc, carry = st[hh]
            vth = vt_ref[hh * dh:(hh + 1) * dh, pl.ds(j * bq, bq)]
            st[hh] = (acc + jnp.dot(vth, a_ref[c], preferred_element_type=F32), carry)
        return st

    heads = range(nh)
    zero = (jnp.zeros((dh, bq), F32), jnp.zeros((1, bq), F32))
    st = step([(hh, qb) for hh in heads], True, [zero] * nh)

    def pair_body(i, st):
        j = qb - 1 - 2 * i
        return step([(hh, j) for hh in heads] + [(hh, j - 1) for hh in heads], False, st)

    def single_body(i, st):
        return step([(hh, 0) for hh in heads], False, st)

    st = lax.fori_loop(0, lax.shift_right_logical(qb, 1), pair_body, st)
    st = lax.fori_loop(0, qb & 1, single_body, st)
    o_ref[...] = jnp.concatenate([st[hh][0].T for hh in heads], axis=1).astype(BF16)


def _sb_out_kernel(h_ref, o_ref, w_ref, out_ref):
    out_ref[...] = h_ref[...] + jnp.dot(o_ref[...], w_ref[...], preferred_element_type=F32)


def _sb_layer(h, batch, seq, g, w_qkv, w_out):
    t, d = h.shape
    dh = d // SB_HEADS
    scale = 1.0 / math.sqrt(dh)
    wqt = (w_qkv[:, :d] * scale).T.astype(BF16)
    wk = w_qkv[:, d:2 * d].astype(BF16)
    wvt = w_qkv[:, 2 * d:].T.astype(BF16)
    tp = _pick(t, 512)
    qt, k, vt = pl.pallas_call(
        _sb_proj_kernel,
        grid=(t // tp,),
        in_specs=[pl.BlockSpec((tp, d), lambda i: (i, 0)), _full((1, d)), _full((d, d)), _full((d, d)),
                  _full((d, d))],
        out_specs=[pl.BlockSpec((d, tp), lambda i: (0, i)), pl.BlockSpec((tp, d), lambda i: (i, 0)),
                   pl.BlockSpec((d, tp), lambda i: (0, i))],
        out_shape=[jax.ShapeDtypeStruct((d, t), BF16), jax.ShapeDtypeStruct((t, d), BF16),
                   jax.ShapeDtypeStruct((d, t), BF16)],
        compiler_params=_params(("arbitrary",)),
        name="sb_qkv_proj",
    )(h, g.reshape(1, d), wqt, wk, wvt)

    bq = _pick(seq, 256)
    nq = seq // bq
    nh = SB_HEADS_PER_STEP
    grp = nh * dh
    nchain = 2 * nh
    o = pl.pallas_call(
        functools.partial(_sb_attn_kernel, bq=bq, dh=dh, nh=nh),
        grid=(batch, d // grp, nq),
        in_specs=[
            pl.BlockSpec((grp, bq), lambda b, p, q: (p, b * nq + q)),
            pl.BlockSpec((seq, grp), lambda b, p, q: (b, p)),
            pl.BlockSpec((grp, seq), lambda b, p, q: (p, b)),
        ],
        out_specs=pl.BlockSpec((bq, grp), lambda b, p, q: (b * nq + q, p)),
        out_shape=jax.ShapeDtypeStruct((t, d), BF16),
        scratch_shapes=[pltpu.VMEM((nchain, bq, bq), BF16), pltpu.VMEM((nchain, bq, bq), BF16),
                        pltpu.VMEM((nchain, bq, bq), F32), pltpu.VMEM((nchain, bq, bq), BF16)],
        compiler_params=_params(("arbitrary", "arbitrary", "arbitrary")),
        name="sb_attention",
    )(qt, k, vt)

    return pl.pallas_call(
        _sb_out_kernel,
        grid=(t // tp,),
        in_specs=[pl.BlockSpec((tp, d), lambda i: (i, 0)), pl.BlockSpec((tp, d), lambda i: (i, 0)),
                  _full((d, d))],
        out_specs=pl.BlockSpec((tp, d), lambda i: (i, 0)),
        out_shape=jax.ShapeDtypeStruct((t, d), F32),
        compiler_params=_params(("arbitrary",)),
        name="sb_out_proj",
    )(h, o, w_out.astype(BF16))


def _route_kernel(h_ref, g_ref, wr_hi_ref, wr_lo_ref, br_ref, idx_ref, gate_ref, *, n_exp, top_k):
    u = _rms(h_ref[...], g_ref[...])
    u_hi = u.astype(BF16)
    u_lo = (u - u_hi.astype(F32)).astype(BF16)
    logits = (jnp.dot(u_hi, wr_hi_ref[...], preferred_element_type=F32)
              + jnp.dot(u_hi, wr_lo_ref[...], preferred_element_type=F32)
              + jnp.dot(u_lo, wr_hi_ref[...], preferred_element_type=F32)) + br_ref[...]
    tm = logits.shape[0]
    lane = lax.broadcasted_iota(I32, (tm, n_exp), 1)
    kcol = lax.broadcasted_iota(I32, (tm, top_k), 1)
    vals = logits
    top_v, top_i = [], []
    for _ in range(top_k):
        m = jnp.max(vals, axis=-1, keepdims=True)
        sel = jnp.min(jnp.where(vals == m, lane, n_exp), axis=-1, keepdims=True)
        top_v.append(m)
        top_i.append(sel)
        vals = jnp.where(lane == sel, -jnp.inf, vals)
    exps = [jnp.exp(v - top_v[0]) for v in top_v]
    denom = exps[0]
    for e in exps[1:]:
        denom = denom + e
    idx = jnp.zeros((tm, top_k), I32)
    gates = jnp.zeros((tm, top_k), F32)
    for k in range(top_k):
        idx = jnp.where(kcol == k, top_i[k], idx)
        gates = jnp.where(kcol == k, exps[k] / denom, gates)
    idx_ref[...] = idx
    gate_ref[...] = gates


def _rank_kernel(idx_ref, idxt_ref, pos_ref, post_ref, gst_ref, off_ref, nch_ref, bexp_ref, meta_ref,
                 run_ref, pstart_ref, *, n_exp, top_k, bm, nbp):
    p = pl.program_id(0)
    i = pl.program_id(1)
    nt = pl.num_programs(1)
    tm = idx_ref.shape[0]
    cpb = bm // RUN_ALIGN
    idx = idx_ref[...]
    lane = lax.broadcasted_iota(I32, (tm, n_exp), 1)
    ohs = [(idx[:, k:k + 1] == lane).astype(F32) for k in range(top_k)]
    oh = ohs[0]
    for o in ohs[1:]:
        oh = oh + o
    cnt = jnp.sum(oh, axis=0, keepdims=True)
    nch = jnp.floor((cnt + (RUN_ALIGN - 1)) * (1.0 / RUN_ALIGN))
    er = lax.broadcasted_iota(I32, (n_exp, n_exp), 0)
    ec = lax.broadcasted_iota(I32, (n_exp, n_exp), 1)

    def excl_cumsum_row(v):
        v8 = jnp.broadcast_to(v, (SUBLANES, n_exp)).astype(BF16)
        return jnp.dot(v8, (er < ec).astype(BF16), preferred_element_type=F32)[0:1, :]

    @pl.when((p == 0) & (i == 0))
    def _():
        run_ref[...] = jnp.zeros_like(run_ref)

    @pl.when(p == 0)
    def _():
        run_ref[...] = run_ref[...] + nch

    @pl.when((p == 0) & (i == nt - 1))
    def _():
        tot = run_ref[...]
        nb = jnp.floor((tot + (cpb - 1)) * (1.0 / cpb))
        bstart = excl_cumsum_row(nb)
        bend = bstart + nb
        pstart_ref[...] = bstart * cpb
        blk = lax.broadcasted_iota(I32, (nbp, n_exp), 0).astype(F32)
        be = jnp.sum((bend <= blk).astype(F32), axis=-1, keepdims=True)
        bexp_ref[...] = jnp.minimum(be, n_exp - 1).astype(I32)
        n_used = jnp.sum(nb, axis=-1, keepdims=True)
        last = jnp.where(nb > 0, bend - 1.0, -1.0)
        mrow = lax.broadcasted_iota(I32, (SUBLANES, n_exp), 0)
        meta = jnp.where(mrow == 0, jnp.broadcast_to(n_used, (SUBLANES, n_exp)),
                         jnp.where(mrow == 1, jnp.broadcast_to(last, (SUBLANES, n_exp)), 0.0))
        meta_ref[...] = meta.astype(I32)
        run_ref[...] = jnp.zeros_like(run_ref)

    @pl.when(p == 1)
    def _():
        tr = lax.broadcasted_iota(I32, (tm, tm), 0)
        tc = lax.broadcasted_iota(I32, (tm, tm), 1)
        off = excl_cumsum_row(nch)
        excl = jnp.dot((tc < tr).astype(BF16), oh.astype(BF16), preferred_element_type=F32)
        base = excl + off * RUN_ALIGN
        kcol = lax.broadcasted_iota(I32, (tm, top_k), 1)
        pos = jnp.zeros((tm, top_k), I32)
        for k in range(top_k):
            pk = jnp.sum(ohs[k] * base, axis=-1, keepdims=True).astype(I32)
            pos = jnp.where(kcol == k, pk, pos)
        pos_ref[...] = pos
        idxt = idxt_ref[...]
        sub = lax.broadcasted_iota(I32, (n_exp, tm), 0)
        ohts = [(idxt[k:k + 1, :] == sub).astype(F32) for k in range(top_k)]
        oht = ohts[0]
        for o in ohts[1:]:
            oht = oht + o
        cnt_col = jnp.sum(oht, axis=1, keepdims=True)
        nch_col = jnp.floor((cnt_col + (RUN_ALIGN - 1)) * (1.0 / RUN_ALIGN))
        off_col = jnp.dot((ec < er).astype(BF16), jnp.broadcast_to(nch_col, (n_exp, LANES)).astype(BF16),
                          preferred_element_type=F32)[:, 0:1]
        exclt = jnp.dot(oht.astype(BF16), (tr < tc).astype(BF16), preferred_element_type=F32)
        baset = exclt + off_col * RUN_ALIGN
        krow = lax.broadcasted_iota(I32, (top_k, tm), 0)
        post = jnp.zeros((top_k, tm), I32)
        for k in range(top_k):
            pk = jnp.sum(ohts[k] * baset, axis=0, keepdims=True).astype(I32)
            post = jnp.where(krow == k, pk, post)
        post_ref[...] = post
        gst_ref[0] = (pstart_ref[...] + run_ref[...]).astype(I32)
        off_ref[0] = off.astype(I32)
        nch_ref[0] = nch.astype(I32)
        run_ref[...] = run_ref[...] + nch


def _run_chunks(gst_ref, off_ref, nch_ref, n_exp, make_copy):
    def per_expert(e, total):
        n = nch_ref[0, 0, e]
        g0 = gst_ref[0, 0, e]
        o0 = off_ref[0, 0, e]

        def per_chunk(j, c):
            make_copy(pl.multiple_of((o0 + j) * RUN_ALIGN, RUN_ALIGN),
                      pl.multiple_of((g0 + j) * RUN_ALIGN, RUN_ALIGN)).start()
            return c

        lax.fori_loop(0, n, per_chunk, 0)
        return total + n

    return lax.fori_loop(0, n_exp, per_expert, 0)


def _dispatch_kernel(last_ref, gst_ref, off_ref, nch_ref, h_ref, g_ref, post_ref, xs_ref, sbuf, zbuf, sem,
                     *, n_exp, top_k, bm):
    i = pl.program_id(0)
    rows = sbuf.shape[0]

    @pl.when(i == 0)
    def _():
        zbuf[...] = jnp.zeros_like(zbuf)

        def zero_copy(e):
            return pltpu.make_async_copy(zbuf, xs_ref.at[pl.ds(last_ref[e] * bm, bm)], sem)

        def start(e, c):
            @pl.when(last_ref[e] >= 0)
            def _():
                zero_copy(e).start()
            return c

        def wait(e, c):
            @pl.when(last_ref[e] >= 0)
            def _():
                zero_copy(e).wait()
            return c

        lax.fori_loop(0, n_exp, start, 0)
        lax.fori_loop(0, n_exp, wait, 0)

    u = _rms(h_ref[...], g_ref[...]).astype(BF16)
    tm = u.shape[0]
    post = post_ref[...]
    q = lax.broadcasted_iota(I32, (rows, tm), 0)
    sel = (q == post[0:1, :])
    for k in range(1, top_k):
        sel = sel | (q == post[k:k + 1, :])
    perm = jnp.where(sel, 1.0, 0.0).astype(BF16)
    sbuf[...] = jnp.dot(perm, u, preferred_element_type=F32)

    def make_copy(src_row, dst_row):
        return pltpu.make_async_copy(sbuf.at[pl.ds(src_row, RUN_ALIGN)], xs_ref.at[pl.ds(dst_row, RUN_ALIGN)], sem)

    n = _run_chunks(gst_ref, off_ref, nch_ref, n_exp, make_copy)

    def wait(j, c):
        make_copy(0, 0).wait()
        return c

    lax.fori_loop(0, n, wait, 0)


def _ffn_kernel(bexp_ref, nused_ref, x_ref, wgu_ref, bgu_ref, wd_ref, bd_ref, y_ref, wgu_bf, wd_bf, *, f):
    b = pl.program_id(0)
    prev = bexp_ref[jnp.maximum(b - 1, 0)]
    active = b < nused_ref[0]

    @pl.when(active & ((b == 0) | (bexp_ref[b] != prev)))
    def _():
        wgu_bf[...] = wgu_ref[0, 0].astype(BF16)
        wd_bf[...] = wd_ref[0, 0].astype(BF16)

    @pl.when(active)
    def _():
        x = x_ref[...].astype(BF16)
        gu = jnp.dot(x, wgu_bf[...], preferred_element_type=F32) + bgu_ref[0]
        gate = jnp.minimum(gu[:, :f], SWIGLU_LIMIT)
        up = jnp.clip(gu[:, f:], -SWIGLU_LIMIT, SWIGLU_LIMIT)
        glu = gate * _sigmoid(gate * SWIGLU_ALPHA)
        act = ((up + 1.0) * glu).astype(BF16)
        y_ref[...] = jnp.dot(act, wd_bf[...], preferred_element_type=F32) + bd_ref[0]


def _combine_kernel(gst_ref, off_ref, nch_ref, h_ref, gate_ref, pos_ref, ys_ref, fg_ref, o_ref, ybuf, sem,
                    *, n_exp, top_k, final):
    i = pl.program_id(0)
    rows = ybuf.shape[0]

    @pl.when(i == 0)
    def _():
        ybuf[...] = jnp.zeros_like(ybuf)

    def make_copy(dst_row, src_row):
        return pltpu.make_async_copy(ys_ref.at[pl.ds(src_row, RUN_ALIGN)], ybuf.at[pl.ds(dst_row, RUN_ALIGN)], sem)

    n = _run_chunks(gst_ref, off_ref, nch_ref, n_exp, make_copy)

    def wait(j, c):
        make_copy(0, 0).wait()
        return c

    lax.fori_loop(0, n, wait, 0)

    gates = gate_ref[...]
    pos = pos_ref[...]
    tm = gates.shape[0]
    q = lax.broadcasted_iota(I32, (tm, rows), 1)
    wsel = jnp.where(q == pos[:, 0:1], gates[:, 0:1], 0.0)
    for k in range(1, top_k):
        wsel = wsel + jnp.where(q == pos[:, k:k + 1], gates[:, k:k + 1], 0.0)
    w_hi = wsel.astype(BF16)
    w_lo = (wsel - w_hi.astype(F32)).astype(BF16)
    yb = ybuf[...].astype(BF16)
    out = (h_ref[...] + jnp.dot(w_hi, yb, preferred_element_type=F32)
           + jnp.dot(w_lo, yb, preferred_element_type=F32))
    if final:
        out = _rms(out, fg_ref[...])
    o_ref[...] = out


def _moe_layer(h, g, w_router, b_router, w_gate_up, b_gate_up, w_down, b_down, final_g, final, layer):
    t, d = h.shape
    n_exp = w_router.shape[1]
    f = w_down.shape[2]
    top_k = TOP_K
    bm = MOE_BLOCK
    ts = _pick(t, MOE_SORT_TILE)
    nt = t // ts
    srows = ts * top_k + n_exp * RUN_ALIGN
    n_blocks = -(-(t * top_k + nt * n_exp * (RUN_ALIGN - 1)) // bm) + n_exp
    n_slots = n_blocks * bm
    nbp = -(-n_blocks // SUBLANES) * SUBLANES

    tm = _pick(t, 512)
    wr_hi = w_router.astype(BF16)
    wr_lo = (w_router - wr_hi.astype(F32)).astype(BF16)
    idx, gates = pl.pallas_call(
        functools.partial(_route_kernel, n_exp=n_exp, top_k=top_k),
        grid=(t // tm,),
        in_specs=[pl.BlockSpec((tm, d), lambda i: (i, 0)), _full((1, d)), _full((d, n_exp)),
                  _full((d, n_exp)), _full((1, n_exp))],
        out_specs=[pl.BlockSpec((tm, top_k), lambda i: (i, 0)), pl.BlockSpec((tm, top_k), lambda i: (i, 0))],
        out_shape=[jax.ShapeDtypeStruct((t, top_k), I32), jax.ShapeDtypeStruct((t, top_k), F32)],
        compiler_params=_params(("arbitrary",)),
        name="moe_route",
    )(h, g.reshape(1, d), wr_hi, wr_lo, b_router.reshape(1, n_exp))

    tab = jax.ShapeDtypeStruct((nt, 1, n_exp), I32)
    tab_spec = pl.BlockSpec((1, 1, n_exp), lambda p, i: (i * p, 0, 0))
    pos, post, gst, off, nch, bexp, meta = pl.pallas_call(
        functools.partial(_rank_kernel, n_exp=n_exp, top_k=top_k, bm=bm, nbp=nbp),
        grid=(2, nt),
        in_specs=[pl.BlockSpec((ts, top_k), lambda p, i: (i, 0)), pl.BlockSpec((top_k, ts), lambda p, i: (0, i))],
        out_specs=[pl.BlockSpec((ts, top_k), lambda p, i: (i * p, 0)),
                   pl.BlockSpec((top_k, ts), lambda p, i: (0, i * p)),
                   tab_spec, tab_spec, tab_spec, _full((nbp, 1)), _full((SUBLANES, n_exp))],
        out_shape=[jax.ShapeDtypeStruct((t, top_k), I32), jax.ShapeDtypeStruct((top_k, t), I32), tab, tab, tab,
                   jax.ShapeDtypeStruct((nbp, 1), I32), jax.ShapeDtypeStruct((SUBLANES, n_exp), I32)],
        scratch_shapes=[pltpu.VMEM((1, n_exp), F32), pltpu.VMEM((1, n_exp), F32)],
        compiler_params=_params(("arbitrary", "arbitrary")),
        name="moe_rank",
    )(idx, idx.T)
    bexp = bexp.reshape(nbp)
    n_used = meta[0, 0:1]
    last_blk = meta[1]

    def smem_tab():
        return pl.BlockSpec((1, 1, n_exp), lambda i, *_: (i, 0, 0), memory_space=pltpu.SMEM)

    xs = pl.pallas_call(
        functools.partial(_dispatch_kernel, n_exp=n_exp, top_k=top_k, bm=bm),
        grid_spec=pltpu.PrefetchScalarGridSpec(
            num_scalar_prefetch=1,
            grid=(nt,),
            in_specs=[
                smem_tab(), smem_tab(), smem_tab(),
                pl.BlockSpec((ts, d), lambda i, last: (i, 0)),
                pl.BlockSpec((1, d), lambda i, last: (0, 0)),
                pl.BlockSpec((top_k, ts), lambda i, last: (0, i)),
            ],
            out_specs=pl.BlockSpec(memory_space=pl.ANY),
            scratch_shapes=[pltpu.VMEM((srows, d), F32), pltpu.VMEM((bm, d), F32), pltpu.SemaphoreType.DMA(())],
        ),
        out_shape=jax.ShapeDtypeStruct((n_slots, d), F32),
        compiler_params=_params(("arbitrary",)),
        name="moe_dispatch",
    )(last_blk, gst, off, nch, h, g.reshape(1, d), post)

    def blk(b, bexp_ref, nused_ref):
        return jnp.minimum(b, nused_ref[0] - 1)

    ys = pl.pallas_call(
        functools.partial(_ffn_kernel, f=f),
        grid_spec=pltpu.PrefetchScalarGridSpec(
            num_scalar_prefetch=2,
            grid=(n_blocks,),
            in_specs=[
                pl.BlockSpec((bm, d), lambda b, be, nu: (blk(b, be, nu), 0)),
                pl.BlockSpec((1, 1, d, 2 * f), lambda b, be, nu: (layer, be[blk(b, be, nu)], 0, 0)),
                pl.BlockSpec((1, 1, 2 * f), lambda b, be, nu: (be[blk(b, be, nu)], 0, 0)),
                pl.BlockSpec((1, 1, f, d), lambda b, be, nu: (layer, be[blk(b, be, nu)], 0, 0)),
                pl.BlockSpec((1, 1, d), lambda b, be, nu: (be[blk(b, be, nu)], 0, 0)),
            ],
            out_specs=pl.BlockSpec((bm, d), lambda b, be, nu: (blk(b, be, nu), 0)),
            scratch_shapes=[pltpu.VMEM((d, 2 * f), BF16), pltpu.VMEM((f, d), BF16)],
        ),
        out_shape=jax.ShapeDtypeStruct((n_slots, d), F32),
        compiler_params=_params(("arbitrary",)),
        name="moe_ffn",
    )(bexp, n_used, xs, w_gate_up, b_gate_up.reshape(n_exp, 1, 2 * f), w_down, b_down.reshape(n_exp, 1, d))

    return pl.pallas_call(
        functools.partial(_combine_kernel, n_exp=n_exp, top_k=top_k, final=final),
        grid=(nt,),
        in_specs=[
            smem_tab(), smem_tab(), smem_tab(),
            pl.BlockSpec((ts, d), lambda i: (i, 0)),
            pl.BlockSpec((ts, top_k), lambda i: (i, 0)),
            pl.BlockSpec((ts, top_k), lambda i: (i, 0)),
            pl.BlockSpec(memory_space=pl.ANY),
            _full((1, d)),
        ],
        out_specs=pl.BlockSpec((ts, d), lambda i: (i, 0)),
        out_shape=jax.ShapeDtypeStruct((t, d), F32),
        scratch_shapes=[pltpu.VMEM((srows, d), F32), pltpu.SemaphoreType.DMA(())],
        compiler_params=_params(("arbitrary",)),
        name="moe_combine",
    )(gst, off, nch, h, gates, pos, ys, final_g.reshape(1, d))


def kernel(x, mix_norm, ffn_norm, final_norm, lru_w_in, lru_conv_w, lru_conv_b, lru_w_a, lru_b_a, lru_w_x, lru_b_x, lru_a_param, lru_w_out, pool_w_in, pool_w_group, pool_scale, pool_w_out, sb_w_qkv, sb_w_out, moe_w_router, moe_b_router, moe_w_gate_up, moe_b_gate_up, moe_w_down, moe_b_down):
    batch, seq, d = x.shape
    depth = mix_norm.shape[0]
    h = x.reshape(batch * seq, d)
    for layer in range(depth):
        kind = layer % N_MIXERS
        slot = layer // N_MIXERS
        if kind == 0:
            h = _lru_layer(h, batch, seq, mix_norm[layer], lru_w_in[slot], lru_conv_w[slot], lru_conv_b[slot],
                           lru_w_a[slot], lru_b_a[slot], lru_w_x[slot], lru_b_x[slot], lru_a_param[slot],
                           lru_w_out[slot])
        elif kind == 1:
            h = _pool_layer(h, batch, seq, mix_norm[layer], pool_w_in[slot], pool_w_group[slot], pool_scale[slot],
                            pool_w_out[slot])
        else:
            h = _sb_layer(h, batch, seq, mix_norm[layer], sb_w_qkv[slot], sb_w_out[slot])
        h = _moe_layer(h, ffn_norm[layer], moe_w_router[layer], moe_b_router[layer], moe_w_gate_up,
                       moe_b_gate_up[layer], moe_w_down, moe_b_down[layer], final_norm,
                       final=(layer == depth - 1), layer=layer)
    return h.reshape(batch, seq, d)
```

```python
import functools
import math

import jax
import jax.numpy as jnp
from jax import lax
from jax.experimental import pallas as pl
from jax.experimental.pallas import tpu as pltpu

F32 = jnp.float32
BF16 = jnp.bfloat16
I32 = jnp.int32

RMS_EPS = 1e-6
LRU_C = 8.0
POOL_WINDOWS = (2, 4, 8, 16)
SB_HEADS = 16
SB_HEADS_PER_STEP = 4
LOG2E = 1.4426950408889634
TOP_K = 4
SWIGLU_LIMIT = 7.0
SWIGLU_ALPHA = 1.702
N_MIXERS = 3

V7X_VMEM_BYTES = 64 * 1024 * 1024
VMEM_LIMIT_BYTES = V7X_VMEM_BYTES - 8 * 1024 * 1024
SUBLANES = 8
LANES = 128

MOE_BLOCK = 512
MOE_SORT_TILE = 256
RUN_ALIGN = SUBLANES


def _params(semantics, vmem=VMEM_LIMIT_BYTES):
    return pltpu.CompilerParams(dimension_semantics=semantics, vmem_limit_bytes=vmem)


def _pick(n, pref):
    t = min(n, pref)
    while n % t:
        t //= 2
    return t


def _rms(h, g):
    ms = jnp.mean(h * h, axis=-1, keepdims=True)
    return h * lax.rsqrt(ms + RMS_EPS) * g


def _sigmoid(x):
    return 1.0 / (1.0 + jnp.exp(-x))


def _softplus(x):
    return jnp.maximum(x, 0.0) + jnp.log(1.0 + jnp.exp(-jnp.abs(x)))


def _gelu_tanh(x):
    c = math.sqrt(2.0 / math.pi)
    return 0.5 * x * (1.0 + jnp.tanh(c * (x + 0.044715 * (x * x * x))))


def _full(shape):
    n = len(shape)
    return pl.BlockSpec(shape, lambda *_: (0,) * n)


def _lru_kernel(h_ref, g_ref, win_ref, cw_ref, cb_ref, wax_ref, ba_ref, bx_ref, ap_ref, wout_ref,
                o_ref, hcar_ref, tail_ref, y_ref, *, ts, w, bw):
    s = pl.program_id(1)

    @pl.when(s == 0)
    def _():
        hcar_ref[...] = jnp.zeros_like(hcar_ref)
        tail_ref[...] = jnp.zeros_like(tail_ref)

    h = h_ref[...]
    u = _rms(h, g_ref[...]).astype(BF16)
    proj = jnp.dot(u, win_ref[...], preferred_element_type=F32)
    row8 = lax.broadcasted_iota(I32, (SUBLANES, bw), 0)
    row = lax.broadcasted_iota(I32, (ts, bw), 0)
    kw = cw_ref.shape[0]

    for c in range(w // bw):
        lo, hi = c * bw, (c + 1) * bw
        gate = proj[:, lo:hi]
        xb = proj[:, w + lo:w + hi]
        tail = tail_ref[:, lo:hi]
        xc = cb_ref[:, lo:hi] + xb * cw_ref[kw - 1:kw, lo:hi]
        for j in range(1, kw):
            rolled = pltpu.roll(xb, j, axis=0)
            first = jnp.where(row8 < j, pltpu.roll(tail, j, axis=0), rolled[0:SUBLANES])
            shifted = jnp.concatenate([first, rolled[SUBLANES:]], axis=0)
            xc = xc + shifted * cw_ref[kw - 1 - j:kw - j, lo:hi]
        tail_ref[:, lo:hi] = xb[ts - SUBLANES:ts]

        res = jnp.dot(xc.astype(BF16), wax_ref[c], preferred_element_type=F32)
        r = _sigmoid(res[:, :bw] + ba_ref[:, lo:hi])
        i = _sigmoid(res[:, bw:] + bx_ref[:, lo:hi])
        log_a = (-LRU_C * _softplus(-ap_ref[:, lo:hi])) * r
        a = jnp.exp(log_a)
        b = jnp.sqrt(-jnp.tanh(log_a) * (a * a + 1.0)) * (i * xc)
        d = 1
        while d < ts:
            keep = row >= d
            a_sh = jnp.where(keep, pltpu.roll(a, d, axis=0), 1.0)
            b_sh = jnp.where(keep, pltpu.roll(b, d, axis=0), 0.0)
            b = a * b_sh + b
            a = a * a_sh
            d *= 2
        hs = a * hcar_ref[:, lo:hi] + b
        hcar_ref[:, lo:hi] = hs[ts - 1:ts]
        y_ref[:, lo:hi] = (hs * _gelu_tanh(gate)).astype(BF16)

    o_ref[...] = h + jnp.dot(y_ref[...], wout_ref[...], preferred_element_type=F32)


def _lru_layer(h, batch, seq, g, w_in, conv_w, conv_b, w_a, b_a, w_x, b_x, a_param, w_out):
    t, d = h.shape
    w = w_in.shape[1] // 2
    nblk, bw, _ = w_a.shape
    ts = _pick(seq, 256)
    ns = seq // ts
    wax = jnp.concatenate([w_a, w_x], axis=-1).astype(BF16)
    row = lambda v: v.reshape(1, -1)
    kern = functools.partial(_lru_kernel, ts=ts, w=w, bw=bw)
    return pl.pallas_call(
        kern,
        grid=(batch, ns),
        in_specs=[
            pl.BlockSpec((ts, d), lambda b, s: (b * ns + s, 0)),
            _full((1, d)), _full((d, 2 * w)), _full(conv_w.shape), _full((1, w)),
            _full(wax.shape), _full((1, w)), _full((1, w)), _full((1, w)), _full((w, d)),
        ],
        out_specs=pl.BlockSpec((ts, d), lambda b, s: (b * ns + s, 0)),
        out_shape=jax.ShapeDtypeStruct((t, d), F32),
        scratch_shapes=[pltpu.VMEM((1, w), F32), pltpu.VMEM((SUBLANES, w), F32), pltpu.VMEM((ts, w), BF16)],
        compiler_params=_params(("arbitrary", "arbitrary")),
        name="lru_mixer",
    )(h, row(g), w_in.astype(BF16), conv_w, row(conv_b), wax, row(b_a), row(b_x), row(a_param),
      w_out.astype(BF16))


def _pool_kernel(h_ref, g_ref, win_ref, wg_ref, sc_ref, wout_ref, o_ref, ext_ref, m_ref, *, ts, d, wins):
    s = pl.program_id(1)
    pad = wins[-1]
    gw = d // len(wins)

    @pl.when(s == 0)
    def _():
        ext_ref[0:pad, :] = jnp.zeros((pad, d), F32)

    h = h_ref[...]
    u = _rms(h, g_ref[...]).astype(BF16)
    v = jnp.dot(u, win_ref[...], preferred_element_type=F32)
    ext_ref[pad:pad + ts, :] = v
    n_avail = (s * ts + lax.broadcasted_iota(I32, (ts, 1), 0) + 1).astype(F32)

    cur = ext_ref[...]
    width = 1
    for gi, win in enumerate(wins):
        while width < win:
            cur = cur + pltpu.roll(cur, width, axis=0)
            width *= 2
        lo, hi = gi * gw, (gi + 1) * gw
        mean = cur[pad:, 0:gw] / jnp.minimum(n_avail, float(win))
        pooled = mean - v[:, lo:hi]
        mixed = jnp.dot(pooled.astype(BF16), wg_ref[gi], preferred_element_type=F32)
        m_ref[:, lo:hi] = (mixed * sc_ref[:, lo:hi]).astype(BF16)
        cur = cur[:, gw:]
    ext_ref[0:pad, :] = v[ts - pad:ts]
    o_ref[...] = h + jnp.dot(m_ref[...], wout_ref[...], preferred_element_type=F32)


def _pool_layer(h, batch, seq, g, w_in, w_group, scale, w_out):
    t, d = h.shape
    ts = _pick(seq, 256)
    ns = seq // ts
    pad = POOL_WINDOWS[-1]
    kern = functools.partial(_pool_kernel, ts=ts, d=d, wins=POOL_WINDOWS)
    return pl.pallas_call(
        kern,
        grid=(batch, ns),
        in_specs=[
            pl.BlockSpec((ts, d), lambda b, s: (b * ns + s, 0)),
            _full((1, d)), _full((d, d)), _full(w_group.shape), _full((1, d)), _full((d, d)),
        ],
        out_specs=pl.BlockSpec((ts, d), lambda b, s: (b * ns + s, 0)),
        out_shape=jax.ShapeDtypeStruct((t, d), F32),
        scratch_shapes=[pltpu.VMEM((ts + pad, d), F32), pltpu.VMEM((ts, d), BF16)],
        compiler_params=_params(("arbitrary", "arbitrary")),
        name="pool_mixer",
    )(h, g.reshape(1, d), w_in.astype(BF16), w_group.astype(BF16), scale.reshape(1, d), w_out.astype(BF16))


def _sb_proj_kernel(h_ref, g_ref, wqt_ref, wk_ref, wvt_ref, qt_ref, k_ref, vt_ref):
    u = _rms(h_ref[...], g_ref[...]).astype(BF16)
    nt = (((1,), (1,)), ((), ()))
    qt_ref[...] = lax.dot_general(wqt_ref[...], u, nt, preferred_element_type=F32).astype(BF16)
    k_ref[...] = jnp.dot(u, wk_ref[...], preferred_element_type=F32).astype(BF16)
    vt_ref[...] = lax.dot_general(wvt_ref[...], u, nt, preferred_element_type=F32).astype(BF16)


def _sb_attn_kernel(qt_ref, k_ref, vt_ref, o_ref, lhi_ref, llo_ref, zs_ref, a_ref, *, bq, dh, nh):
    qb = pl.program_id(2)
    krow = lax.broadcasted_iota(I32, (bq, bq), 0)
    qcol = lax.broadcasted_iota(I32, (bq, bq), 1)
    upper = (qcol > krow).astype(BF16)
    causal = krow < qcol

    def step(chains, masked, st):
        totals = []
        for c, (hh, j) in enumerate(chains):
            qh = qt_ref[hh * dh:(hh + 1) * dh, :]
            kh = k_ref[pl.ds(j * bq, bq), hh * dh:(hh + 1) * dh]
            z = jnp.dot(kh, qh, preferred_element_type=F32)
            t = jnp.exp2(jnp.abs(z) * (-LOG2E))
            l1m = jnp.minimum(-z, 0.0) - jnp.log(1.0 + t)
            zs_ref[c] = z + l1m
            lm = jnp.where(causal, l1m, 0.0) if masked else l1m
            l_hi = lm.astype(BF16)
            lhi_ref[c] = l_hi
            llo_ref[c] = (lm - l_hi.astype(F32)).astype(BF16)
            totals.append(jnp.sum(lm, axis=0, keepdims=True))
        st = list(st)
        for c, (hh, j) in enumerate(chains):
            acc, carry = st[hh]
            suffix = (jnp.dot(upper, lhi_ref[c], preferred_element_type=F32)
                      + jnp.dot(upper, llo_ref[c], preferred_element_type=F32))
            a = jnp.exp(zs_ref[c] + suffix + carry)
            if masked:
                a = jnp.where(causal, a, 0.0)
            a_ref[c] = a.astype(BF16)
            st[hh] = (acc, carry + totals[c])
        for c, (hh, j) in enumerate(chains):
            acc, carry = st[hh]
            vth = vt_ref[hh * dh:(hh + 1) * dh, pl.ds(j * bq, bq)]
            st[hh] = (acc + jnp.dot(vth, a_ref[c], preferred_element_type=F32), carry)
        return st

    heads = range(nh)
    zero = (jnp.zeros((dh, bq), F32), jnp.zeros((1, bq), F32))
    st = step([(hh, qb) for hh in heads], True, [zero] * nh)

    def pair_body(i, st):
        j = qb - 1 - 2 * i
        return step([(hh, j) for hh in heads] + [(hh, j - 1) for hh in heads], False, st)

    def single_body(i, st):
        return step([(hh, 0) for hh in heads], False, st)

    st = lax.fori_loop(0, lax.shift_right_logical(qb, 1), pair_body, st)
    st = lax.fori_loop(0, qb & 1, single_body, st)
    o_ref[...] = jnp.concatenate([st[hh][0].T for hh in heads], axis=1).astype(BF16)


def _sb_out_kernel(h_ref, o_ref, w_ref, out_ref):
    out_ref[...] = h_ref[...] + jnp.dot(o_ref[...], w_ref[...], preferred_element_type=F32)


def _sb_layer(h, batch, seq, g, w_qkv, w_out):
    t, d = h.shape
    dh = d // SB_HEADS
    scale = 1.0 / math.sqrt(dh)
    wqt = (w_qkv[:, :d] * scale).T.astype(BF16)
    wk = w_qkv[:, d:2 * d].astype(BF16)
    wvt = w_qkv[:, 2 * d:].T.astype(BF16)
    tp = _pick(t, 512)
    qt, k, vt = pl.pallas_call(
        _sb_proj_kernel,
        grid=(t // tp,),
        in_specs=[pl.BlockSpec((tp, d), lambda i: (i, 0)), _full((1, d)), _full((d, d)), _full((d, d)),
                  _full((d, d))],
        out_specs=[pl.BlockSpec((d, tp), lambda i: (0, i)), pl.BlockSpec((tp, d), lambda i: (i, 0)),
                   pl.BlockSpec((d, tp), lambda i: (0, i))],
        out_shape=[jax.ShapeDtypeStruct((d, t), BF16), jax.ShapeDtypeStruct((t, d), BF16),
                   jax.ShapeDtypeStruct((d, t), BF16)],
        compiler_params=_params(("arbitrary",)),
        name="sb_qkv_proj",
    )(h, g.reshape(1, d), wqt, wk, wvt)

    bq = _pick(seq, 256)
    nq = seq // bq
    nh = SB_HEADS_PER_STEP
    grp = nh * dh
    nchain = 2 * nh
    o = pl.pallas_call(
        functools.partial(_sb_attn_kernel, bq=bq, dh=dh, nh=nh),
        grid=(batch, d // grp, nq),
        in_specs=[
            pl.BlockSpec((grp, bq), lambda b, p, q: (p, b * nq + q)),
            pl.BlockSpec((seq, grp), lambda b, p, q: (b, p)),
            pl.BlockSpec((grp, seq), lambda b, p, q: (p, b)),
        ],
        out_specs=pl.BlockSpec((bq, grp), lambda b, p, q: (b * nq + q, p)),
        out_shape=jax.ShapeDtypeStruct((t, d), BF16),
        scratch_shapes=[pltpu.VMEM((nchain, bq, bq), BF16), pltpu.VMEM((nchain, bq, bq), BF16),
                        pltpu.VMEM((nchain, bq, bq), F32), pltpu.VMEM((nchain, bq, bq), BF16)],
        compiler_params=_params(("arbitrary", "arbitrary", "arbitrary")),
        name="sb_attention",
    )(qt, k, vt)

    return pl.pallas_call(
        _sb_out_kernel,
        grid=(t // tp,),
        in_specs=[pl.BlockSpec((tp, d), lambda i: (i, 0)), pl.BlockSpec((tp, d), lambda i: (i, 0)),
                  _full((d, d))],
        out_specs=pl.BlockSpec((tp, d), lambda i: (i, 0)),
        out_shape=jax.ShapeDtypeStruct((t, d), F32),
        compiler_params=_params(("arbitrary",)),
        name="sb_out_proj",
    )(h, o, w_out.astype(BF16))


def _route_kernel(h_ref, g_ref, wr_hi_ref, wr_lo_ref, br_ref, idx_ref, gate_ref, *, n_exp, top_k):
    u = _rms(h_ref[...], g_ref[...])
    u_hi = u.astype(BF16)
    u_lo = (u - u_hi.astype(F32)).astype(BF16)
    logits = (jnp.dot(u_hi, wr_hi_ref[...], preferred_element_type=F32)
              + jnp.dot(u_hi, wr_lo_ref[...], preferred_element_type=F32)
              + jnp.dot(u_lo, wr_hi_ref[...], preferred_element_type=F32)) + br_ref[...]
    tm = logits.shape[0]
    lane = lax.broadcasted_iota(I32, (tm, n_exp), 1)
    kcol = lax.broadcasted_iota(I32, (tm, top_k), 1)
    vals = logits
    top_v, top_i = [], []
    for _ in range(top_k):
        m = jnp.max(vals, axis=-1, keepdims=True)
        sel = jnp.min(jnp.where(vals == m, lane, n_exp), axis=-1, keepdims=True)
        top_v.append(m)
        top_i.append(sel)
        vals = jnp.where(lane == sel, -jnp.inf, vals)
    exps = [jnp.exp(v - top_v[0]) for v in top_v]
    denom = exps[0]
    for e in exps[1:]:
        denom = denom + e
    idx = jnp.zeros((tm, top_k), I32)
    gates = jnp.zeros((tm, top_k), F32)
    for k in range(top_k):
        idx = jnp.where(kcol == k, top_i[k], idx)
        gates = jnp.where(kcol == k, exps[k] / denom, gates)
    idx_ref[...] = idx
    gate_ref[...] = gates


def _rank_kernel(idx_ref, idxt_ref, pos_ref, post_ref, gst_ref, off_ref, nch_ref, bexp_ref, nxt_ref, meta_ref,
                 run_ref, pstart_ref, *, n_exp, top_k, bm, nbp):
    p = pl.program_id(0)
    i = pl.program_id(1)
    nt = pl.num_programs(1)
    tm = idx_ref.shape[0]
    cpb = bm // RUN_ALIGN
    idx = idx_ref[...]
    lane = lax.broadcasted_iota(I32, (tm, n_exp), 1)
    ohs = [(idx[:, k:k + 1] == lane).astype(F32) for k in range(top_k)]
    oh = ohs[0]
    for o in ohs[1:]:
        oh = oh + o
    cnt = jnp.sum(oh, axis=0, keepdims=True)
    nch = jnp.floor((cnt + (RUN_ALIGN - 1)) * (1.0 / RUN_ALIGN))
    er = lax.broadcasted_iota(I32, (n_exp, n_exp), 0)
    ec = lax.broadcasted_iota(I32, (n_exp, n_exp), 1)

    def excl_cumsum_row(v):
        v8 = jnp.broadcast_to(v, (SUBLANES, n_exp)).astype(BF16)
        return jnp.dot(v8, (er < ec).astype(BF16), preferred_element_type=F32)[0:1, :]

    @pl.when((p == 0) & (i == 0))
    def _():
        run_ref[...] = jnp.zeros_like(run_ref)

    @pl.when(p == 0)
    def _():
        run_ref[...] = run_ref[...] + nch

    @pl.when((p == 0) & (i == nt - 1))
    def _():
        tot = run_ref[...]
        nb = jnp.floor((tot + (cpb - 1)) * (1.0 / cpb))
        bstart = excl_cumsum_row(nb)
        bend = bstart + nb
        pstart_ref[...] = bstart * cpb
        blk = lax.broadcasted_iota(I32, (nbp, n_exp), 0).astype(F32)
        be = jnp.sum((bend <= blk).astype(F32), axis=-1, keepdims=True)
        be = jnp.minimum(be, n_exp - 1)
        bexp_ref[...] = be.astype(I32)
        n_used = jnp.sum(nb, axis=-1, keepdims=True)
        lane_e = lax.broadcasted_iota(I32, (nbp, n_exp), 1).astype(F32)
        run_end = jnp.sum(jnp.where(lane_e == be, bend, 0.0), axis=-1, keepdims=True)
        nx = jnp.minimum(jnp.sum((bend <= run_end).astype(F32), axis=-1, keepdims=True), n_exp - 1)
        nxt_ref[...] = jnp.where(run_end < n_used, nx, -1.0).astype(I32)
        last = jnp.where(nb > 0, bend - 1.0, -1.0)
        mrow = lax.broadcasted_iota(I32, (SUBLANES, n_exp), 0)
        meta = jnp.where(mrow == 0, jnp.broadcast_to(n_used, (SUBLANES, n_exp)),
                         jnp.where(mrow == 1, jnp.broadcast_to(last, (SUBLANES, n_exp)), 0.0))
        meta_ref[...] = meta.astype(I32)
        run_ref[...] = jnp.zeros_like(run_ref)

    @pl.when(p == 1)
    def _():
        tr = lax.broadcasted_iota(I32, (tm, tm), 0)
        tc = lax.broadcasted_iota(I32, (tm, tm), 1)
        off = excl_cumsum_row(nch)
        excl = jnp.dot((tc < tr).astype(BF16), oh.astype(BF16), preferred_element_type=F32)
        base = excl + off * RUN_ALIGN
        kcol = lax.broadcasted_iota(I32, (tm, top_k), 1)
        pos = jnp.zeros((tm, top_k), I32)
        for k in range(top_k):
            pk = jnp.sum(ohs[k] * base, axis=-1, keepdims=True).astype(I32)
            pos = jnp.where(kcol == k, pk, pos)
        pos_ref[...] = pos
        idxt = idxt_ref[...]
        sub = lax.broadcasted_iota(I32, (n_exp, tm), 0)
        ohts = [(idxt[k:k + 1, :] == sub).astype(F32) for k in range(top_k)]
        oht = ohts[0]
        for o in ohts[1:]:
            oht = oht + o
        cnt_col = jnp.sum(oht, axis=1, keepdims=True)
        nch_col = jnp.floor((cnt_col + (RUN_ALIGN - 1)) * (1.0 / RUN_ALIGN))
        off_col = jnp.dot((ec < er).astype(BF16), jnp.broadcast_to(nch_col, (n_exp, LANES)).astype(BF16),
                          preferred_element_type=F32)[:, 0:1]
        exclt = jnp.dot(oht.astype(BF16), (tr < tc).astype(BF16), preferred_element_type=F32)
        baset = exclt + off_col * RUN_ALIGN
        krow = lax.broadcasted_iota(I32, (top_k, tm), 0)
        post = jnp.zeros((top_k, tm), I32)
        for k in range(top_k):
            pk = jnp.sum(ohts[k] * baset, axis=0, keepdims=True).astype(I32)
            post = jnp.where(krow == k, pk, post)
        post_ref[...] = post
        gst_ref[0] = (pstart_ref[...] + run_ref[...]).astype(I32)
        off_ref[0] = off.astype(I32)
        nch_ref[0] = nch.astype(I32)
        run_ref[...] = run_ref[...] + nch


def _run_chunks(gst_ref, off_ref, nch_ref, n_exp, make_copy):
    def per_expert(e, total):
        n = nch_ref[0, 0, e]
        g0 = gst_ref[0, 0, e]
        o0 = off_ref[0, 0, e]

        def per_chunk(j, c):
            make_copy(pl.multiple_of((o0 + j) * RUN_ALIGN, RUN_ALIGN),
                      pl.multiple_of((g0 + j) * RUN_ALIGN, RUN_ALIGN)).start()
            return c

        lax.fori_loop(0, n, per_chunk, 0)
        return total + n

    return lax.fori_loop(0, n_exp, per_expert, 0)


def _dispatch_kernel(last_ref, gst_ref, off_ref, nch_ref, h_ref, g_ref, post_ref, xs_ref, sbuf, zbuf, pending, sem,
                     zsem, *, n_exp, top_k, bm):
    i = pl.program_id(0)
    nt = pl.num_programs(0)
    slot = i % 2
    rows = sbuf.shape[1]

    @pl.when(i == 0)
    def _():
        zbuf[...] = jnp.zeros_like(zbuf)

        def zero_copy(e):
            return pltpu.make_async_copy(zbuf, xs_ref.at[pl.ds(last_ref[e] * bm, bm)], zsem)

        def start(e, c):
            @pl.when(last_ref[e] >= 0)
            def _():
                zero_copy(e).start()
            return c

        def wait(e, c):
            @pl.when(last_ref[e] >= 0)
            def _():
                zero_copy(e).wait()
            return c

        lax.fori_loop(0, n_exp, start, 0)
        lax.fori_loop(0, n_exp, wait, 0)

    u = _rms(h_ref[...], g_ref[...]).astype(BF16)
    tm = u.shape[0]
    post = post_ref[...]
    q = lax.broadcasted_iota(I32, (rows, tm), 0)
    sel = (q == post[0:1, :])
    for k in range(1, top_k):
        sel = sel | (q == post[k:k + 1, :])
    perm = jnp.where(sel, 1.0, 0.0).astype(BF16)
    sbuf[slot] = jnp.dot(perm, u, preferred_element_type=F32)

    def make_copy(s, src_row, dst_row):
        return pltpu.make_async_copy(sbuf.at[s, pl.ds(src_row, RUN_ALIGN)], xs_ref.at[pl.ds(dst_row, RUN_ALIGN)],
                                     sem.at[s])

    def wait_all(s, n):
        def wait(j, c):
            make_copy(s, 0, 0).wait()
            return c

        lax.fori_loop(0, n, wait, 0)

    n = _run_chunks(gst_ref, off_ref, nch_ref, n_exp, functools.partial(make_copy, slot))

    @pl.when(i > 0)
    def _():
        wait_all(1 - slot, pending[0])

    pending[0] = n

    @pl.when(i == nt - 1)
    def _():
        wait_all(slot, n)


def _ffn_kernel(bexp_ref, nxt_ref, nused_ref, x_ref, wgu_hbm, bgu_ref, wd_hbm, bd_ref, y_ref, wgu_st, wd_st,
                wgu_bf, wd_bf, sem, *, f, layer):
    b = pl.program_id(0)
    prev = bexp_ref[jnp.maximum(b - 1, 0)]
    active = b < nused_ref[0]

    def fetch(e):
        return (pltpu.make_async_copy(wgu_hbm.at[layer, e], wgu_st, sem.at[0]),
                pltpu.make_async_copy(wd_hbm.at[layer, e], wd_st, sem.at[1]))

    @pl.when(b == 0)
    def _():
        for c in fetch(bexp_ref[0]):
            c.start()

    @pl.when(active & ((b == 0) | (bexp_ref[b] != prev)))
    def _():
        for c in fetch(bexp_ref[b]):
            c.wait()
        wgu_bf[...] = wgu_st[...].astype(BF16)
        wd_bf[...] = wd_st[...].astype(BF16)

        @pl.when(nxt_ref[b] >= 0)
        def _():
            for c in fetch(nxt_ref[b]):
                c.start()

    @pl.when(active)
    def _():
        x = x_ref[...].astype(BF16)
        gu = jnp.dot(x, wgu_bf[...], preferred_element_type=F32) + bgu_ref[0]
        gate = jnp.minimum(gu[:, :f], SWIGLU_LIMIT)
        up = jnp.clip(gu[:, f:], -SWIGLU_LIMIT, SWIGLU_LIMIT)
        glu = gate * _sigmoid(gate * SWIGLU_ALPHA)
        act = ((up + 1.0) * glu).astype(BF16)
        y_ref[...] = jnp.dot(act, wd_bf[...], preferred_element_type=F32) + bd_ref[0]


def _combine_kernel(gst_ref, off_ref, nch_ref, gst_nx, off_nx, nch_nx, h_ref, gate_ref, pos_ref, ys_ref, fg_ref,
                    o_ref, ybuf, sem, *, n_exp, top_k, final):
    i = pl.program_id(0)
    nt = pl.num_programs(0)
    slot = i % 2
    rows = ybuf.shape[1]

    def make_copy(s, dst_row, src_row):
        return pltpu.make_async_copy(ys_ref.at[pl.ds(src_row, RUN_ALIGN)], ybuf.at[s, pl.ds(dst_row, RUN_ALIGN)],
                                     sem.at[s])

    @pl.when(i == 0)
    def _():
        ybuf[...] = jnp.zeros_like(ybuf)
        _run_chunks(gst_ref, off_ref, nch_ref, n_exp, functools.partial(make_copy, 0))

    @pl.when(i + 1 < nt)
    def _():
        _run_chunks(gst_nx, off_nx, nch_nx, n_exp, functools.partial(make_copy, 1 - slot))

    n = lax.fori_loop(0, n_exp, lambda e, c: c + nch_ref[0, 0, e], 0)

    def wait(j, c):
        make_copy(slot, 0, 0).wait()
        return c

    lax.fori_loop(0, n, wait, 0)

    gates = gate_ref[...]
    pos = pos_ref[...]
    tm = gates.shape[0]
    q = lax.broadcasted_iota(I32, (tm, rows), 1)
    wsel = jnp.where(q == pos[:, 0:1], gates[:, 0:1], 0.0)
    for k in range(1, top_k):
        wsel = wsel + jnp.where(q == pos[:, k:k + 1], gates[:, k:k + 1], 0.0)
    w_hi = wsel.astype(BF16)
    w_lo = (wsel - w_hi.astype(F32)).astype(BF16)
    yb = ybuf[slot].astype(BF16)
    out = (h_ref[...] + jnp.dot(w_hi, yb, preferred_element_type=F32)
           + jnp.dot(w_lo, yb, preferred_element_type=F32))
    if final:
        out = _rms(out, fg_ref[...])
    o_ref[...] = out


def _moe_layer(h, g, w_router, b_router, w_gate_up, b_gate_up, w_down, b_down, final_g, final, layer):
    t, d = h.shape
    n_exp = w_router.shape[1]
    f = w_down.shape[2]
    top_k = TOP_K
    bm = MOE_BLOCK
    ts = _pick(t, MOE_SORT_TILE)
    nt = t // ts
    srows = ts * top_k + n_exp * RUN_ALIGN
    n_blocks = -(-(t * top_k + nt * n_exp * (RUN_ALIGN - 1)) // bm) + n_exp
    n_slots = n_blocks * bm
    nbp = -(-n_blocks // SUBLANES) * SUBLANES

    tm = _pick(t, 512)
    wr_hi = w_router.astype(BF16)
    wr_lo = (w_router - wr_hi.astype(F32)).astype(BF16)
    idx, gates = pl.pallas_call(
        functools.partial(_route_kernel, n_exp=n_exp, top_k=top_k),
        grid=(t // tm,),
        in_specs=[pl.BlockSpec((tm, d), lambda i: (i, 0)), _full((1, d)), _full((d, n_exp)),
                  _full((d, n_exp)), _full((1, n_exp))],
        out_specs=[pl.BlockSpec((tm, top_k), lambda i: (i, 0)), pl.BlockSpec((tm, top_k), lambda i: (i, 0))],
        out_shape=[jax.ShapeDtypeStruct((t, top_k), I32), jax.ShapeDtypeStruct((t, top_k), F32)],
        compiler_params=_params(("arbitrary",)),
        name="moe_route",
    )(h, g.reshape(1, d), wr_hi, wr_lo, b_router.reshape(1, n_exp))

    tab = jax.ShapeDtypeStruct((nt, 1, n_exp), I32)
    tab_spec = pl.BlockSpec((1, 1, n_exp), lambda p, i: (i * p, 0, 0))
    pos, post, gst, off, nch, bexp, nxt, meta = pl.pallas_call(
        functools.partial(_rank_kernel, n_exp=n_exp, top_k=top_k, bm=bm, nbp=nbp),
        grid=(2, nt),
        in_specs=[pl.BlockSpec((ts, top_k), lambda p, i: (i, 0)), pl.BlockSpec((top_k, ts), lambda p, i: (0, i))],
        out_specs=[pl.BlockSpec((ts, top_k), lambda p, i: (i * p, 0)),
                   pl.BlockSpec((top_k, ts), lambda p, i: (0, i * p)),
                   tab_spec, tab_spec, tab_spec, _full((nbp, 1)), _full((nbp, 1)), _full((SUBLANES, n_exp))],
        out_shape=[jax.ShapeDtypeStruct((t, top_k), I32), jax.ShapeDtypeStruct((top_k, t), I32), tab, tab, tab,
                   jax.ShapeDtypeStruct((nbp, 1), I32), jax.ShapeDtypeStruct((nbp, 1), I32),
                   jax.ShapeDtypeStruct((SUBLANES, n_exp), I32)],
        scratch_shapes=[pltpu.VMEM((1, n_exp), F32), pltpu.VMEM((1, n_exp), F32)],
        compiler_params=_params(("arbitrary", "arbitrary")),
        name="moe_rank",
    )(idx, idx.T)
    bexp = bexp.reshape(nbp)
    nxt = nxt.reshape(nbp)
    n_used = meta[0, 0:1]
    last_blk = meta[1]

    def smem_tab():
        return pl.BlockSpec((1, 1, n_exp), lambda i, *_: (i, 0, 0), memory_space=pltpu.SMEM)

    def smem_tab_next():
        return pl.BlockSpec((1, 1, n_exp), lambda i, *_: (jnp.minimum(i + 1, nt - 1), 0, 0),
                            memory_space=pltpu.SMEM)

    xs = pl.pallas_call(
        functools.partial(_dispatch_kernel, n_exp=n_exp, top_k=top_k, bm=bm),
        grid_spec=pltpu.PrefetchScalarGridSpec(
            num_scalar_prefetch=1,
            grid=(nt,),
            in_specs=[
                smem_tab(), smem_tab(), smem_tab(),
                pl.BlockSpec((ts, d), lambda i, last: (i, 0)),
                pl.BlockSpec((1, d), lambda i, last: (0, 0)),
                pl.BlockSpec((top_k, ts), lambda i, last: (0, i)),
            ],
            out_specs=pl.BlockSpec(memory_space=pl.ANY),
            scratch_shapes=[pltpu.VMEM((2, srows, d), F32), pltpu.VMEM((bm, d), F32), pltpu.SMEM((1,), I32),
                            pltpu.SemaphoreType.DMA((2,)), pltpu.SemaphoreType.DMA(())],
        ),
        out_shape=jax.ShapeDtypeStruct((n_slots, d), F32),
        compiler_params=_params(("arbitrary",)),
        name="moe_dispatch",
    )(last_blk, gst, off, nch, h, g.reshape(1, d), post)

    def blk(b, nused_ref):
        return jnp.minimum(b, nused_ref[0] - 1)

    ys = pl.pallas_call(
        functools.partial(_ffn_kernel, f=f, layer=layer),
        grid_spec=pltpu.PrefetchScalarGridSpec(
            num_scalar_prefetch=3,
            grid=(n_blocks,),
            in_specs=[
                pl.BlockSpec((bm, d), lambda b, be, nx, nu: (blk(b, nu), 0)),
                pl.BlockSpec(memory_space=pl.ANY),
                pl.BlockSpec((1, 1, 2 * f), lambda b, be, nx, nu: (be[blk(b, nu)], 0, 0)),
                pl.BlockSpec(memory_space=pl.ANY),
                pl.BlockSpec((1, 1, d), lambda b, be, nx, nu: (be[blk(b, nu)], 0, 0)),
            ],
            out_specs=pl.BlockSpec((bm, d), lambda b, be, nx, nu: (blk(b, nu), 0)),
            scratch_shapes=[pltpu.VMEM((d, 2 * f), F32), pltpu.VMEM((f, d), F32),
                            pltpu.VMEM((d, 2 * f), BF16), pltpu.VMEM((f, d), BF16), pltpu.SemaphoreType.DMA((2,))],
        ),
        out_shape=jax.ShapeDtypeStruct((n_slots, d), F32),
        compiler_params=_params(("arbitrary",)),
        name="moe_ffn",
    )(bexp, nxt, n_used, xs, w_gate_up, b_gate_up.reshape(n_exp, 1, 2 * f), w_down, b_down.reshape(n_exp, 1, d))

    return pl.pallas_call(
        functools.partial(_combine_kernel, n_exp=n_exp, top_k=top_k, final=final),
        grid=(nt,),
        in_specs=[
            smem_tab(), smem_tab(), smem_tab(), smem_tab_next(), smem_tab_next(), smem_tab_next(),
            pl.BlockSpec((ts, d), lambda i: (i, 0)),
            pl.BlockSpec((ts, top_k), lambda i: (i, 0)),
            pl.BlockSpec((ts, top_k), lambda i: (i, 0)),
            pl.BlockSpec(memory_space=pl.ANY),
            _full((1, d)),
        ],
        out_specs=pl.BlockSpec((ts, d), lambda i: (i, 0)),
        out_shape=jax.ShapeDtypeStruct((t, d), F32),
        scratch_shapes=[pltpu.VMEM((2, srows, d), F32), pltpu.SemaphoreType.DMA((2,))],
        compiler_params=_params(("arbitrary",)),
        name="moe_combine",
    )(gst, off, nch, gst, off, nch, h, gates, pos, ys, final_g.reshape(1, d))


def kernel(x, mix_norm, ffn_norm, final_norm, lru_w_in, lru_conv_w, lru_conv_b, lru_w_a, lru_b_a, lru_w_x, lru_b_x, lru_a_param, lru_w_out, pool_w_in, pool_w_group, pool_scale, pool_w_out, sb_w_qkv, sb_w_out, moe_w_router, moe_b_router, moe_w_gate_up, moe_b_gate_up, moe_w_down, moe_b_down):
    batch, seq, d = x.shape
    depth = mix_norm.shape[0]
    h = x.reshape(batch * seq, d)
    for layer in range(depth):
        kind = layer % N_MIXERS
        slot = layer // N_MIXERS
        if kind == 0:
            h = _lru_layer(h, batch, seq, mix_norm[layer], lru_w_in[slot], lru_conv_w[slot], lru_conv_b[slot],
                           lru_w_a[slot], lru_b_a[slot], lru_w_x[slot], lru_b_x[slot], lru_a_param[slot],
                           lru_w_out[slot])
        elif kind == 1:
            h = _pool_layer(h, batch, seq, mix_norm[layer], pool_w_in[slot], pool_w_group[slot], pool_scale[slot],
                            pool_w_out[slot])
        else:
            h = _sb_layer(h, batch, seq, mix_norm[layer], sb_w_qkv[slot], sb_w_out[slot])
        h = _moe_layer(h, ffn_norm[layer], moe_w_router[layer], moe_b_router[layer], moe_w_gate_up,
                       moe_b_gate_up[layer], moe_w_down, moe_b_down[layer], final_norm,
                       final=(layer == depth - 1), layer=layer)
    return h.reshape(batch, seq, d)
```

```python
import functools
import math

import jax
import jax.numpy as jnp
from jax import lax
from jax.experimental import pallas as pl
from jax.experimental.pallas import tpu as pltpu

F32 = jnp.float32
BF16 = jnp.bfloat16
I32 = jnp.int32

RMS_EPS = 1e-6
LRU_C = 8.0
POOL_WINDOWS = (2, 4, 8, 16)
SB_HEADS = 16
SB_HEADS_PER_STEP = 4
LOG2E = 1.4426950408889634
TOP_K = 4
SWIGLU_LIMIT = 7.0
SWIGLU_ALPHA = 1.702
N_MIXERS = 3

V7X_VMEM_BYTES = 64 * 1024 * 1024
VMEM_LIMIT_BYTES = V7X_VMEM_BYTES - 8 * 1024 * 1024
SUBLANES = 8
LANES = 128

MOE_BLOCK = 512
MOE_SORT_TILE = 256
RUN_ALIGN = SUBLANES
BIG_CHUNK = 4 * RUN_ALIGN
WAIT_CHUNK = 16 * RUN_ALIGN


def _params(semantics, vmem=VMEM_LIMIT_BYTES):
    return pltpu.CompilerParams(dimension_semantics=semantics, vmem_limit_bytes=vmem)


def _pick(n, pref):
    t = min(n, pref)
    while n % t:
        t //= 2
    return t


def _rms(h, g):
    ms = jnp.mean(h * h, axis=-1, keepdims=True)
    return h * lax.rsqrt(ms + RMS_EPS) * g


def _sigmoid(x):
    return 1.0 / (1.0 + jnp.exp(-x))


def _softplus(x):
    return jnp.maximum(x, 0.0) + jnp.log(1.0 + jnp.exp(-jnp.abs(x)))


def _gelu_tanh(x):
    c = math.sqrt(2.0 / math.pi)
    return 0.5 * x * (1.0 + jnp.tanh(c * (x + 0.044715 * (x * x * x))))


def _full(shape):
    n = len(shape)
    return pl.BlockSpec(shape, lambda *_: (0,) * n)


def _lru_kernel(h_ref, g_ref, win_ref, cw_ref, cb_ref, wax_ref, ba_ref, bx_ref, ap_ref, wout_ref,
                o_ref, hcar_ref, tail_ref, y_ref, *, ts, w, bw):
    s = pl.program_id(1)

    @pl.when(s == 0)
    def _():
        hcar_ref[...] = jnp.zeros_like(hcar_ref)
        tail_ref[...] = jnp.zeros_like(tail_ref)

    h = h_ref[...]
    u = _rms(h, g_ref[...]).astype(BF16)
    proj = jnp.dot(u, win_ref[...], preferred_element_type=F32)
    row8 = lax.broadcasted_iota(I32, (SUBLANES, bw), 0)
    kw = cw_ref.shape[0]

    for c in range(w // bw):
        lo, hi = c * bw, (c + 1) * bw
        gate = proj[:, lo:hi]
        xb = proj[:, w + lo:w + hi]
        tail = tail_ref[:, lo:hi]
        xc = cb_ref[:, lo:hi] + xb * cw_ref[kw - 1:kw, lo:hi]
        for j in range(1, kw):
            rolled = pltpu.roll(xb, j, axis=0)
            first = jnp.where(row8 < j, pltpu.roll(tail, j, axis=0), rolled[0:SUBLANES])
            shifted = jnp.concatenate([first, rolled[SUBLANES:]], axis=0)
            xc = xc + shifted * cw_ref[kw - 1 - j:kw - j, lo:hi]
        tail_ref[:, lo:hi] = xb[ts - SUBLANES:ts]

        res = jnp.dot(xc.astype(BF16), wax_ref[c], preferred_element_type=F32)
        r = _sigmoid(res[:, :bw] + ba_ref[:, lo:hi])
        i = _sigmoid(res[:, bw:] + bx_ref[:, lo:hi])
        log_a = (-LRU_C * _softplus(-ap_ref[:, lo:hi])) * r
        a = jnp.exp(log_a)
        b = jnp.sqrt(-jnp.tanh(log_a) * (a * a + 1.0)) * (i * xc)
        hprev = hcar_ref[:, lo:hi]
        groups = []
        for gi in range(ts // SUBLANES):
            ag = a[gi * SUBLANES:(gi + 1) * SUBLANES]
            bg = b[gi * SUBLANES:(gi + 1) * SUBLANES]
            d = 1
            while d < SUBLANES:
                keep = row8 >= d
                a_sh = jnp.where(keep, pltpu.roll(ag, d, axis=0), 1.0)
                b_sh = jnp.where(keep, pltpu.roll(bg, d, axis=0), 0.0)
                bg = ag * b_sh + bg
                ag = ag * a_sh
                d *= 2
            hg = ag * hprev + bg
            hprev = hg[SUBLANES - 1:SUBLANES]
            groups.append(hg)
        hs = jnp.concatenate(groups, axis=0)
        hcar_ref[:, lo:hi] = hprev
        y_ref[:, lo:hi] = (hs * _gelu_tanh(gate)).astype(BF16)

    o_ref[...] = h + jnp.dot(y_ref[...], wout_ref[...], preferred_element_type=F32)


def _lru_layer(h, batch, seq, g, w_in, conv_w, conv_b, w_a, b_a, w_x, b_x, a_param, w_out):
    t, d = h.shape
    w = w_in.shape[1] // 2
    nblk, bw, _ = w_a.shape
    ts = _pick(seq, 256)
    ns = seq // ts
    wax = jnp.concatenate([w_a, w_x], axis=-1).astype(BF16)
    row = lambda v: v.reshape(1, -1)
    kern = functools.partial(_lru_kernel, ts=ts, w=w, bw=bw)
    return pl.pallas_call(
        kern,
        grid=(batch, ns),
        in_specs=[
            pl.BlockSpec((ts, d), lambda b, s: (b * ns + s, 0)),
            _full((1, d)), _full((d, 2 * w)), _full(conv_w.shape), _full((1, w)),
            _full(wax.shape), _full((1, w)), _full((1, w)), _full((1, w)), _full((w, d)),
        ],
        out_specs=pl.BlockSpec((ts, d), lambda b, s: (b * ns + s, 0)),
        out_shape=jax.ShapeDtypeStruct((t, d), F32),
        scratch_shapes=[pltpu.VMEM((1, w), F32), pltpu.VMEM((SUBLANES, w), F32), pltpu.VMEM((ts, w), BF16)],
        compiler_params=_params(("arbitrary", "arbitrary")),
        name="lru_mixer",
    )(h, row(g), w_in.astype(BF16), conv_w, row(conv_b), wax, row(b_a), row(b_x), row(a_param),
      w_out.astype(BF16))


def _pool_kernel(h_ref, g_ref, win_ref, wg_ref, sc_ref, wout_ref, o_ref, ext_ref, m_ref, *, ts, d, wins):
    s = pl.program_id(1)
    pad = wins[-1]
    gw = d // len(wins)

    @pl.when(s == 0)
    def _():
        ext_ref[0:pad, :] = jnp.zeros((pad, d), F32)

    h = h_ref[...]
    u = _rms(h, g_ref[...]).astype(BF16)
    v = jnp.dot(u, win_ref[...], preferred_element_type=F32)
    ext_ref[pad:pad + ts, :] = v
    n_avail = (s * ts + lax.broadcasted_iota(I32, (ts, 1), 0) + 1).astype(F32)

    cur = ext_ref[...]
    width = 1
    for gi, win in enumerate(wins):
        while width < win:
            cur = cur + pltpu.roll(cur, width, axis=0)
            width *= 2
        lo, hi = gi * gw, (gi + 1) * gw
        mean = cur[pad:, 0:gw] / jnp.minimum(n_avail, float(win))
        pooled = mean - v[:, lo:hi]
        mixed = jnp.dot(pooled.astype(BF16), wg_ref[gi], preferred_element_type=F32)
        m_ref[:, lo:hi] = (mixed * sc_ref[:, lo:hi]).astype(BF16)
        cur = cur[:, gw:]
    ext_ref[0:pad, :] = v[ts - pad:ts]
    o_ref[...] = h + jnp.dot(m_ref[...], wout_ref[...], preferred_element_type=F32)


def _pool_layer(h, batch, seq, g, w_in, w_group, scale, w_out):
    t, d = h.shape
    ts = _pick(seq, 256)
    ns = seq // ts
    pad = POOL_WINDOWS[-1]
    kern = functools.partial(_pool_kernel, ts=ts, d=d, wins=POOL_WINDOWS)
    return pl.pallas_call(
        kern,
        grid=(batch, ns),
        in_specs=[
            pl.BlockSpec((ts, d), lambda b, s: (b * ns + s, 0)),
            _full((1, d)), _full((d, d)), _full(w_group.shape), _full((1, d)), _full((d, d)),
        ],
        out_specs=pl.BlockSpec((ts, d), lambda b, s: (b * ns + s, 0)),
        out_shape=jax.ShapeDtypeStruct((t, d), F32),
        scratch_shapes=[pltpu.VMEM((ts + pad, d), F32), pltpu.VMEM((ts, d), BF16)],
        compiler_params=_params(("arbitrary", "arbitrary")),
        name="pool_mixer",
    )(h, g.reshape(1, d), w_in.astype(BF16), w_group.astype(BF16), scale.reshape(1, d), w_out.astype(BF16))


def _sb_proj_kernel(h_ref, g_ref, wqt_ref, wk_ref, wvt_ref, qt_ref, k_ref, vt_ref):
    u = _rms(h_ref[...], g_ref[...]).astype(BF16)
    nt = (((1,), (1,)), ((), ()))
    qt_ref[...] = lax.dot_general(wqt_ref[...], u, nt, preferred_element_type=F32).astype(BF16)
    k_ref[...] = jnp.dot(u, wk_ref[...], preferred_element_type=F32).astype(BF16)
    vt_ref[...] = lax.dot_general(wvt_ref[...], u, nt, preferred_element_type=F32).astype(BF16)


def _sb_attn_kernel(qt_ref, k_ref, vt_ref, o_ref, lhi_ref, llo_ref, zs_ref, a_ref, *, bq, dh, nh):
    qb = pl.program_id(2)
    krow = lax.broadcasted_iota(I32, (bq, bq), 0)
    qcol = lax.broadcasted_iota(I32, (bq, bq), 1)
    upper = (qcol > krow).astype(BF16)
    causal = krow < qcol

    def step(chains, masked, st):
        totals = []
        for c, (hh, j) in enumerate(chains):
            qh = qt_ref[hh * dh:(hh + 1) * dh, :]
            kh = k_ref[pl.ds(j * bq, bq), hh * dh:(hh + 1) * dh]
            z = jnp.dot(kh, qh, preferred_element_type=F32)
            t = jnp.exp2(jnp.abs(z) * (-LOG2E))
            l1m = jnp.minimum(-z, 0.0) - jnp.log(1.0 + t)
            zs_ref[c] = z + l1m
            lm = jnp.where(causal, l1m, 0.0) if masked else l1m
            l_hi = lm.astype(BF16)
            lhi_ref[c] = l_hi
            llo_ref[c] = (lm - l_hi.astype(F32)).astype(BF16)
            totals.append(jnp.sum(lm, axis=0, keepdims=True))
        st = list(st)
        for c, (hh, j) in enumerate(chains):
            acc, carry = st[hh]
            suffix = (jnp.dot(upper, lhi_ref[c], preferred_element_type=F32)
                      + jnp.dot(upper, llo_ref[c], preferred_element_type=F32))
            a = jnp.exp(zs_ref[c] + suffix + carry)
            if masked:
                a = jnp.where(causal, a, 0.0)
            a_ref[c] = a.astype(BF16)
            st[hh] = (acc, carry + totals[c])
        for c, (hh, j) in enumerate(chains):
            acc, carry = st[hh]
            vth = vt_ref[hh * dh:(hh + 1) * dh, pl.ds(j * bq, bq)]
            st[hh] = (acc + jnp.dot(vth, a_ref[c], preferred_element_type=F32), carry)
        return st

    heads = range(nh)
    zero = (jnp.zeros((dh, bq), F32), jnp.zeros((1, bq), F32))
    st = step([(hh, qb) for hh in heads], True, [zero] * nh)

    def pair_body(i, st):
        j = qb - 1 - 2 * i
        return step([(hh, j) for hh in heads] + [(hh, j - 1) for hh in heads], False, st)

    def single_body(i, st):
        return step([(hh, 0) for hh in heads], False, st)

    st = lax.fori_loop(0, lax.shift_right_logical(qb, 1), pair_body, st)
    st = lax.fori_loop(0, qb & 1, single_body, st)
    o_ref[...] = jnp.concatenate([st[hh][0].T for hh in heads], axis=1).astype(BF16)


def _sb_out_kernel(h_ref, o_ref, w_ref, out_ref):
    out_ref[...] = h_ref[...] + jnp.dot(o_ref[...], w_ref[...], preferred_element_type=F32)


def _sb_layer(h, batch, seq, g, w_qkv, w_out):
    t, d = h.shape
    dh = d // SB_HEADS
    scale = 1.0 / math.sqrt(dh)
    wqt = (w_qkv[:, :d] * scale).T.astype(BF16)
    wk = w_qkv[:, d:2 * d].astype(BF16)
    wvt = w_qkv[:, 2 * d:].T.astype(BF16)
    tp = _pick(t, 512)
    qt, k, vt = pl.pallas_call(
        _sb_proj_kernel,
        grid=(t // tp,),
        in_specs=[pl.BlockSpec((tp, d), lambda i: (i, 0)), _full((1, d)), _full((d, d)), _full((d, d)),
                  _full((d, d))],
        out_specs=[pl.BlockSpec((d, tp), lambda i: (0, i)), pl.BlockSpec((tp, d), lambda i: (i, 0)),
                   pl.BlockSpec((d, tp), lambda i: (0, i))],
        out_shape=[jax.ShapeDtypeStruct((d, t), BF16), jax.ShapeDtypeStruct((t, d), BF16),
                   jax.ShapeDtypeStruct((d, t), BF16)],
        compiler_params=_params(("arbitrary",)),
        name="sb_qkv_proj",
    )(h, g.reshape(1, d), wqt, wk, wvt)

    bq = _pick(seq, 256)
    nq = seq // bq
    nh = SB_HEADS_PER_STEP
    grp = nh * dh
    nchain = 2 * nh
    o = pl.pallas_call(
        functools.partial(_sb_attn_kernel, bq=bq, dh=dh, nh=nh),
        grid=(batch, d // grp, nq),
        in_specs=[
            pl.BlockSpec((grp, bq), lambda b, p, q: (p, b * nq + q)),
            pl.BlockSpec((seq, grp), lambda b, p, q: (b, p)),
            pl.BlockSpec((grp, seq), lambda b, p, q: (p, b)),
        ],
        out_specs=pl.BlockSpec((bq, grp), lambda b, p, q: (b * nq + q, p)),
        out_shape=jax.ShapeDtypeStruct((t, d), BF16),
        scratch_shapes=[pltpu.VMEM((nchain, bq, bq), BF16), pltpu.VMEM((nchain, bq, bq), BF16),
                        pltpu.VMEM((nchain, bq, bq), F32), pltpu.VMEM((nchain, bq, bq), BF16)],
        compiler_params=_params(("arbitrary", "arbitrary", "arbitrary")),
        name="sb_attention",
    )(qt, k, vt)

    return pl.pallas_call(
        _sb_out_kernel,
        grid=(t // tp,),
        in_specs=[pl.BlockSpec((tp, d), lambda i: (i, 0)), pl.BlockSpec((tp, d), lambda i: (i, 0)),
                  _full((d, d))],
        out_specs=pl.BlockSpec((tp, d), lambda i: (i, 0)),
        out_shape=jax.ShapeDtypeStruct((t, d), F32),
        compiler_params=_params(("arbitrary",)),
        name="sb_out_proj",
    )(h, o, w_out.astype(BF16))


def _route_kernel(h_ref, g_ref, wr_hi_ref, wr_lo_ref, br_ref, idx_ref, gate_ref, nch_ref, *, n_exp, top_k, ts):
    u = _rms(h_ref[...], g_ref[...])
    u_hi = u.astype(BF16)
    u_lo = (u - u_hi.astype(F32)).astype(BF16)
    logits = (jnp.dot(u_hi, wr_hi_ref[...], preferred_element_type=F32)
              + jnp.dot(u_hi, wr_lo_ref[...], preferred_element_type=F32)
              + jnp.dot(u_lo, wr_hi_ref[...], preferred_element_type=F32)) + br_ref[...]
    tm = logits.shape[0]
    lane = lax.broadcasted_iota(I32, (tm, n_exp), 1)
    kcol = lax.broadcasted_iota(I32, (tm, top_k), 1)
    vals = logits
    top_v, top_i = [], []
    for _ in range(top_k):
        m = jnp.max(vals, axis=-1, keepdims=True)
        sel = jnp.min(jnp.where(vals == m, lane, n_exp), axis=-1, keepdims=True)
        top_v.append(m)
        top_i.append(sel)
        vals = jnp.where(lane == sel, -jnp.inf, vals)
    exps = [jnp.exp(v - top_v[0]) for v in top_v]
    denom = exps[0]
    for e in exps[1:]:
        denom = denom + e
    idx = jnp.zeros((tm, top_k), I32)
    gates = jnp.zeros((tm, top_k), F32)
    for k in range(top_k):
        idx = jnp.where(kcol == k, top_i[k], idx)
        gates = jnp.where(kcol == k, exps[k] / denom, gates)
    idx_ref[...] = idx
    gate_ref[...] = gates
    chosen = jnp.where(lane == top_i[0], 1.0, 0.0)
    for k in range(1, top_k):
        chosen = jnp.where(lane == top_i[k], 1.0, chosen)
    for s in range(tm // ts):
        cnt = jnp.sum(chosen[s * ts:(s + 1) * ts], axis=0, keepdims=True)
        nch_ref[s] = jnp.floor((cnt + (RUN_ALIGN - 1)) * (1.0 / RUN_ALIGN)).astype(I32)


def _rank_kernel(idx_ref, idxt_ref, nchall_ref, pos_ref, post_ref, gst_ref, off_ref, bexp_ref, nxt_ref, meta_ref,
                 run_ref, pstart_ref, *, n_exp, top_k, bm, nbp):
    i = pl.program_id(0)
    tm = idx_ref.shape[0]
    cpb = bm // RUN_ALIGN
    idx = idx_ref[...]
    lane = lax.broadcasted_iota(I32, (tm, n_exp), 1)
    ohs = [(idx[:, k:k + 1] == lane).astype(F32) for k in range(top_k)]
    oh = ohs[0]
    for o in ohs[1:]:
        oh = oh + o
    cnt = jnp.sum(oh, axis=0, keepdims=True)
    nch = jnp.floor((cnt + (RUN_ALIGN - 1)) * (1.0 / RUN_ALIGN))
    er = lax.broadcasted_iota(I32, (n_exp, n_exp), 0)
    ec = lax.broadcasted_iota(I32, (n_exp, n_exp), 1)

    def excl_cumsum_row(v):
        v8 = jnp.broadcast_to(v, (SUBLANES, n_exp)).astype(BF16)
        return jnp.dot(v8, (er < ec).astype(BF16), preferred_element_type=F32)[0:1, :]

    @pl.when(i == 0)
    def _():
        tot = jnp.sum(nchall_ref[...].astype(F32), axis=0)
        nb = jnp.floor((tot + (cpb - 1)) * (1.0 / cpb))
        bstart = excl_cumsum_row(nb)
        bend = bstart + nb
        pstart_ref[...] = bstart * cpb
        blk = lax.broadcasted_iota(I32, (nbp, n_exp), 0).astype(F32)
        be = jnp.sum((bend <= blk).astype(F32), axis=-1, keepdims=True)
        be = jnp.minimum(be, n_exp - 1)
        bexp_ref[...] = be.astype(I32)
        n_used = jnp.sum(nb, axis=-1, keepdims=True)
        lane_e = lax.broadcasted_iota(I32, (nbp, n_exp), 1).astype(F32)
        run_end = jnp.sum(jnp.where(lane_e == be, bend, 0.0), axis=-1, keepdims=True)
        nx = jnp.minimum(jnp.sum((bend <= run_end).astype(F32), axis=-1, keepdims=True), n_exp - 1)
        nxt_ref[...] = jnp.where(run_end < n_used, nx, -1.0).astype(I32)
        last = jnp.where(nb > 0, bend - 1.0, -1.0)
        mrow = lax.broadcasted_iota(I32, (SUBLANES, n_exp), 0)
        meta = jnp.where(mrow == 0, jnp.broadcast_to(n_used, (SUBLANES, n_exp)),
                         jnp.where(mrow == 1, jnp.broadcast_to(last, (SUBLANES, n_exp)), 0.0))
        meta_ref[...] = meta.astype(I32)
        run_ref[...] = jnp.zeros_like(run_ref)

    tr = lax.broadcasted_iota(I32, (tm, tm), 0)
    tc = lax.broadcasted_iota(I32, (tm, tm), 1)
    off = excl_cumsum_row(nch)
    excl = jnp.dot((tc < tr).astype(BF16), oh.astype(BF16), preferred_element_type=F32)
    base = excl + off * RUN_ALIGN
    kcol = lax.broadcasted_iota(I32, (tm, top_k), 1)
    pos = jnp.zeros((tm, top_k), I32)
    for k in range(top_k):
        pk = jnp.sum(ohs[k] * base, axis=-1, keepdims=True).astype(I32)
        pos = jnp.where(kcol == k, pk, pos)
    pos_ref[...] = pos
    idxt = idxt_ref[...]
    sub = lax.broadcasted_iota(I32, (n_exp, tm), 0)
    ohts = [(idxt[k:k + 1, :] == sub).astype(F32) for k in range(top_k)]
    oht = ohts[0]
    for o in ohts[1:]:
        oht = oht + o
    cnt_col = jnp.sum(oht, axis=1, keepdims=True)
    nch_col = jnp.floor((cnt_col + (RUN_ALIGN - 1)) * (1.0 / RUN_ALIGN))
    off_col = jnp.dot((ec < er).astype(BF16), jnp.broadcast_to(nch_col, (n_exp, LANES)).astype(BF16),
                      preferred_element_type=F32)[:, 0:1]
    exclt = jnp.dot(oht.astype(BF16), (tr < tc).astype(BF16), preferred_element_type=F32)
    baset = exclt + off_col * RUN_ALIGN
    krow = lax.broadcasted_iota(I32, (top_k, tm), 0)
    post = jnp.zeros((top_k, tm), I32)
    for k in range(top_k):
        pk = jnp.sum(ohts[k] * baset, axis=0, keepdims=True).astype(I32)
        post = jnp.where(krow == k, pk, post)
    post_ref[...] = post
    gst_ref[0] = (pstart_ref[...] + run_ref[...]).astype(I32)
    off_ref[0] = off.astype(I32)
    run_ref[...] = run_ref[...] + nch


def _run_chunks(gst_ref, off_ref, nch_ref, n_exp, make_copy):
    per_big = BIG_CHUNK // RUN_ALIGN

    def per_expert(e, total):
        n = nch_ref[0, 0, e]
        g0 = gst_ref[0, 0, e]
        o0 = off_ref[0, 0, e]
        nbig = lax.shift_right_logical(n, per_big.bit_length() - 1)

        def big(j, c):
            make_copy(BIG_CHUNK, pl.multiple_of((o0 + j * per_big) * RUN_ALIGN, RUN_ALIGN),
                      pl.multiple_of((g0 + j * per_big) * RUN_ALIGN, RUN_ALIGN)).start()
            return c

        def small(j, c):
            make_copy(RUN_ALIGN, pl.multiple_of((o0 + j) * RUN_ALIGN, RUN_ALIGN),
                      pl.multiple_of((g0 + j) * RUN_ALIGN, RUN_ALIGN)).start()
            return c

        lax.fori_loop(0, nbig, big, 0)
        lax.fori_loop(nbig * per_big, n, small, 0)
        return total + n

    return lax.fori_loop(0, n_exp, per_expert, 0)


def _wait_units(n, make_copy):
    per_wait = WAIT_CHUNK // RUN_ALIGN

    def wait_big(j, c):
        make_copy(WAIT_CHUNK, 0, 0).wait()
        return c

    def wait_small(j, c):
        make_copy(RUN_ALIGN, 0, 0).wait()
        return c

    nbig = lax.shift_right_logical(n, per_wait.bit_length() - 1)
    lax.fori_loop(0, nbig, wait_big, 0)
    lax.fori_loop(nbig * per_wait, n, wait_small, 0)


def _dispatch_kernel(last_ref, gst_ref, off_ref, nch_ref, h_ref, g_ref, post_ref, xs_ref, sbuf, zbuf, pending, sem,
                     zsem, *, n_exp, top_k, bm):
    i = pl.program_id(0)
    nt = pl.num_programs(0)
    slot = i % 2
    rows = sbuf.shape[1]

    @pl.when(i == 0)
    def _():
        zbuf[...] = jnp.zeros_like(zbuf)

        def zero_copy(e):
            return pltpu.make_async_copy(zbuf, xs_ref.at[pl.ds(last_ref[e] * bm, bm)], zsem)

        def start(e, c):
            @pl.when(last_ref[e] >= 0)
            def _():
                zero_copy(e).start()
            return c

        def wait(e, c):
            @pl.when(last_ref[e] >= 0)
            def _():
                zero_copy(e).wait()
            return c

        lax.fori_loop(0, n_exp, start, 0)
        lax.fori_loop(0, n_exp, wait, 0)

    u = _rms(h_ref[...], g_ref[...]).astype(BF16)
    tm = u.shape[0]
    post = post_ref[...]
    q = lax.broadcasted_iota(I32, (rows, tm), 0)
    perm = jnp.zeros((rows, tm), F32)
    for k in range(top_k):
        perm = jnp.where(q == post[k:k + 1, :], 1.0, perm)
    sbuf[slot] = jnp.dot(perm.astype(BF16), u, preferred_element_type=F32)

    def make_copy(s, nrows, src_row, dst_row):
        return pltpu.make_async_copy(sbuf.at[s, pl.ds(src_row, nrows)], xs_ref.at[pl.ds(dst_row, nrows)], sem.at[s])

    n = _run_chunks(gst_ref, off_ref, nch_ref, n_exp, functools.partial(make_copy, slot))

    @pl.when(i > 0)
    def _():
        _wait_units(pending[0], functools.partial(make_copy, 1 - slot))

    pending[0] = n

    @pl.when(i == nt - 1)
    def _():
        _wait_units(n, functools.partial(make_copy, slot))


def _ffn_kernel(bexp_ref, nxt_ref, nused_ref, x_ref, wgu_hbm, bgu_ref, wd_hbm, bd_ref, y_ref, wgu_st, wd_st,
                wgu_bf, wd_bf, sem, *, f, layer):
    b = pl.program_id(0)
    prev = bexp_ref[jnp.maximum(b - 1, 0)]
    active = b < nused_ref[0]

    def fetch(e):
        return (pltpu.make_async_copy(wgu_hbm.at[layer, e], wgu_st, sem.at[0]),
                pltpu.make_async_copy(wd_hbm.at[layer, e], wd_st, sem.at[1]))

    @pl.when(b == 0)
    def _():
        for c in fetch(bexp_ref[0]):
            c.start()

    @pl.when(active & ((b == 0) | (bexp_ref[b] != prev)))
    def _():
        for c in fetch(bexp_ref[b]):
            c.wait()
        wgu_bf[...] = wgu_st[...].astype(BF16)
        wd_bf[...] = wd_st[...].astype(BF16)

        @pl.when(nxt_ref[b] >= 0)
        def _():
            for c in fetch(nxt_ref[b]):
                c.start()

    @pl.when(active)
    def _():
        x = x_ref[...].astype(BF16)
        gu = jnp.dot(x, wgu_bf[...], preferred_element_type=F32) + bgu_ref[0]
        gate = jnp.minimum(gu[:, :f], SWIGLU_LIMIT)
        up = jnp.clip(gu[:, f:], -SWIGLU_LIMIT, SWIGLU_LIMIT)
        glu = gate * _sigmoid(gate * SWIGLU_ALPHA)
        act = ((up + 1.0) * glu).astype(BF16)
        y_ref[...] = jnp.dot(act, wd_bf[...], preferred_element_type=F32) + bd_ref[0]


def _combine_kernel(gst_ref, off_ref, nch_ref, gst_nx, off_nx, nch_nx, h_ref, gate_ref, pos_ref, ys_ref, fg_ref,
                    o_ref, ybuf, sem, *, n_exp, top_k, final):
    i = pl.program_id(0)
    nt = pl.num_programs(0)
    slot = i % 2
    rows = ybuf.shape[1]

    def make_copy(s, nrows, dst_row, src_row):
        return pltpu.make_async_copy(ys_ref.at[pl.ds(src_row, nrows)], ybuf.at[s, pl.ds(dst_row, nrows)], sem.at[s])

    @pl.when(i == 0)
    def _():
        ybuf[...] = jnp.zeros_like(ybuf)
        _run_chunks(gst_ref, off_ref, nch_ref, n_exp, functools.partial(make_copy, 0))

    @pl.when(i + 1 < nt)
    def _():
        _run_chunks(gst_nx, off_nx, nch_nx, n_exp, functools.partial(make_copy, 1 - slot))

    n = lax.fori_loop(0, n_exp, lambda e, c: c + nch_ref[0, 0, e], 0)
    _wait_units(n, functools.partial(make_copy, slot))

    gates = gate_ref[...]
    pos = pos_ref[...]
    tm, d = h_ref.shape
    q = lax.broadcasted_iota(I32, (tm, rows), 1)
    wsel = jnp.zeros((tm, rows), F32)
    for k in range(top_k):
        wsel = jnp.where(q == pos[:, k:k + 1], gates[:, k:k + 1], wsel)
    w_hi = wsel.astype(BF16)
    w_lo = (wsel - w_hi.astype(F32)).astype(BF16)
    yb = ybuf[slot].astype(BF16)
    out = (h_ref[...] + jnp.dot(w_hi, yb, preferred_element_type=F32)
           + jnp.dot(w_lo, yb, preferred_element_type=F32))
    if final:
        out = _rms(out, fg_ref[...])
    o_ref[...] = out


def _moe_layer(h, g, w_router, b_router, w_gate_up, b_gate_up, w_down, b_down, final_g, final, layer):
    t, d = h.shape
    n_exp = w_router.shape[1]
    f = w_down.shape[2]
    top_k = TOP_K
    bm = MOE_BLOCK
    ts = _pick(t, MOE_SORT_TILE)
    nt = t // ts
    srows = -(-(ts * top_k + n_exp * RUN_ALIGN) // LANES) * LANES
    n_blocks = -(-(t * top_k + nt * n_exp * (RUN_ALIGN - 1)) // bm) + n_exp
    n_slots = n_blocks * bm
    nbp = -(-n_blocks // SUBLANES) * SUBLANES

    tm = ts * max(1, _pick(nt, 2))
    tab = jax.ShapeDtypeStruct((nt, 1, n_exp), I32)
    wr_hi = w_router.astype(BF16)
    wr_lo = (w_router - wr_hi.astype(F32)).astype(BF16)
    idx, gates, nch = pl.pallas_call(
        functools.partial(_route_kernel, n_exp=n_exp, top_k=top_k, ts=ts),
        grid=(t // tm,),
        in_specs=[pl.BlockSpec((tm, d), lambda i: (i, 0)), _full((1, d)), _full((d, n_exp)),
                  _full((d, n_exp)), _full((1, n_exp))],
        out_specs=[pl.BlockSpec((tm, top_k), lambda i: (i, 0)), pl.BlockSpec((tm, top_k), lambda i: (i, 0)),
                   pl.BlockSpec((tm // ts, 1, n_exp), lambda i: (i, 0, 0))],
        out_shape=[jax.ShapeDtypeStruct((t, top_k), I32), jax.ShapeDtypeStruct((t, top_k), F32), tab],
        compiler_params=_params(("arbitrary",)),
        name="moe_route",
    )(h, g.reshape(1, d), wr_hi, wr_lo, b_router.reshape(1, n_exp))

    tab_spec = pl.BlockSpec((1, 1, n_exp), lambda i: (i, 0, 0))
    pos, post, gst, off, bexp, nxt, meta = pl.pallas_call(
        functools.partial(_rank_kernel, n_exp=n_exp, top_k=top_k, bm=bm, nbp=nbp),
        grid=(nt,),
        in_specs=[pl.BlockSpec((ts, top_k), lambda i: (i, 0)), pl.BlockSpec((top_k, ts), lambda i: (0, i)),
                  _full((nt, 1, n_exp))],
        out_specs=[pl.BlockSpec((ts, top_k), lambda i: (i, 0)), pl.BlockSpec((top_k, ts), lambda i: (0, i)),
                   tab_spec, tab_spec, _full((nbp, 1)), _full((nbp, 1)), _full((SUBLANES, n_exp))],
        out_shape=[jax.ShapeDtypeStruct((t, top_k), I32), jax.ShapeDtypeStruct((top_k, t), I32), tab, tab,
                   jax.ShapeDtypeStruct((nbp, 1), I32), jax.ShapeDtypeStruct((nbp, 1), I32),
                   jax.ShapeDtypeStruct((SUBLANES, n_exp), I32)],
        scratch_shapes=[pltpu.VMEM((1, n_exp), F32), pltpu.VMEM((1, n_exp), F32)],
        compiler_params=_params(("arbitrary",)),
        name="moe_rank",
    )(idx, idx.T, nch)
    bexp = bexp.reshape(nbp)
    nxt = nxt.reshape(nbp)
    n_used = meta[0, 0:1]
    last_blk = meta[1]

    def smem_tab():
        return pl.BlockSpec((1, 1, n_exp), lambda i, *_: (i, 0, 0), memory_space=pltpu.SMEM)

    def smem_tab_next():
        return pl.BlockSpec((1, 1, n_exp), lambda i, *_: (jnp.minimum(i + 1, nt - 1), 0, 0),
                            memory_space=pltpu.SMEM)

    xs = pl.pallas_call(
        functools.partial(_dispatch_kernel, n_exp=n_exp, top_k=top_k, bm=bm),
        grid_spec=pltpu.PrefetchScalarGridSpec(
            num_scalar_prefetch=1,
            grid=(nt,),
            in_specs=[
                smem_tab(), smem_tab(), smem_tab(),
                pl.BlockSpec((ts, d), lambda i, last: (i, 0)),
                pl.BlockSpec((1, d), lambda i, last: (0, 0)),
                pl.BlockSpec((top_k, ts), lambda i, last: (0, i)),
            ],
            out_specs=pl.BlockSpec(memory_space=pl.ANY),
            scratch_shapes=[pltpu.VMEM((2, srows, d), F32), pltpu.VMEM((bm, d), F32), pltpu.SMEM((1,), I32),
                            pltpu.SemaphoreType.DMA((2,)), pltpu.SemaphoreType.DMA(())],
        ),
        out_shape=jax.ShapeDtypeStruct((n_slots, d), F32),
        compiler_params=_params(("arbitrary",)),
        name="moe_dispatch",
    )(last_blk, gst, off, nch, h, g.reshape(1, d), post)

    def blk(b, nused_ref):
        return jnp.minimum(b, nused_ref[0] - 1)

    ys = pl.pallas_call(
        functools.partial(_ffn_kernel, f=f, layer=layer),
        grid_spec=pltpu.PrefetchScalarGridSpec(
            num_scalar_prefetch=3,
            grid=(n_blocks,),
            in_specs=[
                pl.BlockSpec((bm, d), lambda b, be, nx, nu: (blk(b, nu), 0)),
                pl.BlockSpec(memory_space=pl.ANY),
                pl.BlockSpec((1, 1, 2 * f), lambda b, be, nx, nu: (be[blk(b, nu)], 0, 0)),
                pl.BlockSpec(memory_space=pl.ANY),
                pl.BlockSpec((1, 1, d), lambda b, be, nx, nu: (be[blk(b, nu)], 0, 0)),
            ],
            out_specs=pl.BlockSpec((bm, d), lambda b, be, nx, nu: (blk(b, nu), 0)),
            scratch_shapes=[pltpu.VMEM((d, 2 * f), F32), pltpu.VMEM((f, d), F32),
                            pltpu.VMEM((d, 2 * f), BF16), pltpu.VMEM((f, d), BF16), pltpu.SemaphoreType.DMA((2,))],
        ),
        out_shape=jax.ShapeDtypeStruct((n_slots, d), F32),
        compiler_params=_params(("arbitrary",)),
        name="moe_ffn",
    )(bexp, nxt, n_used, xs, w_gate_up, b_gate_up.reshape(n_exp, 1, 2 * f), w_down, b_down.reshape(n_exp, 1, d))

    return pl.pallas_call(
        functools.partial(_combine_kernel, n_exp=n_exp, top_k=top_k, final=final),
        grid=(nt,),
        in_specs=[
            smem_tab(), smem_tab(), smem_tab(), smem_tab_next(), smem_tab_next(), smem_tab_next(),
            pl.BlockSpec((ts, d), lambda i: (i, 0)),
            pl.BlockSpec((ts, top_k), lambda i: (i, 0)),
            pl.BlockSpec((ts, top_k), lambda i: (i, 0)),
            pl.BlockSpec(memory_space=pl.ANY),
            _full((1, d)),
        ],
        out_specs=pl.BlockSpec((ts, d), lambda i: (i, 0)),
        out_shape=jax.ShapeDtypeStruct((t, d), F32),
        scratch_shapes=[pltpu.VMEM((2, srows, d), F32), pltpu.SemaphoreType.DMA((2,))],
        compiler_params=_params(("arbitrary",)),
        name="moe_combine",
    )(gst, off, nch, gst, off, nch, h, gates, pos, ys, final_g.reshape(1, d))


def kernel(x, mix_norm, ffn_norm, final_norm, lru_w_in, lru_conv_w, lru_conv_b, lru_w_a, lru_b_a, lru_w_x, lru_b_x, lru_a_param, lru_w_out, pool_w_in, pool_w_group, pool_scale, pool_w_out, sb_w_qkv, sb_w_out, moe_w_router, moe_b_router, moe_w_gate_up, moe_b_gate_up, moe_w_down, moe_b_down):
    batch, seq, d = x.shape
    depth = mix_norm.shape[0]
    h = x.reshape(batch * seq, d)
    for layer in range(depth):
        kind = layer % N_MIXERS
        slot = layer // N_MIXERS
        if kind == 0:
            h = _lru_layer(h, batch, seq, mix_norm[layer], lru_w_in[slot], lru_conv_w[slot], lru_conv_b[slot],
                           lru_w_a[slot], lru_b_a[slot], lru_w_x[slot], lru_b_x[slot], lru_a_param[slot],
                           lru_w_out[slot])
        elif kind == 1:
            h = _pool_layer(h, batch, seq, mix_norm[layer], pool_w_in[slot], pool_w_group[slot], pool_scale[slot],
                            pool_w_out[slot])
        else:
            h = _sb_layer(h, batch, seq, mix_norm[layer], sb_w_qkv[slot], sb_w_out[slot])
        h = _moe_layer(h, ffn_norm[layer], moe_w_router[layer], moe_b_router[layer], moe_w_gate_up,
                       moe_b_gate_up[layer], moe_w_down, moe_b_down[layer], final_norm,
                       final=(layer == depth - 1), layer=layer)
    return h.reshape(batch, seq, d)
```

```python
import functools
import math

import jax
import jax.numpy as jnp
from jax import lax
from jax.experimental import pallas as pl
from jax.experimental.pallas import tpu as pltpu

F32 = jnp.float32
BF16 = jnp.bfloat16
I32 = jnp.int32

RMS_EPS = 1e-6
LRU_C = 8.0
POOL_WINDOWS = (2, 4, 8, 16)
SB_HEADS = 16
SB_HEADS_PER_STEP = 4
LOG2E = 1.4426950408889634
TOP_K = 4
SWIGLU_LIMIT = 7.0
SWIGLU_ALPHA = 1.702
N_MIXERS = 3

V7X_VMEM_BYTES = 64 * 1024 * 1024
VMEM_LIMIT_BYTES = V7X_VMEM_BYTES - 8 * 1024 * 1024
SUBLANES = 8
LANES = 128

MOE_BLOCK = 512
MOE_SORT_TILE = 256
RUN_ALIGN = SUBLANES
BIG_CHUNK = 4 * RUN_ALIGN
WAIT_CHUNK = 16 * RUN_ALIGN


def _params(semantics, vmem=VMEM_LIMIT_BYTES):
    return pltpu.CompilerParams(dimension_semantics=semantics, vmem_limit_bytes=vmem)


def _pick(n, pref):
    t = min(n, pref)
    while n % t:
        t //= 2
    return t


def _rms(h, g):
    ms = jnp.mean(h * h, axis=-1, keepdims=True)
    return h * lax.rsqrt(ms + RMS_EPS) * g


def _sigmoid(x):
    return 1.0 / (1.0 + jnp.exp(-x))


def _softplus(x):
    return jnp.maximum(x, 0.0) + jnp.log(1.0 + jnp.exp(-jnp.abs(x)))


def _gelu_tanh(x):
    c = math.sqrt(2.0 / math.pi)
    return 0.5 * x * (1.0 + jnp.tanh(c * (x + 0.044715 * (x * x * x))))


def _full(shape):
    n = len(shape)
    return pl.BlockSpec(shape, lambda *_: (0,) * n)


def _lru_kernel(h_ref, g_ref, win_ref, cw_ref, cb_ref, wax_ref, ba_ref, bx_ref, ap_ref, wout_ref,
                o_ref, hcar_ref, ext_ref, y_ref, *, ts, w, bw):
    s = pl.program_id(1)

    @pl.when(s == 0)
    def _():
        hcar_ref[...] = jnp.zeros_like(hcar_ref)
        ext_ref[0:SUBLANES, :] = jnp.zeros((SUBLANES, w), F32)

    h = h_ref[...]
    u = _rms(h, g_ref[...]).astype(BF16)
    proj = jnp.dot(u, win_ref[...], preferred_element_type=F32)
    row8 = lax.broadcasted_iota(I32, (SUBLANES, bw), 0)
    kw = cw_ref.shape[0]

    for c in range(w // bw):
        lo, hi = c * bw, (c + 1) * bw
        gate = proj[:, lo:hi]
        xb = proj[:, w + lo:w + hi]
        ext_ref[SUBLANES:SUBLANES + ts, lo:hi] = xb
        xc = cb_ref[:, lo:hi] + xb * cw_ref[kw - 1:kw, lo:hi]
        for j in range(1, kw):
            xc = xc + ext_ref[SUBLANES - j:SUBLANES - j + ts, lo:hi] * cw_ref[kw - 1 - j:kw - j, lo:hi]
        ext_ref[0:SUBLANES, lo:hi] = xb[ts - SUBLANES:ts]

        res = jnp.dot(xc.astype(BF16), wax_ref[c], preferred_element_type=F32)
        r = _sigmoid(res[:, :bw] + ba_ref[:, lo:hi])
        i = _sigmoid(res[:, bw:] + bx_ref[:, lo:hi])
        log_a = (-LRU_C * _softplus(-ap_ref[:, lo:hi])) * r
        a = jnp.exp(log_a)
        b = jnp.sqrt(-jnp.tanh(log_a) * (a * a + 1.0)) * (i * xc)
        hprev = hcar_ref[:, lo:hi]
        groups = []
        for gi in range(ts // SUBLANES):
            ag = a[gi * SUBLANES:(gi + 1) * SUBLANES]
            bg = b[gi * SUBLANES:(gi + 1) * SUBLANES]
            d = 1
            while d < SUBLANES:
                keep = row8 >= d
                a_sh = jnp.where(keep, pltpu.roll(ag, d, axis=0), 1.0)
                b_sh = jnp.where(keep, pltpu.roll(bg, d, axis=0), 0.0)
                bg = ag * b_sh + bg
                ag = ag * a_sh
                d *= 2
            hg = ag * hprev + bg
            hprev = hg[SUBLANES - 1:SUBLANES]
            groups.append(hg)
        hs = jnp.concatenate(groups, axis=0)
        hcar_ref[:, lo:hi] = hprev
        y_ref[:, lo:hi] = (hs * _gelu_tanh(gate)).astype(BF16)

    o_ref[...] = h + jnp.dot(y_ref[...], wout_ref[...], preferred_element_type=F32)


def _lru_layer(h, batch, seq, g, w_in, conv_w, conv_b, w_a, b_a, w_x, b_x, a_param, w_out):
    t, d = h.shape
    w = w_in.shape[1] // 2
    nblk, bw, _ = w_a.shape
    ts = _pick(seq, 256)
    ns = seq // ts
    wax = jnp.concatenate([w_a, w_x], axis=-1).astype(BF16)
    row = lambda v: v.reshape(1, -1)
    kern = functools.partial(_lru_kernel, ts=ts, w=w, bw=bw)
    return pl.pallas_call(
        kern,
        grid=(batch, ns),
        in_specs=[
            pl.BlockSpec((ts, d), lambda b, s: (b * ns + s, 0)),
            _full((1, d)), _full((d, 2 * w)), _full(conv_w.shape), _full((1, w)),
            _full(wax.shape), _full((1, w)), _full((1, w)), _full((1, w)), _full((w, d)),
        ],
        out_specs=pl.BlockSpec((ts, d), lambda b, s: (b * ns + s, 0)),
        out_shape=jax.ShapeDtypeStruct((t, d), F32),
        scratch_shapes=[pltpu.VMEM((1, w), F32), pltpu.VMEM((SUBLANES + ts, w), F32), pltpu.VMEM((ts, w), BF16)],
        compiler_params=_params(("arbitrary", "arbitrary")),
        name="lru_mixer",
    )(h, row(g), w_in.astype(BF16), conv_w, row(conv_b), wax, row(b_a), row(b_x), row(a_param),
      w_out.astype(BF16))


def _pool_kernel(h_ref, g_ref, win_ref, wg_ref, sc_ref, wout_ref, o_ref, ext_ref, m_ref, *, ts, d, wins):
    s = pl.program_id(1)
    pad = wins[-1]
    gw = d // len(wins)

    @pl.when(s == 0)
    def _():
        ext_ref[0:pad, :] = jnp.zeros((pad, d), F32)

    h = h_ref[...]
    u = _rms(h, g_ref[...]).astype(BF16)
    v = jnp.dot(u, win_ref[...], preferred_element_type=F32)
    ext_ref[pad:pad + ts, :] = v
    n_avail = (s * ts + lax.broadcasted_iota(I32, (ts, 1), 0) + 1).astype(F32)

    cur = ext_ref[...]
    width = 1
    for gi, win in enumerate(wins):
        while width < win:
            cur = cur + pltpu.roll(cur, width, axis=0)
            width *= 2
        lo, hi = gi * gw, (gi + 1) * gw
        mean = cur[pad:, 0:gw] / jnp.minimum(n_avail, float(win))
        pooled = mean - v[:, lo:hi]
        mixed = jnp.dot(pooled.astype(BF16), wg_ref[gi], preferred_element_type=F32)
        m_ref[:, lo:hi] = (mixed * sc_ref[:, lo:hi]).astype(BF16)
        cur = cur[:, gw:]
    ext_ref[0:pad, :] = v[ts - pad:ts]
    o_ref[...] = h + jnp.dot(m_ref[...], wout_ref[...], preferred_element_type=F32)


def _pool_layer(h, batch, seq, g, w_in, w_group, scale, w_out):
    t, d = h.shape
    ts = _pick(seq, 256)
    ns = seq // ts
    pad = POOL_WINDOWS[-1]
    kern = functools.partial(_pool_kernel, ts=ts, d=d, wins=POOL_WINDOWS)
    return pl.pallas_call(
        kern,
        grid=(batch, ns),
        in_specs=[
            pl.BlockSpec((ts, d), lambda b, s: (b * ns + s, 0)),
            _full((1, d)), _full((d, d)), _full(w_group.shape), _full((1, d)), _full((d, d)),
        ],
        out_specs=pl.BlockSpec((ts, d), lambda b, s: (b * ns + s, 0)),
        out_shape=jax.ShapeDtypeStruct((t, d), F32),
        scratch_shapes=[pltpu.VMEM((ts + pad, d), F32), pltpu.VMEM((ts, d), BF16)],
        compiler_params=_params(("arbitrary", "arbitrary")),
        name="pool_mixer",
    )(h, g.reshape(1, d), w_in.astype(BF16), w_group.astype(BF16), scale.reshape(1, d), w_out.astype(BF16))


def _sb_proj_kernel(h_ref, g_ref, wqt_ref, wk_ref, wvt_ref, qt_ref, k_ref, vt_ref):
    u = _rms(h_ref[...], g_ref[...]).astype(BF16)
    nt = (((1,), (1,)), ((), ()))
    qt_ref[...] = lax.dot_general(wqt_ref[...], u, nt, preferred_element_type=F32).astype(BF16)
    k_ref[...] = jnp.dot(u, wk_ref[...], preferred_element_type=F32).astype(BF16)
    vt_ref[...] = lax.dot_general(wvt_ref[...], u, nt, preferred_element_type=F32).astype(BF16)


def _sb_attn_kernel(qt_ref, k_ref, vt_ref, o_ref, lb_ref, zs_ref, a_ref, *, bq, dh, nh):
    qb = pl.program_id(2)
    krow = lax.broadcasted_iota(I32, (bq, bq), 0)
    qcol = lax.broadcasted_iota(I32, (bq, bq), 1)
    upper = (qcol > krow).astype(BF16)
    causal = krow < qcol

    def step(chains, masked, st):
        totals = []
        for c, (hh, j) in enumerate(chains):
            qh = qt_ref[hh * dh:(hh + 1) * dh, :]
            kh = k_ref[pl.ds(j * bq, bq), hh * dh:(hh + 1) * dh]
            z = jnp.dot(kh, qh, preferred_element_type=F32)
            t = jnp.exp2(jnp.abs(z) * (-LOG2E))
            l1m = jnp.minimum(-z, 0.0) - jnp.log(1.0 + t)
            zs_ref[c] = z + l1m
            lm = jnp.where(causal, l1m, 0.0) if masked else l1m
            lb_ref[c] = lm.astype(BF16)
            totals.append(jnp.sum(lm, axis=0, keepdims=True))
        st = list(st)
        for c, (hh, j) in enumerate(chains):
            acc, carry = st[hh]
            suffix = jnp.dot(upper, lb_ref[c], preferred_element_type=F32)
            a = jnp.exp(zs_ref[c] + suffix + carry)
            if masked:
                a = jnp.where(causal, a, 0.0)
            a_ref[c] = a.astype(BF16)
            st[hh] = (acc, carry + totals[c])
        for c, (hh, j) in enumerate(chains):
            acc, carry = st[hh]
            vth = vt_ref[hh * dh:(hh + 1) * dh, pl.ds(j * bq, bq)]
            st[hh] = (acc + jnp.dot(vth, a_ref[c], preferred_element_type=F32), carry)
        return st

    heads = range(nh)
    zero = (jnp.zeros((dh, bq), F32), jnp.zeros((1, bq), F32))
    st = step([(hh, qb) for hh in heads], True, [zero] * nh)

    def pair_body(i, st):
        j = qb - 1 - 2 * i
        return step([(hh, j) for hh in heads] + [(hh, j - 1) for hh in heads], False, st)

    def single_body(i, st):
        return step([(hh, 0) for hh in heads], False, st)

    st = lax.fori_loop(0, lax.shift_right_logical(qb, 1), pair_body, st)
    st = lax.fori_loop(0, qb & 1, single_body, st)
    o_ref[...] = jnp.concatenate([st[hh][0].T for hh in heads], axis=1).astype(BF16)


def _sb_out_kernel(h_ref, o_ref, w_ref, out_ref):
    out_ref[...] = h_ref[...] + jnp.dot(o_ref[...], w_ref[...], preferred_element_type=F32)


def _sb_layer(h, batch, seq, g, w_qkv, w_out):
    t, d = h.shape
    dh = d // SB_HEADS
    scale = 1.0 / math.sqrt(dh)
    wqt = (w_qkv[:, :d] * scale).T.astype(BF16)
    wk = w_qkv[:, d:2 * d].astype(BF16)
    wvt = w_qkv[:, 2 * d:].T.astype(BF16)
    tp = _pick(t, 512)
    qt, k, vt = pl.pallas_call(
        _sb_proj_kernel,
        grid=(t // tp,),
        in_specs=[pl.BlockSpec((tp, d), lambda i: (i, 0)), _full((1, d)), _full((d, d)), _full((d, d)),
                  _full((d, d))],
        out_specs=[pl.BlockSpec((d, tp), lambda i: (0, i)), pl.BlockSpec((tp, d), lambda i: (i, 0)),
                   pl.BlockSpec((d, tp), lambda i: (0, i))],
        out_shape=[jax.ShapeDtypeStruct((d, t), BF16), jax.ShapeDtypeStruct((t, d), BF16),
                   jax.ShapeDtypeStruct((d, t), BF16)],
        compiler_params=_params(("arbitrary",)),
        name="sb_qkv_proj",
    )(h, g.reshape(1, d), wqt, wk, wvt)

    bq = _pick(seq, 256)
    nq = seq // bq
    nh = SB_HEADS_PER_STEP
    grp = nh * dh
    nchain = 2 * nh
    o = pl.pallas_call(
        functools.partial(_sb_attn_kernel, bq=bq, dh=dh, nh=nh),
        grid=(batch, d // grp, nq),
        in_specs=[
            pl.BlockSpec((grp, bq), lambda b, p, q: (p, b * nq + q)),
            pl.BlockSpec((seq, grp), lambda b, p, q: (b, p)),
            pl.BlockSpec((grp, seq), lambda b, p, q: (p, b)),
        ],
        out_specs=pl.BlockSpec((bq, grp), lambda b, p, q: (b * nq + q, p)),
        out_shape=jax.ShapeDtypeStruct((t, d), BF16),
        scratch_shapes=[pltpu.VMEM((nchain, bq, bq), BF16), pltpu.VMEM((nchain, bq, bq), F32),
                        pltpu.VMEM((nchain, bq, bq), BF16)],
        compiler_params=_params(("arbitrary", "arbitrary", "arbitrary")),
        name="sb_attention",
    )(qt, k, vt)

    return pl.pallas_call(
        _sb_out_kernel,
        grid=(t // tp,),
        in_specs=[pl.BlockSpec((tp, d), lambda i: (i, 0)), pl.BlockSpec((tp, d), lambda i: (i, 0)),
                  _full((d, d))],
        out_specs=pl.BlockSpec((tp, d), lambda i: (i, 0)),
        out_shape=jax.ShapeDtypeStruct((t, d), F32),
        compiler_params=_params(("arbitrary",)),
        name="sb_out_proj",
    )(h, o, w_out.astype(BF16))


def _route_kernel(h_ref, g_ref, wr_hi_ref, wr_lo_ref, br_ref, idx_ref, gate_ref, nch_ref, *, n_exp, top_k, ts):
    u = _rms(h_ref[...], g_ref[...])
    u_hi = u.astype(BF16)
    u_lo = (u - u_hi.astype(F32)).astype(BF16)
    logits = (jnp.dot(u_hi, wr_hi_ref[...], preferred_element_type=F32)
              + jnp.dot(u_hi, wr_lo_ref[...], preferred_element_type=F32)
              + jnp.dot(u_lo, wr_hi_ref[...], preferred_element_type=F32)) + br_ref[...]
    tm = logits.shape[0]
    lane = lax.broadcasted_iota(I32, (tm, n_exp), 1)
    kcol = lax.broadcasted_iota(I32, (tm, top_k), 1)
    vals = logits
    top_v, top_i = [], []
    for _ in range(top_k):
        m = jnp.max(vals, axis=-1, keepdims=True)
        sel = jnp.min(jnp.where(vals == m, lane, n_exp), axis=-1, keepdims=True)
        top_v.append(m)
        top_i.append(sel)
        vals = jnp.where(lane == sel, -jnp.inf, vals)
    exps = [jnp.exp(v - top_v[0]) for v in top_v]
    denom = exps[0]
    for e in exps[1:]:
        denom = denom + e
    idx = jnp.zeros((tm, top_k), I32)
    gates = jnp.zeros((tm, top_k), F32)
    for k in range(top_k):
        idx = jnp.where(kcol == k, top_i[k], idx)
        gates = jnp.where(kcol == k, exps[k] / denom, gates)
    idx_ref[...] = idx
    gate_ref[...] = gates
    chosen = jnp.where(lane == top_i[0], 1.0, 0.0)
    for k in range(1, top_k):
        chosen = jnp.where(lane == top_i[k], 1.0, chosen)
    for s in range(tm // ts):
        cnt = jnp.sum(chosen[s * ts:(s + 1) * ts], axis=0, keepdims=True)
        nch_ref[s] = jnp.floor((cnt + (RUN_ALIGN - 1)) * (1.0 / RUN_ALIGN)).astype(I32)


def _rank_kernel(idx_ref, idxt_ref, nchall_ref, pos_ref, post_ref, gst_ref, off_ref, bexp_ref, nxt_ref, meta_ref,
                 run_ref, pstart_ref, *, n_exp, top_k, bm, nbp):
    i = pl.program_id(0)
    tm = idx_ref.shape[0]
    cpb = bm // RUN_ALIGN
    idx = idx_ref[...]
    lane = lax.broadcasted_iota(I32, (tm, n_exp), 1)
    ohs = [(idx[:, k:k + 1] == lane).astype(F32) for k in range(top_k)]
    oh = ohs[0]
    for o in ohs[1:]:
        oh = oh + o
    cnt = jnp.sum(oh, axis=0, keepdims=True)
    nch = jnp.floor((cnt + (RUN_ALIGN - 1)) * (1.0 / RUN_ALIGN))
    er = lax.broadcasted_iota(I32, (n_exp, n_exp), 0)
    ec = lax.broadcasted_iota(I32, (n_exp, n_exp), 1)

    def excl_cumsum_row(v):
        v8 = jnp.broadcast_to(v, (SUBLANES, n_exp)).astype(BF16)
        return jnp.dot(v8, (er < ec).astype(BF16), preferred_element_type=F32)[0:1, :]

    @pl.when(i == 0)
    def _():
        tot = jnp.sum(nchall_ref[...].astype(F32), axis=0)
        nb = jnp.floor((tot + (cpb - 1)) * (1.0 / cpb))
        bstart = excl_cumsum_row(nb)
        bend = bstart + nb
        pstart_ref[...] = bstart * cpb
        blk = lax.broadcasted_iota(I32, (nbp, n_exp), 0).astype(F32)
        be = jnp.sum((bend <= blk).astype(F32), axis=-1, keepdims=True)
        be = jnp.minimum(be, n_exp - 1)
        bexp_ref[...] = be.astype(I32)
        n_used = jnp.sum(nb, axis=-1, keepdims=True)
        lane_e = lax.broadcasted_iota(I32, (nbp, n_exp), 1).astype(F32)
        run_end = jnp.sum(jnp.where(lane_e == be, bend, 0.0), axis=-1, keepdims=True)
        nx = jnp.minimum(jnp.sum((bend <= run_end).astype(F32), axis=-1, keepdims=True), n_exp - 1)
        nxt_ref[...] = jnp.where(run_end < n_used, nx, -1.0).astype(I32)
        last = jnp.where(nb > 0, bend - 1.0, -1.0)
        mrow = lax.broadcasted_iota(I32, (SUBLANES, n_exp), 0)
        meta = jnp.where(mrow == 0, jnp.broadcast_to(n_used, (SUBLANES, n_exp)),
                         jnp.where(mrow == 1, jnp.broadcast_to(last, (SUBLANES, n_exp)), 0.0))
        meta_ref[...] = meta.astype(I32)
        run_ref[...] = jnp.zeros_like(run_ref)

    tr = lax.broadcasted_iota(I32, (tm, tm), 0)
    tc = lax.broadcasted_iota(I32, (tm, tm), 1)
    off = excl_cumsum_row(nch)
    excl = jnp.dot((tc < tr).astype(BF16), oh.astype(BF16), preferred_element_type=F32)
    base = excl + off * RUN_ALIGN
    kcol = lax.broadcasted_iota(I32, (tm, top_k), 1)
    pos = jnp.zeros((tm, top_k), I32)
    for k in range(top_k):
        pk = jnp.sum(ohs[k] * base, axis=-1, keepdims=True).astype(I32)
        pos = jnp.where(kcol == k, pk, pos)
    pos_ref[...] = pos
    idxt = idxt_ref[...]
    sub = lax.broadcasted_iota(I32, (n_exp, tm), 0)
    ohts = [(idxt[k:k + 1, :] == sub).astype(F32) for k in range(top_k)]
    oht = ohts[0]
    for o in ohts[1:]:
        oht = oht + o
    cnt_col = jnp.sum(oht, axis=1, keepdims=True)
    nch_col = jnp.floor((cnt_col + (RUN_ALIGN - 1)) * (1.0 / RUN_ALIGN))
    off_col = jnp.dot((ec < er).astype(BF16), jnp.broadcast_to(nch_col, (n_exp, LANES)).astype(BF16),
                      preferred_element_type=F32)[:, 0:1]
    exclt = jnp.dot(oht.astype(BF16), (tr < tc).astype(BF16), preferred_element_type=F32)
    baset = exclt + off_col * RUN_ALIGN
    krow = lax.broadcasted_iota(I32, (top_k, tm), 0)
    post = jnp.zeros((top_k, tm), I32)
    for k in range(top_k):
        pk = jnp.sum(ohts[k] * baset, axis=0, keepdims=True).astype(I32)
        post = jnp.where(krow == k, pk, post)
    post_ref[...] = post
    gst_ref[0] = (pstart_ref[...] + run_ref[...]).astype(I32)
    off_ref[0] = off.astype(I32)
    run_ref[...] = run_ref[...] + nch


def _run_chunks(gst_ref, off_ref, nch_ref, n_exp, make_copy):
    per_big = BIG_CHUNK // RUN_ALIGN

    def per_expert(e, total):
        n = nch_ref[0, 0, e]
        g0 = gst_ref[0, 0, e]
        o0 = off_ref[0, 0, e]
        nbig = lax.shift_right_logical(n, per_big.bit_length() - 1)

        def big(j, c):
            make_copy(BIG_CHUNK, pl.multiple_of((o0 + j * per_big) * RUN_ALIGN, RUN_ALIGN),
                      pl.multiple_of((g0 + j * per_big) * RUN_ALIGN, RUN_ALIGN)).start()
            return c

        def small(j, c):
            make_copy(RUN_ALIGN, pl.multiple_of((o0 + j) * RUN_ALIGN, RUN_ALIGN),
                      pl.multiple_of((g0 + j) * RUN_ALIGN, RUN_ALIGN)).start()
            return c

        lax.fori_loop(0, nbig, big, 0)
        lax.fori_loop(nbig * per_big, n, small, 0)
        return total + n

    return lax.fori_loop(0, n_exp, per_expert, 0)


def _wait_units(n, make_copy):
    per_wait = WAIT_CHUNK // RUN_ALIGN

    def wait_big(j, c):
        make_copy(WAIT_CHUNK, 0, 0).wait()
        return c

    def wait_small(j, c):
        make_copy(RUN_ALIGN, 0, 0).wait()
        return c

    nbig = lax.shift_right_logical(n, per_wait.bit_length() - 1)
    lax.fori_loop(0, nbig, wait_big, 0)
    lax.fori_loop(nbig * per_wait, n, wait_small, 0)


def _dispatch_kernel(last_ref, gst_ref, off_ref, nch_ref, h_ref, g_ref, post_ref, xs_ref, sbuf, zbuf, pending, sem,
                     zsem, *, n_exp, top_k, bm):
    i = pl.program_id(0)
    nt = pl.num_programs(0)
    slot = i % 2
    rows = sbuf.shape[1]

    @pl.when(i == 0)
    def _():
        zbuf[...] = jnp.zeros_like(zbuf)

        def zero_copy(e):
            return pltpu.make_async_copy(zbuf, xs_ref.at[pl.ds(last_ref[e] * bm, bm)], zsem)

        def start(e, c):
            @pl.when(last_ref[e] >= 0)
            def _():
                zero_copy(e).start()
            return c

        def wait(e, c):
            @pl.when(last_ref[e] >= 0)
            def _():
                zero_copy(e).wait()
            return c

        lax.fori_loop(0, n_exp, start, 0)
        lax.fori_loop(0, n_exp, wait, 0)

    u = _rms(h_ref[...], g_ref[...]).astype(BF16)
    tm = u.shape[0]
    post = post_ref[...]
    q = lax.broadcasted_iota(I32, (rows, tm), 0)
    perm = jnp.zeros((rows, tm), F32)
    for k in range(top_k):
        perm = jnp.where(q == post[k:k + 1, :], 1.0, perm)
    sbuf[slot] = jnp.dot(perm.astype(BF16), u, preferred_element_type=F32)

    def make_copy(s, nrows, src_row, dst_row):
        return pltpu.make_async_copy(sbuf.at[s, pl.ds(src_row, nrows)], xs_ref.at[pl.ds(dst_row, nrows)], sem.at[s])

    n = _run_chunks(gst_ref, off_ref, nch_ref, n_exp, functools.partial(make_copy, slot))

    @pl.when(i > 0)
    def _():
        _wait_units(pending[0], functools.partial(make_copy, 1 - slot))

    pending[0] = n

    @pl.when(i == nt - 1)
    def _():
        _wait_units(n, functools.partial(make_copy, slot))


def _ffn_kernel(bexp_ref, nxt_ref, nused_ref, x_ref, wgu_hbm, bgu_ref, wd_hbm, bd_ref, y_ref, wgu_st, wd_st,
                wgu_bf, wd_bf, sem, *, f, layer):
    b = pl.program_id(0)
    prev = bexp_ref[jnp.maximum(b - 1, 0)]
    active = b < nused_ref[0]

    def fetch(e):
        return (pltpu.make_async_copy(wgu_hbm.at[layer, e], wgu_st, sem.at[0]),
                pltpu.make_async_copy(wd_hbm.at[layer, e], wd_st, sem.at[1]))

    @pl.when(b == 0)
    def _():
        for c in fetch(bexp_ref[0]):
            c.start()

    @pl.when(active & ((b == 0) | (bexp_ref[b] != prev)))
    def _():
        for c in fetch(bexp_ref[b]):
            c.wait()
        wgu_bf[...] = wgu_st[...].astype(BF16)
        wd_bf[...] = wd_st[...].astype(BF16)

        @pl.when(nxt_ref[b] >= 0)
        def _():
            for c in fetch(nxt_ref[b]):
                c.start()

    @pl.when(active)
    def _():
        x = x_ref[...].astype(BF16)
        gu = jnp.dot(x, wgu_bf[...], preferred_element_type=F32) + bgu_ref[0]
        gate = jnp.minimum(gu[:, :f], SWIGLU_LIMIT)
        up = jnp.clip(gu[:, f:], -SWIGLU_LIMIT, SWIGLU_LIMIT)
        glu = gate * _sigmoid(gate * SWIGLU_ALPHA)
        act = ((up + 1.0) * glu).astype(BF16)
        y_ref[...] = jnp.dot(act, wd_bf[...], preferred_element_type=F32) + bd_ref[0]


def _combine_kernel(gst_ref, off_ref, nch_ref, gst_nx, off_nx, nch_nx, h_ref, gate_ref, pos_ref, ys_ref, fg_ref,
                    o_ref, ybuf, sem, *, n_exp, top_k, final):
    i = pl.program_id(0)
    nt = pl.num_programs(0)
    slot = i % 2
    rows = ybuf.shape[1]

    def make_copy(s, nrows, dst_row, src_row):
        return pltpu.make_async_copy(ys_ref.at[pl.ds(src_row, nrows)], ybuf.at[s, pl.ds(dst_row, nrows)], sem.at[s])

    @pl.when(i == 0)
    def _():
        ybuf[...] = jnp.zeros_like(ybuf)
        _run_chunks(gst_ref, off_ref, nch_ref, n_exp, functools.partial(make_copy, 0))

    @pl.when(i + 1 < nt)
    def _():
        _run_chunks(gst_nx, off_nx, nch_nx, n_exp, functools.partial(make_copy, 1 - slot))

    n = lax.fori_loop(0, n_exp, lambda e, c: c + nch_ref[0, 0, e], 0)
    _wait_units(n, functools.partial(make_copy, slot))

    gates = gate_ref[...]
    pos = pos_ref[...]
    tm, d = h_ref.shape
    q = lax.broadcasted_iota(I32, (tm, rows), 1)
    wsel = jnp.zeros((tm, rows), F32)
    for k in range(top_k):
        wsel = jnp.where(q == pos[:, k:k + 1], gates[:, k:k + 1], wsel)
    out = h_ref[...] + jnp.dot(wsel.astype(BF16), ybuf[slot].astype(BF16), preferred_element_type=F32)
    if final:
        out = _rms(out, fg_ref[...])
    o_ref[...] = out


def _moe_layer(h, g, w_router, b_router, w_gate_up, b_gate_up, w_down, b_down, final_g, final, layer):
    t, d = h.shape
    n_exp = w_router.shape[1]
    f = w_down.shape[2]
    top_k = TOP_K
    bm = MOE_BLOCK
    ts = _pick(t, MOE_SORT_TILE)
    nt = t // ts
    srows = -(-(ts * top_k + n_exp * RUN_ALIGN) // LANES) * LANES
    n_blocks = -(-(t * top_k + nt * n_exp * (RUN_ALIGN - 1)) // bm) + n_exp
    n_slots = n_blocks * bm
    nbp = -(-n_blocks // SUBLANES) * SUBLANES

    tm = ts * max(1, _pick(nt, 2))
    tab = jax.ShapeDtypeStruct((nt, 1, n_exp), I32)
    wr_hi = w_router.astype(BF16)
    wr_lo = (w_router - wr_hi.astype(F32)).astype(BF16)
    idx, gates, nch = pl.pallas_call(
        functools.partial(_route_kernel, n_exp=n_exp, top_k=top_k, ts=ts),
        grid=(t // tm,),
        in_specs=[pl.BlockSpec((tm, d), lambda i: (i, 0)), _full((1, d)), _full((d, n_exp)),
                  _full((d, n_exp)), _full((1, n_exp))],
        out_specs=[pl.BlockSpec((tm, top_k), lambda i: (i, 0)), pl.BlockSpec((tm, top_k), lambda i: (i, 0)),
                   pl.BlockSpec((tm // ts, 1, n_exp), lambda i: (i, 0, 0))],
        out_shape=[jax.ShapeDtypeStruct((t, top_k), I32), jax.ShapeDtypeStruct((t, top_k), F32), tab],
        compiler_params=_params(("arbitrary",)),
        name="moe_route",
    )(h, g.reshape(1, d), wr_hi, wr_lo, b_router.reshape(1, n_exp))

    tab_spec = pl.BlockSpec((1, 1, n_exp), lambda i: (i, 0, 0))
    pos, post, gst, off, bexp, nxt, meta = pl.pallas_call(
        functools.partial(_rank_kernel, n_exp=n_exp, top_k=top_k, bm=bm, nbp=nbp),
        grid=(nt,),
        in_specs=[pl.BlockSpec((ts, top_k), lambda i: (i, 0)), pl.BlockSpec((top_k, ts), lambda i: (0, i)),
                  _full((nt, 1, n_exp))],
        out_specs=[pl.BlockSpec((ts, top_k), lambda i: (i, 0)), pl.BlockSpec((top_k, ts), lambda i: (0, i)),
                   tab_spec, tab_spec, _full((nbp, 1)), _full((nbp, 1)), _full((SUBLANES, n_exp))],
        out_shape=[jax.ShapeDtypeStruct((t, top_k), I32), jax.ShapeDtypeStruct((top_k, t), I32), tab, tab,
                   jax.ShapeDtypeStruct((nbp, 1), I32), jax.ShapeDtypeStruct((nbp, 1), I32),
                   jax.ShapeDtypeStruct((SUBLANES, n_exp), I32)],
        scratch_shapes=[pltpu.VMEM((1, n_exp), F32), pltpu.VMEM((1, n_exp), F32)],
        compiler_params=_params(("arbitrary",)),
        name="moe_rank",
    )(idx, idx.T, nch)
    bexp = bexp.reshape(nbp)
    nxt = nxt.reshape(nbp)
    n_used = meta[0, 0:1]
    last_blk = meta[1]

    def smem_tab():
        return pl.BlockSpec((1, 1, n_exp), lambda i, *_: (i, 0, 0), memory_space=pltpu.SMEM)

    def smem_tab_next():
        return pl.BlockSpec((1, 1, n_exp), lambda i, *_: (jnp.minimum(i + 1, nt - 1), 0, 0),
                            memory_space=pltpu.SMEM)

    xs = pl.pallas_call(
        functools.partial(_dispatch_kernel, n_exp=n_exp, top_k=top_k, bm=bm),
        grid_spec=pltpu.PrefetchScalarGridSpec(
            num_scalar_prefetch=1,
            grid=(nt,),
            in_specs=[
                smem_tab(), smem_tab(), smem_tab(),
                pl.BlockSpec((ts, d), lambda i, last: (i, 0)),
                pl.BlockSpec((1, d), lambda i, last: (0, 0)),
                pl.BlockSpec((top_k, ts), lambda i, last: (0, i)),
            ],
            out_specs=pl.BlockSpec(memory_space=pl.ANY),
            scratch_shapes=[pltpu.VMEM((2, srows, d), F32), pltpu.VMEM((bm, d), F32), pltpu.SMEM((1,), I32),
                            pltpu.SemaphoreType.DMA((2,)), pltpu.SemaphoreType.DMA(())],
        ),
        out_shape=jax.ShapeDtypeStruct((n_slots, d), F32),
        compiler_params=_params(("arbitrary",)),
        name="moe_dispatch",
    )(last_blk, gst, off, nch, h, g.reshape(1, d), post)

    def blk(b, nused_ref):
        return jnp.minimum(b, nused_ref[0] - 1)

    ys = pl.pallas_call(
        functools.partial(_ffn_kernel, f=f, layer=layer),
        grid_spec=pltpu.PrefetchScalarGridSpec(
            num_scalar_prefetch=3,
            grid=(n_blocks,),
            in_specs=[
                pl.BlockSpec((bm, d), lambda b, be, nx, nu: (blk(b, nu), 0)),
                pl.BlockSpec(memory_space=pl.ANY),
                pl.BlockSpec((1, 1, 2 * f), lambda b, be, nx, nu: (be[blk(b, nu)], 0, 0)),
                pl.BlockSpec(memory_space=pl.ANY),
                pl.BlockSpec((1, 1, d), lambda b, be, nx, nu: (be[blk(b, nu)], 0, 0)),
            ],
            out_specs=pl.BlockSpec((bm, d), lambda b, be, nx, nu: (blk(b, nu), 0)),
            scratch_shapes=[pltpu.VMEM((d, 2 * f), F32), pltpu.VMEM((f, d), F32),
                            pltpu.VMEM((d, 2 * f), BF16), pltpu.VMEM((f, d), BF16), pltpu.SemaphoreType.DMA((2,))],
        ),
        out_shape=jax.ShapeDtypeStruct((n_slots, d), F32),
        compiler_params=_params(("arbitrary",)),
        name="moe_ffn",
    )(bexp, nxt, n_used, xs, w_gate_up, b_gate_up.reshape(n_exp, 1, 2 * f), w_down, b_down.reshape(n_exp, 1, d))

    return pl.pallas_call(
        functools.partial(_combine_kernel, n_exp=n_exp, top_k=top_k, final=final),
        grid=(nt,),
        in_specs=[
            smem_tab(), smem_tab(), smem_tab(), smem_tab_next(), smem_tab_next(), smem_tab_next(),
            pl.BlockSpec((ts, d), lambda i: (i, 0)),
            pl.BlockSpec((ts, top_k), lambda i: (i, 0)),
            pl.BlockSpec((ts, top_k), lambda i: (i, 0)),
            pl.BlockSpec(memory_space=pl.ANY),
            _full((1, d)),
        ],
        out_specs=pl.BlockSpec((ts, d), lambda i: (i, 0)),
        out_shape=jax.ShapeDtypeStruct((t, d), F32),
        scratch_shapes=[pltpu.VMEM((2, srows, d), F32), pltpu.SemaphoreType.DMA((2,))],
        compiler_params=_params(("arbitrary",)),
        name="moe_combine",
    )(gst, off, nch, gst, off, nch, h, gates, pos, ys, final_g.reshape(1, d))


def kernel(x, mix_norm, ffn_norm, final_norm, lru_w_in, lru_conv_w, lru_conv_b, lru_w_a, lru_b_a, lru_w_x, lru_b_x, lru_a_param, lru_w_out, pool_w_in, pool_w_group, pool_scale, pool_w_out, sb_w_qkv, sb_w_out, moe_w_router, moe_b_router, moe_w_gate_up, moe_b_gate_up, moe_w_down, moe_b_down):
    batch, seq, d = x.shape
    depth = mix_norm.shape[0]
    h = x.reshape(batch * seq, d)
    for layer in range(depth):
        kind = layer % N_MIXERS
        slot = layer // N_MIXERS
        if kind == 0:
            h = _lru_layer(h, batch, seq, mix_norm[layer], lru_w_in[slot], lru_conv_w[slot], lru_conv_b[slot],
                           lru_w_a[slot], lru_b_a[slot], lru_w_x[slot], lru_b_x[slot], lru_a_param[slot],
                           lru_w_out[slot])
        elif kind == 1:
            h = _pool_layer(h, batch, seq, mix_norm[layer], pool_w_in[slot], pool_w_group[slot], pool_scale[slot],
                            pool_w_out[slot])
        else:
            h = _sb_layer(h, batch, seq, mix_norm[layer], sb_w_qkv[slot], sb_w_out[slot])
        h = _moe_layer(h, ffn_norm[layer], moe_w_router[layer], moe_b_router[layer], moe_w_gate_up,
                       moe_b_gate_up[layer], moe_w_down, moe_b_down[layer], final_norm,
                       final=(layer == depth - 1), layer=layer)
    return h.reshape(batch, seq, d)
```

```python
import functools
import math

import jax
import jax.numpy as jnp
from jax import lax
from jax.experimental import pallas as pl
from jax.experimental.pallas import tpu as pltpu

F32 = jnp.float32
BF16 = jnp.bfloat16
I32 = jnp.int32

RMS_EPS = 1e-6
LRU_C = 8.0
POOL_WINDOWS = (2, 4, 8, 16)
SB_HEADS = 16
SB_HEADS_PER_STEP = 4
LOG2E = 1.4426950408889634
TOP_K = 4
SWIGLU_LIMIT = 7.0
SWIGLU_ALPHA = 1.702
N_MIXERS = 3

V7X_VMEM_BYTES = 64 * 1024 * 1024
VMEM_LIMIT_BYTES = V7X_VMEM_BYTES - 8 * 1024 * 1024
SUBLANES = 8
LANES = 128

MOE_BLOCK = 512
MOE_SORT_TILE = 256
RUN_ALIGN = SUBLANES
BIG_CHUNK = 4 * RUN_ALIGN
WAIT_CHUNK = 16 * RUN_ALIGN


def _params(semantics, vmem=VMEM_LIMIT_BYTES):
    return pltpu.CompilerParams(dimension_semantics=semantics, vmem_limit_bytes=vmem)


def _pick(n, pref):
    t = min(n, pref)
    while n % t:
        t //= 2
    return t


def _rms(h, g):
    ms = jnp.mean(h * h, axis=-1, keepdims=True)
    return h * lax.rsqrt(ms + RMS_EPS) * g


def _sigmoid(x):
    return 1.0 / (1.0 + jnp.exp(-x))


def _softplus(x):
    return jnp.maximum(x, 0.0) + jnp.log(1.0 + jnp.exp(-jnp.abs(x)))


def _gelu_tanh(x):
    c = math.sqrt(2.0 / math.pi)
    return 0.5 * x * (1.0 + jnp.tanh(c * (x + 0.044715 * (x * x * x))))


def _full(shape):
    n = len(shape)
    return pl.BlockSpec(shape, lambda *_: (0,) * n)


def _lru_kernel(h_ref, g_ref, win_ref, cw_ref, cb_ref, wax_ref, ba_ref, bx_ref, ap_ref, wout_ref,
                o_ref, hcar_ref, ext_ref, y_ref, *, ts, w, bw):
    s = pl.program_id(1)

    @pl.when(s == 0)
    def _():
        hcar_ref[...] = jnp.zeros_like(hcar_ref)
        ext_ref[0:SUBLANES, :] = jnp.zeros((SUBLANES, w), F32)

    h = h_ref[...]
    u = _rms(h, g_ref[...]).astype(BF16)
    proj = jnp.dot(u, win_ref[...], preferred_element_type=F32)
    row8 = lax.broadcasted_iota(I32, (SUBLANES, bw), 0)
    kw = cw_ref.shape[0]

    for c in range(w // bw):
        lo, hi = c * bw, (c + 1) * bw
        gate = proj[:, lo:hi]
        xb = proj[:, w + lo:w + hi]
        ext_ref[SUBLANES:SUBLANES + ts, lo:hi] = xb
        xc = cb_ref[:, lo:hi] + xb * cw_ref[kw - 1:kw, lo:hi]
        for j in range(1, kw):
            xc = xc + ext_ref[SUBLANES - j:SUBLANES - j + ts, lo:hi] * cw_ref[kw - 1 - j:kw - j, lo:hi]
        ext_ref[0:SUBLANES, lo:hi] = xb[ts - SUBLANES:ts]

        res = jnp.dot(xc.astype(BF16), wax_ref[c], preferred_element_type=F32)
        r = _sigmoid(res[:, :bw] + ba_ref[:, lo:hi])
        i = _sigmoid(res[:, bw:] + bx_ref[:, lo:hi])
        log_a = (-LRU_C * _softplus(-ap_ref[:, lo:hi])) * r
        a = jnp.exp(log_a)
        b = jnp.sqrt(-jnp.tanh(log_a) * (a * a + 1.0)) * (i * xc)
        hprev = hcar_ref[:, lo:hi]
        groups = []
        for gi in range(ts // SUBLANES):
            ag = a[gi * SUBLANES:(gi + 1) * SUBLANES]
            bg = b[gi * SUBLANES:(gi + 1) * SUBLANES]
            d = 1
            while d < SUBLANES:
                keep = row8 >= d
                a_sh = jnp.where(keep, pltpu.roll(ag, d, axis=0), 1.0)
                b_sh = jnp.where(keep, pltpu.roll(bg, d, axis=0), 0.0)
                bg = ag * b_sh + bg
                ag = ag * a_sh
                d *= 2
            hg = ag * hprev + bg
            hprev = hg[SUBLANES - 1:SUBLANES]
            groups.append(hg)
        hs = jnp.concatenate(groups, axis=0)
        hcar_ref[:, lo:hi] = hprev
        y_ref[:, lo:hi] = (hs * _gelu_tanh(gate)).astype(BF16)

    o_ref[...] = h + jnp.dot(y_ref[...], wout_ref[...], preferred_element_type=F32)


def _lru_layer(h, batch, seq, g, w_in, conv_w, conv_b, w_a, b_a, w_x, b_x, a_param, w_out):
    t, d = h.shape
    w = w_in.shape[1] // 2
    nblk, bw, _ = w_a.shape
    ts = _pick(seq, 256)
    ns = seq // ts
    wax = jnp.concatenate([w_a, w_x], axis=-1).astype(BF16)
    row = lambda v: v.reshape(1, -1)
    kern = functools.partial(_lru_kernel, ts=ts, w=w, bw=bw)
    return pl.pallas_call(
        kern,
        grid=(batch, ns),
        in_specs=[
            pl.BlockSpec((ts, d), lambda b, s: (b * ns + s, 0)),
            _full((1, d)), _full((d, 2 * w)), _full(conv_w.shape), _full((1, w)),
            _full(wax.shape), _full((1, w)), _full((1, w)), _full((1, w)), _full((w, d)),
        ],
        out_specs=pl.BlockSpec((ts, d), lambda b, s: (b * ns + s, 0)),
        out_shape=jax.ShapeDtypeStruct((t, d), F32),
        scratch_shapes=[pltpu.VMEM((1, w), F32), pltpu.VMEM((SUBLANES + ts, w), F32), pltpu.VMEM((ts, w), BF16)],
        compiler_params=_params(("arbitrary", "arbitrary")),
        name="lru_mixer",
    )(h, row(g), w_in.astype(BF16), conv_w, row(conv_b), wax, row(b_a), row(b_x), row(a_param),
      w_out.astype(BF16))


def _pool_kernel(h_ref, g_ref, win_ref, wg_ref, sc_ref, wout_ref, o_ref, ext_ref, m_ref, *, ts, d, wins):
    s = pl.program_id(1)
    pad = wins[-1]
    gw = d // len(wins)

    @pl.when(s == 0)
    def _():
        ext_ref[0:pad, :] = jnp.zeros((pad, d), F32)

    h = h_ref[...]
    u = _rms(h, g_ref[...]).astype(BF16)
    v = jnp.dot(u, win_ref[...], preferred_element_type=F32)
    ext_ref[pad:pad + ts, :] = v
    n_avail = (s * ts + lax.broadcasted_iota(I32, (ts, 1), 0) + 1).astype(F32)

    cur = ext_ref[...]
    width = 1
    for gi, win in enumerate(wins):
        while width < win:
            cur = cur + pltpu.roll(cur, width, axis=0)
            width *= 2
        lo, hi = gi * gw, (gi + 1) * gw
        mean = cur[pad:, 0:gw] / jnp.minimum(n_avail, float(win))
        pooled = mean - v[:, lo:hi]
        mixed = jnp.dot(pooled.astype(BF16), wg_ref[gi], preferred_element_type=F32)
        m_ref[:, lo:hi] = (mixed * sc_ref[:, lo:hi]).astype(BF16)
        cur = cur[:, gw:]
    ext_ref[0:pad, :] = v[ts - pad:ts]
    o_ref[...] = h + jnp.dot(m_ref[...], wout_ref[...], preferred_element_type=F32)


def _pool_layer(h, batch, seq, g, w_in, w_group, scale, w_out):
    t, d = h.shape
    ts = _pick(seq, 256)
    ns = seq // ts
    pad = POOL_WINDOWS[-1]
    kern = functools.partial(_pool_kernel, ts=ts, d=d, wins=POOL_WINDOWS)
    return pl.pallas_call(
        kern,
        grid=(batch, ns),
        in_specs=[
            pl.BlockSpec((ts, d), lambda b, s: (b * ns + s, 0)),
            _full((1, d)), _full((d, d)), _full(w_group.shape), _full((1, d)), _full((d, d)),
        ],
        out_specs=pl.BlockSpec((ts, d), lambda b, s: (b * ns + s, 0)),
        out_shape=jax.ShapeDtypeStruct((t, d), F32),
        scratch_shapes=[pltpu.VMEM((ts + pad, d), F32), pltpu.VMEM((ts, d), BF16)],
        compiler_params=_params(("arbitrary", "arbitrary")),
        name="pool_mixer",
    )(h, g.reshape(1, d), w_in.astype(BF16), w_group.astype(BF16), scale.reshape(1, d), w_out.astype(BF16))


def _sb_proj_kernel(h_ref, g_ref, wqt_ref, wk_ref, wvt_ref, qt_ref, k_ref, vt_ref):
    u = _rms(h_ref[...], g_ref[...]).astype(BF16)
    nt = (((1,), (1,)), ((), ()))
    qt_ref[...] = lax.dot_general(wqt_ref[...], u, nt, preferred_element_type=F32).astype(BF16)
    k_ref[...] = jnp.dot(u, wk_ref[...], preferred_element_type=F32).astype(BF16)
    vt_ref[...] = lax.dot_general(wvt_ref[...], u, nt, preferred_element_type=F32).astype(BF16)


def _sb_attn_kernel(qt_ref, k_ref, vt_ref, o_ref, lb_ref, zs_ref, a_ref, *, bq, dh, nh):
    qb = pl.program_id(2)
    krow = lax.broadcasted_iota(I32, (bq, bq), 0)
    qcol = lax.broadcasted_iota(I32, (bq, bq), 1)
    upper = (qcol >= krow).astype(BF16)
    causal = krow < qcol

    def step(chains, masked, st):
        for c, (hh, j) in enumerate(chains):
            qh = qt_ref[hh * dh:(hh + 1) * dh, :]
            kh = k_ref[pl.ds(j * bq, bq), hh * dh:(hh + 1) * dh]
            z = jnp.dot(kh, qh, preferred_element_type=F32)
            t = jnp.exp2(jnp.abs(z) * (-LOG2E))
            l1m = jnp.minimum(-z, 0.0) - jnp.log(1.0 + t)
            zs_ref[c] = z
            lm = jnp.where(causal, l1m, 0.0) if masked else l1m
            lb_ref[c] = lm.astype(BF16)
        st = list(st)
        for c, (hh, j) in enumerate(chains):
            acc, carry = st[hh]
            suffix = jnp.dot(upper, lb_ref[c], preferred_element_type=F32)
            a = jnp.exp(zs_ref[c] + suffix + carry)
            if masked:
                a = jnp.where(causal, a, 0.0)
            a_ref[c] = a.astype(BF16)
            st[hh] = (acc, carry + suffix[0:1, :])
        for c, (hh, j) in enumerate(chains):
            acc, carry = st[hh]
            vth = vt_ref[hh * dh:(hh + 1) * dh, pl.ds(j * bq, bq)]
            st[hh] = (acc + jnp.dot(vth, a_ref[c], preferred_element_type=F32), carry)
        return st

    heads = range(nh)
    zero = (jnp.zeros((dh, bq), F32), jnp.zeros((1, bq), F32))
    st = step([(hh, qb) for hh in heads], True, [zero] * nh)

    def pair_body(i, st):
        j = qb - 1 - 2 * i
        return step([(hh, j) for hh in heads] + [(hh, j - 1) for hh in heads], False, st)

    def single_body(i, st):
        return step([(hh, 0) for hh in heads], False, st)

    st = lax.fori_loop(0, lax.shift_right_logical(qb, 1), pair_body, st)
    st = lax.fori_loop(0, qb & 1, single_body, st)
    o_ref[...] = jnp.concatenate([st[hh][0].T for hh in heads], axis=1).astype(BF16)


def _sb_out_kernel(h_ref, o_ref, w_ref, out_ref):
    out_ref[...] = h_ref[...] + jnp.dot(o_ref[...], w_ref[...], preferred_element_type=F32)


def _sb_layer(h, batch, seq, g, w_qkv, w_out):
    t, d = h.shape
    dh = d // SB_HEADS
    scale = 1.0 / math.sqrt(dh)
    wqt = (w_qkv[:, :d] * scale).T.astype(BF16)
    wk = w_qkv[:, d:2 * d].astype(BF16)
    wvt = w_qkv[:, 2 * d:].T.astype(BF16)
    tp = _pick(t, 512)
    qt, k, vt = pl.pallas_call(
        _sb_proj_kernel,
        grid=(t // tp,),
        in_specs=[pl.BlockSpec((tp, d), lambda i: (i, 0)), _full((1, d)), _full((d, d)), _full((d, d)),
                  _full((d, d))],
        out_specs=[pl.BlockSpec((d, tp), lambda i: (0, i)), pl.BlockSpec((tp, d), lambda i: (i, 0)),
                   pl.BlockSpec((d, tp), lambda i: (0, i))],
        out_shape=[jax.ShapeDtypeStruct((d, t), BF16), jax.ShapeDtypeStruct((t, d), BF16),
                   jax.ShapeDtypeStruct((d, t), BF16)],
        compiler_params=_params(("arbitrary",)),
        name="sb_qkv_proj",
    )(h, g.reshape(1, d), wqt, wk, wvt)

    bq = _pick(seq, 256)
    nq = seq // bq
    nh = SB_HEADS_PER_STEP
    grp = nh * dh
    nchain = 2 * nh
    o = pl.pallas_call(
        functools.partial(_sb_attn_kernel, bq=bq, dh=dh, nh=nh),
        grid=(batch, d // grp, nq),
        in_specs=[
            pl.BlockSpec((grp, bq), lambda b, p, q: (p, b * nq + q)),
            pl.BlockSpec((seq, grp), lambda b, p, q: (b, p)),
            pl.BlockSpec((grp, seq), lambda b, p, q: (p, b)),
        ],
        out_specs=pl.BlockSpec((bq, grp), lambda b, p, q: (b * nq + q, p)),
        out_shape=jax.ShapeDtypeStruct((t, d), BF16),
        scratch_shapes=[pltpu.VMEM((nchain, bq, bq), BF16), pltpu.VMEM((nchain, bq, bq), F32),
                        pltpu.VMEM((nchain, bq, bq), BF16)],
        compiler_params=_params(("arbitrary", "arbitrary", "arbitrary")),
        name="sb_attention",
    )(qt, k, vt)

    return pl.pallas_call(
        _sb_out_kernel,
        grid=(t // tp,),
        in_specs=[pl.BlockSpec((tp, d), lambda i: (i, 0)), pl.BlockSpec((tp, d), lambda i: (i, 0)),
                  _full((d, d))],
        out_specs=pl.BlockSpec((tp, d), lambda i: (i, 0)),
        out_shape=jax.ShapeDtypeStruct((t, d), F32),
        compiler_params=_params(("arbitrary",)),
        name="sb_out_proj",
    )(h, o, w_out.astype(BF16))


def _route_kernel(h_ref, g_ref, wr_hi_ref, wr_lo_ref, br_ref, idxt_ref, gatet_ref, nch_ref, *, n_exp, top_k, ts):
    u = _rms(h_ref[...], g_ref[...])
    u_hi = u.astype(BF16)
    u_lo = (u - u_hi.astype(F32)).astype(BF16)
    nt_dims = (((1,), (1,)), ((), ()))
    logits = (lax.dot_general(wr_hi_ref[...], u_hi, nt_dims, preferred_element_type=F32)
              + lax.dot_general(wr_lo_ref[...], u_hi, nt_dims, preferred_element_type=F32)
              + lax.dot_general(wr_hi_ref[...], u_lo, nt_dims, preferred_element_type=F32)) + br_ref[...]
    tm = logits.shape[1]
    sub = lax.broadcasted_iota(I32, (n_exp, tm), 0)
    krow = lax.broadcasted_iota(I32, (top_k, tm), 0)
    vals = logits
    top_v, top_i = [], []
    for _ in range(top_k):
        m = jnp.max(vals, axis=0, keepdims=True)
        sel = jnp.min(jnp.where(vals == m, sub, n_exp), axis=0, keepdims=True)
        top_v.append(m)
        top_i.append(sel)
        vals = jnp.where(sub == sel, -jnp.inf, vals)
    exps = [jnp.exp(v - top_v[0]) for v in top_v]
    denom = exps[0]
    for e in exps[1:]:
        denom = denom + e
    idx = jnp.zeros((top_k, tm), I32)
    gates = jnp.zeros((top_k, tm), F32)
    for k in range(top_k):
        idx = jnp.where(krow == k, top_i[k], idx)
        gates = jnp.where(krow == k, exps[k] / denom, gates)
    idxt_ref[...] = idx
    gatet_ref[...] = gates
    chosen = jnp.where(sub == top_i[0], 1.0, 0.0)
    for k in range(1, top_k):
        chosen = jnp.where(sub == top_i[k], 1.0, chosen)
    for s in range(tm // ts):
        cnt = jnp.sum(chosen[:, s * ts:(s + 1) * ts], axis=1, keepdims=True)
        nch_ref[s] = jnp.floor((cnt + (RUN_ALIGN - 1)) * (1.0 / RUN_ALIGN)).astype(I32)


def _rank_kernel(idx_ref, idxt_ref, nchall_ref, pos_ref, post_ref, gst_ref, off_ref, bexp_ref, nxt_ref, meta_ref,
                 run_ref, pstart_ref, *, n_exp, top_k, bm, nbp):
    i = pl.program_id(0)
    tm = idx_ref.shape[0]
    cpb = bm // RUN_ALIGN
    idx = idx_ref[...]
    lane = lax.broadcasted_iota(I32, (tm, n_exp), 1)
    ohs = [(idx[:, k:k + 1] == lane).astype(F32) for k in range(top_k)]
    oh = ohs[0]
    for o in ohs[1:]:
        oh = oh + o
    cnt = jnp.sum(oh, axis=0, keepdims=True)
    nch = jnp.floor((cnt + (RUN_ALIGN - 1)) * (1.0 / RUN_ALIGN))
    er = lax.broadcasted_iota(I32, (n_exp, n_exp), 0)
    ec = lax.broadcasted_iota(I32, (n_exp, n_exp), 1)

    def excl_cumsum_row(v):
        v8 = jnp.broadcast_to(v, (SUBLANES, n_exp)).astype(BF16)
        return jnp.dot(v8, (er < ec).astype(BF16), preferred_element_type=F32)[0:1, :]

    @pl.when(i == 0)
    def _():
        tot = jnp.sum(nchall_ref[...].astype(F32), axis=0)
        nb = jnp.floor((tot + (cpb - 1)) * (1.0 / cpb))
        bstart = excl_cumsum_row(nb)
        bend = bstart + nb
        pstart_ref[...] = bstart * cpb
        blk = lax.broadcasted_iota(I32, (nbp, n_exp), 0).astype(F32)
        be = jnp.sum((bend <= blk).astype(F32), axis=-1, keepdims=True)
        be = jnp.minimum(be, n_exp - 1)
        bexp_ref[...] = be.astype(I32)
        n_used = jnp.sum(nb, axis=-1, keepdims=True)
        lane_e = lax.broadcasted_iota(I32, (nbp, n_exp), 1).astype(F32)
        run_end = jnp.sum(jnp.where(lane_e == be, bend, 0.0), axis=-1, keepdims=True)
        nx = jnp.minimum(jnp.sum((bend <= run_end).astype(F32), axis=-1, keepdims=True), n_exp - 1)
        nxt_ref[...] = jnp.where(run_end < n_used, nx, -1.0).astype(I32)
        last = jnp.where(nb > 0, bend - 1.0, -1.0)
        mrow = lax.broadcasted_iota(I32, (SUBLANES, n_exp), 0)
        meta = jnp.where(mrow == 0, jnp.broadcast_to(n_used, (SUBLANES, n_exp)),
                         jnp.where(mrow == 1, jnp.broadcast_to(last, (SUBLANES, n_exp)), 0.0))
        meta_ref[...] = meta.astype(I32)
        run_ref[...] = jnp.zeros_like(run_ref)

    tr = lax.broadcasted_iota(I32, (tm, tm), 0)
    tc = lax.broadcasted_iota(I32, (tm, tm), 1)
    off = excl_cumsum_row(nch)
    excl = jnp.dot((tc < tr).astype(BF16), oh.astype(BF16), preferred_element_type=F32)
    base = excl + off * RUN_ALIGN
    kcol = lax.broadcasted_iota(I32, (tm, top_k), 1)
    pos = jnp.zeros((tm, top_k), I32)
    for k in range(top_k):
        pk = jnp.sum(ohs[k] * base, axis=-1, keepdims=True).astype(I32)
        pos = jnp.where(kcol == k, pk, pos)
    pos_ref[...] = pos
    idxt = idxt_ref[...]
    sub = lax.broadcasted_iota(I32, (n_exp, tm), 0)
    ohts = [(idxt[k:k + 1, :] == sub).astype(F32) for k in range(top_k)]
    oht = ohts[0]
    for o in ohts[1:]:
        oht = oht + o
    cnt_col = jnp.sum(oht, axis=1, keepdims=True)
    nch_col = jnp.floor((cnt_col + (RUN_ALIGN - 1)) * (1.0 / RUN_ALIGN))
    off_col = jnp.dot((ec < er).astype(BF16), jnp.broadcast_to(nch_col, (n_exp, LANES)).astype(BF16),
                      preferred_element_type=F32)[:, 0:1]
    exclt = jnp.dot(oht.astype(BF16), (tr < tc).astype(BF16), preferred_element_type=F32)
    baset = exclt + off_col * RUN_ALIGN
    krow = lax.broadcasted_iota(I32, (top_k, tm), 0)
    post = jnp.zeros((top_k, tm), I32)
    for k in range(top_k):
        pk = jnp.sum(ohts[k] * baset, axis=0, keepdims=True).astype(I32)
        post = jnp.where(krow == k, pk, post)
    post_ref[...] = post
    gst_ref[0] = (pstart_ref[...] + run_ref[...]).astype(I32)
    off_ref[0] = off.astype(I32)
    run_ref[...] = run_ref[...] + nch


def _run_chunks(gst_ref, off_ref, nch_ref, n_exp, make_copy):
    per_big = BIG_CHUNK // RUN_ALIGN

    def per_expert(e, total):
        n = nch_ref[0, 0, e]
        g0 = gst_ref[0, 0, e]
        o0 = off_ref[0, 0, e]
        nbig = lax.shift_right_logical(n, per_big.bit_length() - 1)

        def big(j, c):
            make_copy(BIG_CHUNK, pl.multiple_of((o0 + j * per_big) * RUN_ALIGN, RUN_ALIGN),
                      pl.multiple_of((g0 + j * per_big) * RUN_ALIGN, RUN_ALIGN)).start()
            return c

        def small(j, c):
            make_copy(RUN_ALIGN, pl.multiple_of((o0 + j) * RUN_ALIGN, RUN_ALIGN),
                      pl.multiple_of((g0 + j) * RUN_ALIGN, RUN_ALIGN)).start()
            return c

        lax.fori_loop(0, nbig, big, 0)
        lax.fori_loop(nbig * per_big, n, small, 0)
        return total + n

    return lax.fori_loop(0, n_exp, per_expert, 0)


def _wait_units(n, make_copy):
    per_wait = WAIT_CHUNK // RUN_ALIGN

    def wait_big(j, c):
        make_copy(WAIT_CHUNK, 0, 0).wait()
        return c

    def wait_small(j, c):
        make_copy(RUN_ALIGN, 0, 0).wait()
        return c

    nbig = lax.shift_right_logical(n, per_wait.bit_length() - 1)
    lax.fori_loop(0, nbig, wait_big, 0)
    lax.fori_loop(nbig * per_wait, n, wait_small, 0)


def _dispatch_kernel(last_ref, gst_ref, off_ref, nch_ref, h_ref, g_ref, post_ref, xs_ref, sbuf, zbuf, pending, sem,
                     zsem, *, n_exp, top_k, bm):
    i = pl.program_id(0)
    nt = pl.num_programs(0)
    slot = i % 2
    rows = sbuf.shape[1]

    @pl.when(i == 0)
    def _():
        zbuf[...] = jnp.zeros_like(zbuf)

        def zero_copy(e):
            return pltpu.make_async_copy(zbuf, xs_ref.at[pl.ds(last_ref[e] * bm, bm)], zsem)

        def start(e, c):
            @pl.when(last_ref[e] >= 0)
            def _():
                zero_copy(e).start()
            return c

        def wait(e, c):
            @pl.when(last_ref[e] >= 0)
            def _():
                zero_copy(e).wait()
            return c

        lax.fori_loop(0, n_exp, start, 0)
        lax.fori_loop(0, n_exp, wait, 0)

    u = _rms(h_ref[...], g_ref[...]).astype(BF16)
    tm = u.shape[0]
    post = post_ref[...]
    q = lax.broadcasted_iota(I32, (rows, tm), 0)
    perm = jnp.zeros((rows, tm), F32)
    for k in range(top_k):
        perm = jnp.where(q == post[k:k + 1, :], 1.0, perm)
    sbuf[slot] = jnp.dot(perm.astype(BF16), u, preferred_element_type=F32)

    def make_copy(s, nrows, src_row, dst_row):
        return pltpu.make_async_copy(sbuf.at[s, pl.ds(src_row, nrows)], xs_ref.at[pl.ds(dst_row, nrows)], sem.at[s])

    n = _run_chunks(gst_ref, off_ref, nch_ref, n_exp, functools.partial(make_copy, slot))

    @pl.when(i > 0)
    def _():
        _wait_units(pending[0], functools.partial(make_copy, 1 - slot))

    pending[0] = n

    @pl.when(i == nt - 1)
    def _():
        _wait_units(n, functools.partial(make_copy, slot))


def _ffn_kernel(bexp_ref, nxt_ref, nused_ref, x_ref, wgu_hbm, bgu_ref, wd_hbm, bd_ref, y_ref, wgu_st, wd_st,
                wgu_bf, wd_bf, sem, *, f, layer):
    b = pl.program_id(0)
    prev = bexp_ref[jnp.maximum(b - 1, 0)]
    active = b < nused_ref[0]

    def fetch(e):
        return (pltpu.make_async_copy(wgu_hbm.at[layer, e], wgu_st, sem.at[0]),
                pltpu.make_async_copy(wd_hbm.at[layer, e], wd_st, sem.at[1]))

    @pl.when(b == 0)
    def _():
        for c in fetch(bexp_ref[0]):
            c.start()

    @pl.when(active & ((b == 0) | (bexp_ref[b] != prev)))
    def _():
        for c in fetch(bexp_ref[b]):
            c.wait()
        wgu_bf[...] = wgu_st[...].astype(BF16)
        wd_bf[...] = wd_st[...].astype(BF16)

        @pl.when(nxt_ref[b] >= 0)
        def _():
            for c in fetch(nxt_ref[b]):
                c.start()

    @pl.when(active)
    def _():
        x = x_ref[...].astype(BF16)
        gu = jnp.dot(x, wgu_bf[...], preferred_element_type=F32) + bgu_ref[0]
        gate = jnp.minimum(gu[:, :f], SWIGLU_LIMIT)
        up = jnp.clip(gu[:, f:], -SWIGLU_LIMIT, SWIGLU_LIMIT)
        glu = gate * _sigmoid(gate * SWIGLU_ALPHA)
        act = ((up + 1.0) * glu).astype(BF16)
        y_ref[...] = jnp.dot(act, wd_bf[...], preferred_element_type=F32) + bd_ref[0]


def _combine_kernel(gst_ref, off_ref, nch_ref, gst_nx, off_nx, nch_nx, h_ref, gate_ref, pos_ref, ys_ref, fg_ref,
                    o_ref, ybuf, sem, *, n_exp, top_k, final):
    i = pl.program_id(0)
    nt = pl.num_programs(0)
    slot = i % 2
    rows = ybuf.shape[1]

    def make_copy(s, nrows, dst_row, src_row):
        return pltpu.make_async_copy(ys_ref.at[pl.ds(src_row, nrows)], ybuf.at[s, pl.ds(dst_row, nrows)], sem.at[s])

    @pl.when(i == 0)
    def _():
        ybuf[...] = jnp.zeros_like(ybuf)
        _run_chunks(gst_ref, off_ref, nch_ref, n_exp, functools.partial(make_copy, 0))

    @pl.when(i + 1 < nt)
    def _():
        _run_chunks(gst_nx, off_nx, nch_nx, n_exp, functools.partial(make_copy, 1 - slot))

    n = lax.fori_loop(0, n_exp, lambda e, c: c + nch_ref[0, 0, e], 0)
    _wait_units(n, functools.partial(make_copy, slot))

    gates = gate_ref[...]
    pos = pos_ref[...]
    tm, d = h_ref.shape
    q = lax.broadcasted_iota(I32, (tm, rows), 1)
    wsel = jnp.zeros((tm, rows), F32)
    for k in range(top_k):
        wsel = jnp.where(q == pos[:, k:k + 1], gates[:, k:k + 1], wsel)
    out = h_ref[...] + jnp.dot(wsel.astype(BF16), ybuf[slot].astype(BF16), preferred_element_type=F32)
    if final:
        out = _rms(out, fg_ref[...])
    o_ref[...] = out


def _moe_layer(h, g, w_router, b_router, w_gate_up, b_gate_up, w_down, b_down, final_g, final, layer):
    t, d = h.shape
    n_exp = w_router.shape[1]
    f = w_down.shape[2]
    top_k = TOP_K
    bm = MOE_BLOCK
    ts = _pick(t, MOE_SORT_TILE)
    nt = t // ts
    srows = -(-(ts * top_k + n_exp * RUN_ALIGN) // LANES) * LANES
    n_blocks = -(-(t * top_k + nt * n_exp * (RUN_ALIGN - 1)) // bm) + n_exp
    n_slots = n_blocks * bm
    nbp = -(-n_blocks // SUBLANES) * SUBLANES

    tm = ts * max(1, _pick(nt, 2))
    tab = jax.ShapeDtypeStruct((nt, 1, n_exp), I32)
    wrt = w_router.T
    wr_hi = wrt.astype(BF16)
    wr_lo = (wrt - wr_hi.astype(F32)).astype(BF16)
    idxt, gatest, ncht = pl.pallas_call(
        functools.partial(_route_kernel, n_exp=n_exp, top_k=top_k, ts=ts),
        grid=(t // tm,),
        in_specs=[pl.BlockSpec((tm, d), lambda i: (i, 0)), _full((1, d)), _full((n_exp, d)),
                  _full((n_exp, d)), _full((n_exp, 1))],
        out_specs=[pl.BlockSpec((top_k, tm), lambda i: (0, i)), pl.BlockSpec((top_k, tm), lambda i: (0, i)),
                   pl.BlockSpec((tm // ts, n_exp, 1), lambda i: (i, 0, 0))],
        out_shape=[jax.ShapeDtypeStruct((top_k, t), I32), jax.ShapeDtypeStruct((top_k, t), F32),
                   jax.ShapeDtypeStruct((nt, n_exp, 1), I32)],
        compiler_params=_params(("arbitrary",)),
        name="moe_route",
    )(h, g.reshape(1, d), wr_hi, wr_lo, b_router.reshape(n_exp, 1))
    idx, gates, nch = idxt.T, gatest.T, ncht.reshape(nt, 1, n_exp)

    tab_spec = pl.BlockSpec((1, 1, n_exp), lambda i: (i, 0, 0))
    pos, post, gst, off, bexp, nxt, meta = pl.pallas_call(
        functools.partial(_rank_kernel, n_exp=n_exp, top_k=top_k, bm=bm, nbp=nbp),
        grid=(nt,),
        in_specs=[pl.BlockSpec((ts, top_k), lambda i: (i, 0)), pl.BlockSpec((top_k, ts), lambda i: (0, i)),
                  _full((nt, 1, n_exp))],
        out_specs=[pl.BlockSpec((ts, top_k), lambda i: (i, 0)), pl.BlockSpec((top_k, ts), lambda i: (0, i)),
                   tab_spec, tab_spec, _full((nbp, 1)), _full((nbp, 1)), _full((SUBLANES, n_exp))],
        out_shape=[jax.ShapeDtypeStruct((t, top_k), I32), jax.ShapeDtypeStruct((top_k, t), I32), tab, tab,
                   jax.ShapeDtypeStruct((nbp, 1), I32), jax.ShapeDtypeStruct((nbp, 1), I32),
                   jax.ShapeDtypeStruct((SUBLANES, n_exp), I32)],
        scratch_shapes=[pltpu.VMEM((1, n_exp), F32), pltpu.VMEM((1, n_exp), F32)],
        compiler_params=_params(("arbitrary",)),
        name="moe_rank",
    )(idx, idxt, nch)
    bexp = bexp.reshape(nbp)
    nxt = nxt.reshape(nbp)
    n_used = meta[0, 0:1]
    last_blk = meta[1]

    def smem_tab():
        return pl.BlockSpec((1, 1, n_exp), lambda i, *_: (i, 0, 0), memory_space=pltpu.SMEM)

    def smem_tab_next():
        return pl.BlockSpec((1, 1, n_exp), lambda i, *_: (jnp.minimum(i + 1, nt - 1), 0, 0),
                            memory_space=pltpu.SMEM)

    xs = pl.pallas_call(
        functools.partial(_dispatch_kernel, n_exp=n_exp, top_k=top_k, bm=bm),
        grid_spec=pltpu.PrefetchScalarGridSpec(
            num_scalar_prefetch=1,
            grid=(nt,),
            in_specs=[
                smem_tab(), smem_tab(), smem_tab(),
                pl.BlockSpec((ts, d), lambda i, last: (i, 0)),
                pl.BlockSpec((1, d), lambda i, last: (0, 0)),
                pl.BlockSpec((top_k, ts), lambda i, last: (0, i)),
            ],
            out_specs=pl.BlockSpec(memory_space=pl.ANY),
            scratch_shapes=[pltpu.VMEM((2, srows, d), F32), pltpu.VMEM((bm, d), F32), pltpu.SMEM((1,), I32),
                            pltpu.SemaphoreType.DMA((2,)), pltpu.SemaphoreType.DMA(())],
        ),
        out_shape=jax.ShapeDtypeStruct((n_slots, d), F32),
        compiler_params=_params(("arbitrary",)),
        name="moe_dispatch",
    )(last_blk, gst, off, nch, h, g.reshape(1, d), post)

    def blk(b, nused_ref):
        return jnp.minimum(b, nused_ref[0] - 1)

    ys = pl.pallas_call(
        functools.partial(_ffn_kernel, f=f, layer=layer),
        grid_spec=pltpu.PrefetchScalarGridSpec(
            num_scalar_prefetch=3,
            grid=(n_blocks,),
            in_specs=[
                pl.BlockSpec((bm, d), lambda b, be, nx, nu: (blk(b, nu), 0)),
                pl.BlockSpec(memory_space=pl.ANY),
                pl.BlockSpec((1, 1, 2 * f), lambda b, be, nx, nu: (be[blk(b, nu)], 0, 0)),
                pl.BlockSpec(memory_space=pl.ANY),
                pl.BlockSpec((1, 1, d), lambda b, be, nx, nu: (be[blk(b, nu)], 0, 0)),
            ],
            out_specs=pl.BlockSpec((bm, d), lambda b, be, nx, nu: (blk(b, nu), 0)),
            scratch_shapes=[pltpu.VMEM((d, 2 * f), F32), pltpu.VMEM((f, d), F32),
                            pltpu.VMEM((d, 2 * f), BF16), pltpu.VMEM((f, d), BF16), pltpu.SemaphoreType.DMA((2,))],
        ),
        out_shape=jax.ShapeDtypeStruct((n_slots, d), F32),
        compiler_params=_params(("arbitrary",)),
        name="moe_ffn",
    )(bexp, nxt, n_used, xs, w_gate_up, b_gate_up.reshape(n_exp, 1, 2 * f), w_down, b_down.reshape(n_exp, 1, d))

    return pl.pallas_call(
        functools.partial(_combine_kernel, n_exp=n_exp, top_k=top_k, final=final),
        grid=(nt,),
        in_specs=[
            smem_tab(), smem_tab(), smem_tab(), smem_tab_next(), smem_tab_next(), smem_tab_next(),
            pl.BlockSpec((ts, d), lambda i: (i, 0)),
            pl.BlockSpec((ts, top_k), lambda i: (i, 0)),
            pl.BlockSpec((ts, top_k), lambda i: (i, 0)),
            pl.BlockSpec(memory_space=pl.ANY),
            _full((1, d)),
        ],
        out_specs=pl.BlockSpec((ts, d), lambda i: (i, 0)),
        out_shape=jax.ShapeDtypeStruct((t, d), F32),
        scratch_shapes=[pltpu.VMEM((2, srows, d), F32), pltpu.SemaphoreType.DMA((2,))],
        compiler_params=_params(("arbitrary",)),
        name="moe_combine",
    )(gst, off, nch, gst, off, nch, h, gates, pos, ys, final_g.reshape(1, d))


def kernel(x, mix_norm, ffn_norm, final_norm, lru_w_in, lru_conv_w, lru_conv_b, lru_w_a, lru_b_a, lru_w_x, lru_b_x, lru_a_param, lru_w_out, pool_w_in, pool_w_group, pool_scale, pool_w_out, sb_w_qkv, sb_w_out, moe_w_router, moe_b_router, moe_w_gate_up, moe_b_gate_up, moe_w_down, moe_b_down):
    batch, seq, d = x.shape
    depth = mix_norm.shape[0]
    h = x.reshape(batch * seq, d)
    for layer in range(depth):
        kind = layer % N_MIXERS
        slot = layer // N_MIXERS
        if kind == 0:
            h = _lru_layer(h, batch, seq, mix_norm[layer], lru_w_in[slot], lru_conv_w[slot], lru_conv_b[slot],
                           lru_w_a[slot], lru_b_a[slot], lru_w_x[slot], lru_b_x[slot], lru_a_param[slot],
                           lru_w_out[slot])
        elif kind == 1:
            h = _pool_layer(h, batch, seq, mix_norm[layer], pool_w_in[slot], pool_w_group[slot], pool_scale[slot],
                            pool_w_out[slot])
        else:
            h = _sb_layer(h, batch, seq, mix_norm[layer], sb_w_qkv[slot], sb_w_out[slot])
        h = _moe_layer(h, ffn_norm[layer], moe_w_router[layer], moe_b_router[layer], moe_w_gate_up,
                       moe_b_gate_up[layer], moe_w_down, moe_b_down[layer], final_norm,
                       final=(layer == depth - 1), layer=layer)
    return h.reshape(batch, seq, d)
```

```python
import functools
import math

import jax
import jax.numpy as jnp
from jax import lax
from jax.experimental import pallas as pl
from jax.experimental.pallas import tpu as pltpu

F32 = jnp.float32
BF16 = jnp.bfloat16
I32 = jnp.int32

RMS_EPS = 1e-6
LRU_C = 8.0
POOL_WINDOWS = (2, 4, 8, 16)
SB_HEADS = 16
SB_HEADS_PER_STEP = 4
LOG2E = 1.4426950408889634
TOP_K = 4
SWIGLU_LIMIT = 7.0
SWIGLU_ALPHA = 1.702
N_MIXERS = 3

V7X_VMEM_BYTES = 64 * 1024 * 1024
VMEM_LIMIT_BYTES = V7X_VMEM_BYTES - 8 * 1024 * 1024
SUBLANES = 8
LANES = 128

MOE_BLOCK = 512
MOE_SORT_TILE = 256
RUN_ALIGN = SUBLANES
BIG_CHUNK = 4 * RUN_ALIGN
WAIT_CHUNK = 16 * RUN_ALIGN


def _params(semantics, vmem=VMEM_LIMIT_BYTES):
    return pltpu.CompilerParams(dimension_semantics=semantics, vmem_limit_bytes=vmem)


def _pick(n, pref):
    t = min(n, pref)
    while n % t:
        t //= 2
    return t


def _rms(h, g):
    ms = jnp.mean(h * h, axis=-1, keepdims=True)
    return h * lax.rsqrt(ms + RMS_EPS) * g


def _sigmoid(x):
    return 1.0 / (1.0 + jnp.exp(-x))


def _softplus(x):
    return jnp.maximum(x, 0.0) + jnp.log(1.0 + jnp.exp(-jnp.abs(x)))


def _gelu_tanh(x):
    c = math.sqrt(2.0 / math.pi)
    return 0.5 * x * (1.0 + jnp.tanh(c * (x + 0.044715 * (x * x * x))))


def _full(shape):
    n = len(shape)
    return pl.BlockSpec(shape, lambda *_: (0,) * n)


def _lru_kernel(h_ref, g_ref, win_ref, cw_ref, cb_ref, wax_ref, ba_ref, bx_ref, ap_ref, wout_ref,
                o_ref, hcar_ref, ext_ref, y_ref, *, ts, w, bw):
    s = pl.program_id(1)

    @pl.when(s == 0)
    def _():
        hcar_ref[...] = jnp.zeros_like(hcar_ref)
        ext_ref[0:SUBLANES, :] = jnp.zeros((SUBLANES, w), F32)

    h = h_ref[...]
    u = _rms(h, g_ref[...]).astype(BF16)
    proj = jnp.dot(u, win_ref[...], preferred_element_type=F32)
    row8 = lax.broadcasted_iota(I32, (SUBLANES, bw), 0)
    kw = cw_ref.shape[0]

    for c in range(w // bw):
        lo, hi = c * bw, (c + 1) * bw
        gate = proj[:, lo:hi]
        xb = proj[:, w + lo:w + hi]
        ext_ref[SUBLANES:SUBLANES + ts, lo:hi] = xb
        xc = cb_ref[:, lo:hi] + xb * cw_ref[kw - 1:kw, lo:hi]
        for j in range(1, kw):
            xc = xc + ext_ref[SUBLANES - j:SUBLANES - j + ts, lo:hi] * cw_ref[kw - 1 - j:kw - j, lo:hi]
        ext_ref[0:SUBLANES, lo:hi] = xb[ts - SUBLANES:ts]

        res = jnp.dot(xc.astype(BF16), wax_ref[c], preferred_element_type=F32)
        r = _sigmoid(res[:, :bw] + ba_ref[:, lo:hi])
        i = _sigmoid(res[:, bw:] + bx_ref[:, lo:hi])
        log_a = (-LRU_C * _softplus(-ap_ref[:, lo:hi])) * r
        a = jnp.exp(log_a)
        b = jnp.sqrt(-jnp.tanh(log_a) * (a * a + 1.0)) * (i * xc)
        hprev = hcar_ref[:, lo:hi]
        groups = []
        for gi in range(ts // SUBLANES):
            ag = a[gi * SUBLANES:(gi + 1) * SUBLANES]
            bg = b[gi * SUBLANES:(gi + 1) * SUBLANES]
            d = 1
            while d < SUBLANES:
                keep = row8 >= d
                a_sh = jnp.where(keep, pltpu.roll(ag, d, axis=0), 1.0)
                b_sh = jnp.where(keep, pltpu.roll(bg, d, axis=0), 0.0)
                bg = ag * b_sh + bg
                ag = ag * a_sh
                d *= 2
            hg = ag * hprev + bg
            hprev = hg[SUBLANES - 1:SUBLANES]
            groups.append(hg)
        hs = jnp.concatenate(groups, axis=0)
        hcar_ref[:, lo:hi] = hprev
        y_ref[:, lo:hi] = (hs * _gelu_tanh(gate)).astype(BF16)

    o_ref[...] = h + jnp.dot(y_ref[...], wout_ref[...], preferred_element_type=F32)


def _lru_layer(h, batch, seq, g, w_in, conv_w, conv_b, w_a, b_a, w_x, b_x, a_param, w_out):
    t, d = h.shape
    w = w_in.shape[1] // 2
    nblk, bw, _ = w_a.shape
    ts = _pick(seq, 256)
    ns = seq // ts
    wax = jnp.concatenate([w_a, w_x], axis=-1).astype(BF16)
    row = lambda v: v.reshape(1, -1)
    kern = functools.partial(_lru_kernel, ts=ts, w=w, bw=bw)
    return pl.pallas_call(
        kern,
        grid=(batch, ns),
        in_specs=[
            pl.BlockSpec((ts, d), lambda b, s: (b * ns + s, 0)),
            _full((1, d)), _full((d, 2 * w)), _full(conv_w.shape), _full((1, w)),
            _full(wax.shape), _full((1, w)), _full((1, w)), _full((1, w)), _full((w, d)),
        ],
        out_specs=pl.BlockSpec((ts, d), lambda b, s: (b * ns + s, 0)),
        out_shape=jax.ShapeDtypeStruct((t, d), F32),
        scratch_shapes=[pltpu.VMEM((1, w), F32), pltpu.VMEM((SUBLANES + ts, w), F32), pltpu.VMEM((ts, w), BF16)],
        compiler_params=_params(("arbitrary", "arbitrary")),
        name="lru_mixer",
    )(h, row(g), w_in.astype(BF16), conv_w, row(conv_b), wax, row(b_a), row(b_x), row(a_param),
      w_out.astype(BF16))


def _pool_kernel(h_ref, g_ref, win_ref, wg_ref, sc_ref, wout_ref, o_ref, ext_ref, m_ref, *, ts, d, wins):
    s = pl.program_id(1)
    pad = wins[-1]
    gw = d // len(wins)

    @pl.when(s == 0)
    def _():
        ext_ref[0:pad, :] = jnp.zeros((pad, d), F32)

    h = h_ref[...]
    u = _rms(h, g_ref[...]).astype(BF16)
    v = jnp.dot(u, win_ref[...], preferred_element_type=F32)
    ext_ref[pad:pad + ts, :] = v
    n_avail = (s * ts + lax.broadcasted_iota(I32, (ts, 1), 0) + 1).astype(F32)

    cur = ext_ref[...]
    width = 1
    for gi, win in enumerate(wins):
        while width < win:
            cur = cur + pltpu.roll(cur, width, axis=0)
            width *= 2
        lo, hi = gi * gw, (gi + 1) * gw
        mean = cur[pad:, 0:gw] / jnp.minimum(n_avail, float(win))
        pooled = mean - v[:, lo:hi]
        mixed = jnp.dot(pooled.astype(BF16), wg_ref[gi], preferred_element_type=F32)
        m_ref[:, lo:hi] = (mixed * sc_ref[:, lo:hi]).astype(BF16)
        cur = cur[:, gw:]
    ext_ref[0:pad, :] = v[ts - pad:ts]
    o_ref[...] = h + jnp.dot(m_ref[...], wout_ref[...], preferred_element_type=F32)


def _pool_layer(h, batch, seq, g, w_in, w_group, scale, w_out):
    t, d = h.shape
    ts = _pick(seq, 256)
    ns = seq // ts
    pad = POOL_WINDOWS[-1]
    kern = functools.partial(_pool_kernel, ts=ts, d=d, wins=POOL_WINDOWS)
    return pl.pallas_call(
        kern,
        grid=(batch, ns),
        in_specs=[
            pl.BlockSpec((ts, d), lambda b, s: (b * ns + s, 0)),
            _full((1, d)), _full((d, d)), _full(w_group.shape), _full((1, d)), _full((d, d)),
        ],
        out_specs=pl.BlockSpec((ts, d), lambda b, s: (b * ns + s, 0)),
        out_shape=jax.ShapeDtypeStruct((t, d), F32),
        scratch_shapes=[pltpu.VMEM((ts + pad, d), F32), pltpu.VMEM((ts, d), BF16)],
        compiler_params=_params(("arbitrary", "arbitrary")),
        name="pool_mixer",
    )(h, g.reshape(1, d), w_in.astype(BF16), w_group.astype(BF16), scale.reshape(1, d), w_out.astype(BF16))


def _sb_proj_kernel(h_ref, g_ref, wqt_ref, wk_ref, wvt_ref, qt_ref, k_ref, vt_ref):
    u = _rms(h_ref[...], g_ref[...]).astype(BF16)
    nt = (((1,), (1,)), ((), ()))
    qt_ref[...] = lax.dot_general(wqt_ref[...], u, nt, preferred_element_type=F32).astype(BF16)
    k_ref[...] = jnp.dot(u, wk_ref[...], preferred_element_type=F32).astype(BF16)
    vt_ref[...] = lax.dot_general(wvt_ref[...], u, nt, preferred_element_type=F32).astype(BF16)


def _sb_attn_kernel(qt_ref, k_ref, vt_ref, o_ref, lb_ref, zs_ref, a_ref, *, bq, dh, nh):
    qb = pl.program_id(2)
    krow = lax.broadcasted_iota(I32, (bq, bq), 0)
    qcol = lax.broadcasted_iota(I32, (bq, bq), 1)
    upper = (qcol >= krow).astype(BF16)
    causal = krow < qcol

    def step(chains, masked, st):
        for c, (hh, j) in enumerate(chains):
            qh = qt_ref[hh * dh:(hh + 1) * dh, :]
            kh = k_ref[pl.ds(j * bq, bq), hh * dh:(hh + 1) * dh]
            z = jnp.dot(kh, qh, preferred_element_type=F32)
            t = jnp.exp2(jnp.abs(z) * (-LOG2E))
            l1m = jnp.minimum(-z, 0.0) - jnp.log(1.0 + t)
            zs_ref[c] = z
            lm = jnp.where(causal, l1m, 0.0) if masked else l1m
            lb_ref[c] = lm.astype(BF16)
        st = list(st)
        for c, (hh, j) in enumerate(chains):
            acc, carry = st[hh]
            suffix = jnp.dot(upper, lb_ref[c], preferred_element_type=F32)
            a = jnp.exp(zs_ref[c] + suffix + carry)
            if masked:
                a = jnp.where(causal, a, 0.0)
            a_ref[c] = a.astype(BF16)
            st[hh] = (acc, carry + suffix[0:1, :])
        for c, (hh, j) in enumerate(chains):
            acc, carry = st[hh]
            vth = vt_ref[hh * dh:(hh + 1) * dh, pl.ds(j * bq, bq)]
            st[hh] = (acc + jnp.dot(vth, a_ref[c], preferred_element_type=F32), carry)
        return st

    heads = range(nh)
    zero = (jnp.zeros((dh, bq), F32), jnp.zeros((1, bq), F32))
    st = step([(hh, qb) for hh in heads], True, [zero] * nh)

    def pair_body(i, st):
        j = qb - 1 - 2 * i
        return step([(hh, j) for hh in heads] + [(hh, j - 1) for hh in heads], False, st)

    def single_body(i, st):
        return step([(hh, 0) for hh in heads], False, st)

    st = lax.fori_loop(0, lax.shift_right_logical(qb, 1), pair_body, st)
    st = lax.fori_loop(0, qb & 1, single_body, st)
    o_ref[...] = jnp.concatenate([st[hh][0].T for hh in heads], axis=1).astype(BF16)


def _sb_out_kernel(h_ref, o_ref, w_ref, out_ref):
    out_ref[...] = h_ref[...] + jnp.dot(o_ref[...], w_ref[...], preferred_element_type=F32)


def _sb_layer(h, batch, seq, g, w_qkv, w_out):
    t, d = h.shape
    dh = d // SB_HEADS
    scale = 1.0 / math.sqrt(dh)
    wqt = (w_qkv[:, :d] * scale).T.astype(BF16)
    wk = w_qkv[:, d:2 * d].astype(BF16)
    wvt = w_qkv[:, 2 * d:].T.astype(BF16)
    tp = _pick(t, 512)
    qt, k, vt = pl.pallas_call(
        _sb_proj_kernel,
        grid=(t // tp,),
        in_specs=[pl.BlockSpec((tp, d), lambda i: (i, 0)), _full((1, d)), _full((d, d)), _full((d, d)),
                  _full((d, d))],
        out_specs=[pl.BlockSpec((d, tp), lambda i: (0, i)), pl.BlockSpec((tp, d), lambda i: (i, 0)),
                   pl.BlockSpec((d, tp), lambda i: (0, i))],
        out_shape=[jax.ShapeDtypeStruct((d, t), BF16), jax.ShapeDtypeStruct((t, d), BF16),
                   jax.ShapeDtypeStruct((d, t), BF16)],
        compiler_params=_params(("arbitrary",)),
        name="sb_qkv_proj",
    )(h, g.reshape(1, d), wqt, wk, wvt)

    bq = _pick(seq, 256)
    nq = seq // bq
    nh = SB_HEADS_PER_STEP
    grp = nh * dh
    nchain = 2 * nh
    o = pl.pallas_call(
        functools.partial(_sb_attn_kernel, bq=bq, dh=dh, nh=nh),
        grid=(batch, d // grp, nq),
        in_specs=[
            pl.BlockSpec((grp, bq), lambda b, p, q: (p, b * nq + q)),
            pl.BlockSpec((seq, grp), lambda b, p, q: (b, p)),
            pl.BlockSpec((grp, seq), lambda b, p, q: (p, b)),
        ],
        out_specs=pl.BlockSpec((bq, grp), lambda b, p, q: (b * nq + q, p)),
        out_shape=jax.ShapeDtypeStruct((t, d), BF16),
        scratch_shapes=[pltpu.VMEM((nchain, bq, bq), BF16), pltpu.VMEM((nchain, bq, bq), F32),
                        pltpu.VMEM((nchain, bq, bq), BF16)],
        compiler_params=_params(("arbitrary", "arbitrary", "arbitrary")),
        name="sb_attention",
    )(qt, k, vt)

    return pl.pallas_call(
        _sb_out_kernel,
        grid=(t // tp,),
        in_specs=[pl.BlockSpec((tp, d), lambda i: (i, 0)), pl.BlockSpec((tp, d), lambda i: (i, 0)),
                  _full((d, d))],
        out_specs=pl.BlockSpec((tp, d), lambda i: (i, 0)),
        out_shape=jax.ShapeDtypeStruct((t, d), F32),
        compiler_params=_params(("arbitrary",)),
        name="sb_out_proj",
    )(h, o, w_out.astype(BF16))


def _route_kernel(h_ref, g_ref, wr_hi_ref, wr_lo_ref, br_ref, idxt_ref, gatet_ref, nch_ref, *, n_exp, top_k, ts):
    u = _rms(h_ref[...], g_ref[...])
    u_hi = u.astype(BF16)
    u_lo = (u - u_hi.astype(F32)).astype(BF16)
    nt_dims = (((1,), (1,)), ((), ()))
    logits = (lax.dot_general(wr_hi_ref[...], u_hi, nt_dims, preferred_element_type=F32)
              + lax.dot_general(wr_lo_ref[...], u_hi, nt_dims, preferred_element_type=F32)
              + lax.dot_general(wr_hi_ref[...], u_lo, nt_dims, preferred_element_type=F32)) + br_ref[...]
    tm = logits.shape[1]
    sub = lax.broadcasted_iota(I32, (n_exp, tm), 0)
    krow = lax.broadcasted_iota(I32, (top_k, tm), 0)
    vals = logits
    top_v, top_i = [], []
    for _ in range(top_k):
        m = jnp.max(vals, axis=0, keepdims=True)
        sel = jnp.min(jnp.where(vals == m, sub, n_exp), axis=0, keepdims=True)
        top_v.append(m)
        top_i.append(sel)
        vals = jnp.where(sub == sel, -jnp.inf, vals)
    exps = [jnp.exp(v - top_v[0]) for v in top_v]
    denom = exps[0]
    for e in exps[1:]:
        denom = denom + e
    idx = jnp.zeros((top_k, tm), I32)
    gates = jnp.zeros((top_k, tm), F32)
    for k in range(top_k):
        idx = jnp.where(krow == k, top_i[k], idx)
        gates = jnp.where(krow == k, exps[k] / denom, gates)
    idxt_ref[...] = idx
    gatet_ref[...] = gates
    chosen = jnp.where(sub == top_i[0], 1.0, 0.0)
    for k in range(1, top_k):
        chosen = jnp.where(sub == top_i[k], 1.0, chosen)
    for s in range(tm // ts):
        cnt = jnp.sum(chosen[:, s * ts:(s + 1) * ts], axis=1, keepdims=True)
        nch_ref[s] = jnp.floor((cnt + (RUN_ALIGN - 1)) * (1.0 / RUN_ALIGN)).astype(I32)


def _rank_kernel(idx_ref, idxt_ref, nchall_ref, pos_ref, post_ref, blr_ref, bgr_ref, slr_ref, sgr_ref, cnt_ref,
                 bexp_ref, nxt_ref, meta_ref, run_ref, pstart_ref, *, n_exp, top_k, bm, nbp):
    i = pl.program_id(0)
    tm = idx_ref.shape[0]
    cpb = bm // RUN_ALIGN
    idx = idx_ref[...]
    lane = lax.broadcasted_iota(I32, (tm, n_exp), 1)
    ohs = [(idx[:, k:k + 1] == lane).astype(F32) for k in range(top_k)]
    oh = ohs[0]
    for o in ohs[1:]:
        oh = oh + o
    cnt = jnp.sum(oh, axis=0, keepdims=True)
    nch = jnp.floor((cnt + (RUN_ALIGN - 1)) * (1.0 / RUN_ALIGN))
    er = lax.broadcasted_iota(I32, (n_exp, n_exp), 0)
    ec = lax.broadcasted_iota(I32, (n_exp, n_exp), 1)

    def excl_cumsum_row(v):
        v8 = jnp.broadcast_to(v, (SUBLANES, n_exp)).astype(BF16)
        return jnp.dot(v8, (er < ec).astype(BF16), preferred_element_type=F32)[0:1, :]

    @pl.when(i == 0)
    def _():
        tot = jnp.sum(nchall_ref[...].astype(F32), axis=0)
        nb = jnp.floor((tot + (cpb - 1)) * (1.0 / cpb))
        bstart = excl_cumsum_row(nb)
        bend = bstart + nb
        pstart_ref[...] = bstart * cpb
        blk = lax.broadcasted_iota(I32, (nbp, n_exp), 0).astype(F32)
        be = jnp.sum((bend <= blk).astype(F32), axis=-1, keepdims=True)
        be = jnp.minimum(be, n_exp - 1)
        bexp_ref[...] = be.astype(I32)
        n_used = jnp.sum(nb, axis=-1, keepdims=True)
        lane_e = lax.broadcasted_iota(I32, (nbp, n_exp), 1).astype(F32)
        run_end = jnp.sum(jnp.where(lane_e == be, bend, 0.0), axis=-1, keepdims=True)
        nx = jnp.minimum(jnp.sum((bend <= run_end).astype(F32), axis=-1, keepdims=True), n_exp - 1)
        nxt_ref[...] = jnp.where(run_end < n_used, nx, -1.0).astype(I32)
        last = jnp.where(nb > 0, bend - 1.0, -1.0)
        mrow = lax.broadcasted_iota(I32, (SUBLANES, n_exp), 0)
        meta = jnp.where(mrow == 0, jnp.broadcast_to(n_used, (SUBLANES, n_exp)),
                         jnp.where(mrow == 1, jnp.broadcast_to(last, (SUBLANES, n_exp)), 0.0))
        meta_ref[...] = meta.astype(I32)
        run_ref[...] = jnp.zeros_like(run_ref)

    tr = lax.broadcasted_iota(I32, (tm, tm), 0)
    tc = lax.broadcasted_iota(I32, (tm, tm), 1)
    off = excl_cumsum_row(nch)
    excl = jnp.dot((tc < tr).astype(BF16), oh.astype(BF16), preferred_element_type=F32)
    base = excl + off * RUN_ALIGN
    kcol = lax.broadcasted_iota(I32, (tm, top_k), 1)
    pos = jnp.zeros((tm, top_k), I32)
    for k in range(top_k):
        pk = jnp.sum(ohs[k] * base, axis=-1, keepdims=True).astype(I32)
        pos = jnp.where(kcol == k, pk, pos)
    pos_ref[...] = pos
    idxt = idxt_ref[...]
    sub = lax.broadcasted_iota(I32, (n_exp, tm), 0)
    ohts = [(idxt[k:k + 1, :] == sub).astype(F32) for k in range(top_k)]
    oht = ohts[0]
    for o in ohts[1:]:
        oht = oht + o
    cnt_col = jnp.sum(oht, axis=1, keepdims=True)
    nch_col = jnp.floor((cnt_col + (RUN_ALIGN - 1)) * (1.0 / RUN_ALIGN))
    off_col = jnp.dot((ec < er).astype(BF16), jnp.broadcast_to(nch_col, (n_exp, LANES)).astype(BF16),
                      preferred_element_type=F32)[:, 0:1]
    exclt = jnp.dot(oht.astype(BF16), (tr < tc).astype(BF16), preferred_element_type=F32)
    baset = exclt + off_col * RUN_ALIGN
    krow = lax.broadcasted_iota(I32, (top_k, tm), 0)
    post = jnp.zeros((top_k, tm), I32)
    for k in range(top_k):
        pk = jnp.sum(ohts[k] * baset, axis=0, keepdims=True).astype(I32)
        post = jnp.where(krow == k, pk, post)
    post_ref[...] = post

    gst = pstart_ref[...] + run_ref[...]
    per_big = BIG_CHUNK // RUN_ALIGN
    nbig = jnp.floor(nch * (1.0 / per_big))
    nsm = nch - nbig * per_big

    def copy_list(count, loc0, glob0, stride, rows):
        before = excl_cumsum_row(count)
        j = lax.broadcasted_iota(I32, (rows, n_exp), 0).astype(F32)
        owner = jnp.sum(((before + count) <= j).astype(F32), axis=-1, keepdims=True)
        mine = lax.broadcasted_iota(I32, (rows, n_exp), 1).astype(F32) == owner
        pick = lambda v: jnp.sum(jnp.where(mine, v, 0.0), axis=-1, keepdims=True)
        piece = (j[:, 0:1] - pick(before)) * stride
        return (pick(loc0) + piece).astype(I32), (pick(glob0) + piece).astype(I32)

    blr_ref[0], bgr_ref[0] = copy_list(nbig, off, gst, per_big, blr_ref.shape[1])
    slr_ref[0], sgr_ref[0] = copy_list(nsm, off + nbig * per_big, gst + nbig * per_big, 1, slr_ref.shape[1])
    crow = lax.broadcasted_iota(I32, (SUBLANES, 1), 0)
    cnt_ref[0] = jnp.where(crow == 0, jnp.sum(nbig, axis=-1, keepdims=True),
                           jnp.where(crow == 1, jnp.sum(nsm, axis=-1, keepdims=True), 0.0)).astype(I32)
    run_ref[...] = run_ref[...] + nch


def _copy_units(lists):
    cnt_ref = lists[0]
    return cnt_ref[0, 0, 0] * (BIG_CHUNK // RUN_ALIGN) + cnt_ref[0, 0, 1]


def _start_copies(lists, make_copy):
    cnt_ref, blr_ref, bgr_ref, slr_ref, sgr_ref = lists

    def start(nrows, loc_ref, glob_ref):
        def body(j, c):
            make_copy(nrows, pl.multiple_of(loc_ref[0, 0, j] * RUN_ALIGN, RUN_ALIGN),
                      pl.multiple_of(glob_ref[0, 0, j] * RUN_ALIGN, RUN_ALIGN)).start()
            return c
        return body

    lax.fori_loop(0, cnt_ref[0, 0, 0], start(BIG_CHUNK, blr_ref, bgr_ref), 0)
    lax.fori_loop(0, cnt_ref[0, 0, 1], start(RUN_ALIGN, slr_ref, sgr_ref), 0)


def _wait_units(n, make_copy):
    per_wait = WAIT_CHUNK // RUN_ALIGN

    def wait_big(j, c):
        make_copy(WAIT_CHUNK, 0, 0).wait()
        return c

    def wait_small(j, c):
        make_copy(RUN_ALIGN, 0, 0).wait()
        return c

    nbig = lax.shift_right_logical(n, per_wait.bit_length() - 1)
    lax.fori_loop(0, nbig, wait_big, 0)
    lax.fori_loop(nbig * per_wait, n, wait_small, 0)


def _dispatch_kernel(last_ref, cnt_ref, blr_ref, bgr_ref, slr_ref, sgr_ref, h_ref, g_ref, post_ref, xs_ref, sbuf,
                     zbuf, pending, sem, zsem, *, n_exp, top_k, bm):
    i = pl.program_id(0)
    nt = pl.num_programs(0)
    slot = i % 2
    rows = sbuf.shape[1]

    @pl.when(i == 0)
    def _():
        zbuf[...] = jnp.zeros_like(zbuf)

        def zero_copy(e):
            return pltpu.make_async_copy(zbuf, xs_ref.at[pl.ds(last_ref[e] * bm, bm)], zsem)

        def start(e, c):
            @pl.when(last_ref[e] >= 0)
            def _():
                zero_copy(e).start()
            return c

        def wait(e, c):
            @pl.when(last_ref[e] >= 0)
            def _():
                zero_copy(e).wait()
            return c

        lax.fori_loop(0, n_exp, start, 0)
        lax.fori_loop(0, n_exp, wait, 0)

    u = _rms(h_ref[...], g_ref[...]).astype(BF16)
    tm = u.shape[0]
    post = post_ref[...]
    q = lax.broadcasted_iota(I32, (rows, tm), 0)
    perm = jnp.zeros((rows, tm), F32)
    for k in range(top_k):
        perm = jnp.where(q == post[k:k + 1, :], 1.0, perm)
    sbuf[slot] = jnp.dot(perm.astype(BF16), u, preferred_element_type=F32)

    def make_copy(s, nrows, src_row, dst_row):
        return pltpu.make_async_copy(sbuf.at[s, pl.ds(src_row, nrows)], xs_ref.at[pl.ds(dst_row, nrows)], sem.at[s])

    lists = (cnt_ref, blr_ref, bgr_ref, slr_ref, sgr_ref)
    _start_copies(lists, functools.partial(make_copy, slot))
    n = _copy_units(lists)

    @pl.when(i > 0)
    def _():
        _wait_units(pending[0], functools.partial(make_copy, 1 - slot))

    pending[0] = n

    @pl.when(i == nt - 1)
    def _():
        _wait_units(n, functools.partial(make_copy, slot))


def _ffn_kernel(bexp_ref, nxt_ref, nused_ref, x_ref, wgu_hbm, bgu_ref, wd_hbm, bd_ref, y_ref, wgu_st, wd_st,
                wgu_bf, wd_bf, sem, *, f, layer):
    b = pl.program_id(0)
    prev = bexp_ref[jnp.maximum(b - 1, 0)]
    active = b < nused_ref[0]

    def fetch(e):
        return (pltpu.make_async_copy(wgu_hbm.at[layer, e], wgu_st, sem.at[0]),
                pltpu.make_async_copy(wd_hbm.at[layer, e], wd_st, sem.at[1]))

    @pl.when(b == 0)
    def _():
        for c in fetch(bexp_ref[0]):
            c.start()

    @pl.when(active & ((b == 0) | (bexp_ref[b] != prev)))
    def _():
        for c in fetch(bexp_ref[b]):
            c.wait()
        wgu_bf[...] = wgu_st[...].astype(BF16)
        wd_bf[...] = wd_st[...].astype(BF16)

        @pl.when(nxt_ref[b] >= 0)
        def _():
            for c in fetch(nxt_ref[b]):
                c.start()

    @pl.when(active)
    def _():
        x = x_ref[...].astype(BF16)
        gu = jnp.dot(x, wgu_bf[...], preferred_element_type=F32) + bgu_ref[0]
        gate = jnp.minimum(gu[:, :f], SWIGLU_LIMIT)
        up = jnp.clip(gu[:, f:], -SWIGLU_LIMIT, SWIGLU_LIMIT)
        glu = gate * _sigmoid(gate * SWIGLU_ALPHA)
        act = ((up + 1.0) * glu).astype(BF16)
        y_ref[...] = jnp.dot(act, wd_bf[...], preferred_element_type=F32) + bd_ref[0]


def _combine_kernel(*refs, top_k, final):
    lists, lists_nx = refs[0:5], refs[5:10]
    h_ref, gate_ref, pos_ref, ys_ref, fg_ref, o_ref, ybuf, sem = refs[10:]
    i = pl.program_id(0)
    nt = pl.num_programs(0)
    slot = i % 2
    rows = ybuf.shape[1]

    def make_copy(s, nrows, dst_row, src_row):
        return pltpu.make_async_copy(ys_ref.at[pl.ds(src_row, nrows)], ybuf.at[s, pl.ds(dst_row, nrows)], sem.at[s])

    @pl.when(i == 0)
    def _():
        ybuf[...] = jnp.zeros_like(ybuf)
        _start_copies(lists, functools.partial(make_copy, 0))

    @pl.when(i + 1 < nt)
    def _():
        _start_copies(lists_nx, functools.partial(make_copy, 1 - slot))

    _wait_units(_copy_units(lists), functools.partial(make_copy, slot))

    gates = gate_ref[...]
    pos = pos_ref[...]
    tm, d = h_ref.shape
    q = lax.broadcasted_iota(I32, (tm, rows), 1)
    wsel = jnp.zeros((tm, rows), F32)
    for k in range(top_k):
        wsel = jnp.where(q == pos[:, k:k + 1], gates[:, k:k + 1], wsel)
    out = h_ref[...] + jnp.dot(wsel.astype(BF16), ybuf[slot].astype(BF16), preferred_element_type=F32)
    if final:
        out = _rms(out, fg_ref[...])
    o_ref[...] = out


def _moe_layer(h, g, w_router, b_router, w_gate_up, b_gate_up, w_down, b_down, final_g, final, layer):
    t, d = h.shape
    n_exp = w_router.shape[1]
    f = w_down.shape[2]
    top_k = TOP_K
    bm = MOE_BLOCK
    ts = _pick(t, MOE_SORT_TILE)
    nt = t // ts
    srows = -(-(ts * top_k + n_exp * RUN_ALIGN) // LANES) * LANES
    n_blocks = -(-(t * top_k + nt * n_exp * (RUN_ALIGN - 1)) // bm) + n_exp
    n_slots = n_blocks * bm
    nbp = -(-n_blocks // SUBLANES) * SUBLANES

    tm = ts * max(1, _pick(nt, 2))
    wrt = w_router.T
    wr_hi = wrt.astype(BF16)
    wr_lo = (wrt - wr_hi.astype(F32)).astype(BF16)
    idxt, gatest, ncht = pl.pallas_call(
        functools.partial(_route_kernel, n_exp=n_exp, top_k=top_k, ts=ts),
        grid=(t // tm,),
        in_specs=[pl.BlockSpec((tm, d), lambda i: (i, 0)), _full((1, d)), _full((n_exp, d)),
                  _full((n_exp, d)), _full((n_exp, 1))],
        out_specs=[pl.BlockSpec((top_k, tm), lambda i: (0, i)), pl.BlockSpec((top_k, tm), lambda i: (0, i)),
                   pl.BlockSpec((tm // ts, n_exp, 1), lambda i: (i, 0, 0))],
        out_shape=[jax.ShapeDtypeStruct((top_k, t), I32), jax.ShapeDtypeStruct((top_k, t), F32),
                   jax.ShapeDtypeStruct((nt, n_exp, 1), I32)],
        compiler_params=_params(("arbitrary",)),
        name="moe_route",
    )(h, g.reshape(1, d), wr_hi, wr_lo, b_router.reshape(n_exp, 1))
    idx, gates, nch = idxt.T, gatest.T, ncht.reshape(nt, 1, n_exp)

    per_big = BIG_CHUNK // RUN_ALIGN
    big_rows = -(-(srows // BIG_CHUNK) // SUBLANES) * SUBLANES
    small_rows = -(-(n_exp * (per_big - 1)) // SUBLANES) * SUBLANES
    list_rows = (big_rows, big_rows, small_rows, small_rows, SUBLANES)
    rank_out = pl.pallas_call(
        functools.partial(_rank_kernel, n_exp=n_exp, top_k=top_k, bm=bm, nbp=nbp),
        grid=(nt,),
        in_specs=[pl.BlockSpec((ts, top_k), lambda i: (i, 0)), pl.BlockSpec((top_k, ts), lambda i: (0, i)),
                  _full((nt, 1, n_exp))],
        out_specs=[pl.BlockSpec((ts, top_k), lambda i: (i, 0)), pl.BlockSpec((top_k, ts), lambda i: (0, i))]
        + [pl.BlockSpec((1, r, 1), lambda i: (i, 0, 0)) for r in list_rows]
        + [_full((nbp, 1)), _full((nbp, 1)), _full((SUBLANES, n_exp))],
        out_shape=[jax.ShapeDtypeStruct((t, top_k), I32), jax.ShapeDtypeStruct((top_k, t), I32)]
        + [jax.ShapeDtypeStruct((nt, r, 1), I32) for r in list_rows]
        + [jax.ShapeDtypeStruct((nbp, 1), I32), jax.ShapeDtypeStruct((nbp, 1), I32),
           jax.ShapeDtypeStruct((SUBLANES, n_exp), I32)],
        scratch_shapes=[pltpu.VMEM((1, n_exp), F32), pltpu.VMEM((1, n_exp), F32)],
        compiler_params=_params(("arbitrary",)),
        name="moe_rank",
    )(idx, idxt, nch)
    pos, post = rank_out[0], rank_out[1]
    blr, bgr, slr, sgr, cnt = rank_out[2:7]
    bexp, nxt, meta = rank_out[7:]
    lists = [a.reshape(nt, 1, a.shape[1]) for a in (cnt, blr, bgr, slr, sgr)]
    bexp = bexp.reshape(nbp)
    nxt = nxt.reshape(nbp)
    n_used = meta[0, 0:1]
    last_blk = meta[1]

    def smem_lists(step=0):
        return [pl.BlockSpec((1, 1, a.shape[2]), lambda i, *_: (jnp.minimum(i + step, nt - 1), 0, 0),
                             memory_space=pltpu.SMEM) for a in lists]

    xs = pl.pallas_call(
        functools.partial(_dispatch_kernel, n_exp=n_exp, top_k=top_k, bm=bm),
        grid_spec=pltpu.PrefetchScalarGridSpec(
            num_scalar_prefetch=1,
            grid=(nt,),
            in_specs=smem_lists() + [
                pl.BlockSpec((ts, d), lambda i, last: (i, 0)),
                pl.BlockSpec((1, d), lambda i, last: (0, 0)),
                pl.BlockSpec((top_k, ts), lambda i, last: (0, i)),
            ],
            out_specs=pl.BlockSpec(memory_space=pl.ANY),
            scratch_shapes=[pltpu.VMEM((2, srows, d), F32), pltpu.VMEM((bm, d), F32), pltpu.SMEM((1,), I32),
                            pltpu.SemaphoreType.DMA((2,)), pltpu.SemaphoreType.DMA(())],
        ),
        out_shape=jax.ShapeDtypeStruct((n_slots, d), F32),
        compiler_params=_params(("arbitrary",)),
        name="moe_dispatch",
    )(last_blk, *lists, h, g.reshape(1, d), post)

    def blk(b, nused_ref):
        return jnp.minimum(b, nused_ref[0] - 1)

    ys = pl.pallas_call(
        functools.partial(_ffn_kernel, f=f, layer=layer),
        grid_spec=pltpu.PrefetchScalarGridSpec(
            num_scalar_prefetch=3,
            grid=(n_blocks,),
            in_specs=[
                pl.BlockSpec((bm, d), lambda b, be, nx, nu: (blk(b, nu), 0)),
                pl.BlockSpec(memory_space=pl.ANY),
                pl.BlockSpec((1, 1, 2 * f), lambda b, be, nx, nu: (be[blk(b, nu)], 0, 0)),
                pl.BlockSpec(memory_space=pl.ANY),
                pl.BlockSpec((1, 1, d), lambda b, be, nx, nu: (be[blk(b, nu)], 0, 0)),
            ],
            out_specs=pl.BlockSpec((bm, d), lambda b, be, nx, nu: (blk(b, nu), 0)),
            scratch_shapes=[pltpu.VMEM((d, 2 * f), F32), pltpu.VMEM((f, d), F32),
                            pltpu.VMEM((d, 2 * f), BF16), pltpu.VMEM((f, d), BF16), pltpu.SemaphoreType.DMA((2,))],
        ),
        out_shape=jax.ShapeDtypeStruct((n_slots, d), F32),
        compiler_params=_params(("arbitrary",)),
        name="moe_ffn",
    )(bexp, nxt, n_used, xs, w_gate_up, b_gate_up.reshape(n_exp, 1, 2 * f), w_down, b_down.reshape(n_exp, 1, d))

    return pl.pallas_call(
        functools.partial(_combine_kernel, top_k=top_k, final=final),
        grid=(nt,),
        in_specs=smem_lists() + smem_lists(step=1) + [
            pl.BlockSpec((ts, d), lambda i: (i, 0)),
            pl.BlockSpec((ts, top_k), lambda i: (i, 0)),
            pl.BlockSpec((ts, top_k), lambda i: (i, 0)),
            pl.BlockSpec(memory_space=pl.ANY),
            _full((1, d)),
        ],
        out_specs=pl.BlockSpec((ts, d), lambda i: (i, 0)),
        out_shape=jax.ShapeDtypeStruct((t, d), F32),
        scratch_shapes=[pltpu.VMEM((2, srows, d), F32), pltpu.SemaphoreType.DMA((2,))],
        compiler_params=_params(("arbitrary",)),
        name="moe_combine",
    )(*lists, *lists, h, gates, pos, ys, final_g.reshape(1, d))


def kernel(x, mix_norm, ffn_norm, final_norm, lru_w_in, lru_conv_w, lru_conv_b, lru_w_a, lru_b_a, lru_w_x, lru_b_x, lru_a_param, lru_w_out, pool_w_in, pool_w_group, pool_scale, pool_w_out, sb_w_qkv, sb_w_out, moe_w_router, moe_b_router, moe_w_gate_up, moe_b_gate_up, moe_w_down, moe_b_down):
    batch, seq, d = x.shape
    depth = mix_norm.shape[0]
    h = x.reshape(batch * seq, d)
    for layer in range(depth):
        kind = layer % N_MIXERS
        slot = layer // N_MIXERS
        if kind == 0:
            h = _lru_layer(h, batch, seq, mix_norm[layer], lru_w_in[slot], lru_conv_w[slot], lru_conv_b[slot],
                           lru_w_a[slot], lru_b_a[slot], lru_w_x[slot], lru_b_x[slot], lru_a_param[slot],
                           lru_w_out[slot])
        elif kind == 1:
            h = _pool_layer(h, batch, seq, mix_norm[layer], pool_w_in[slot], pool_w_group[slot], pool_scale[slot],
                            pool_w_out[slot])
        else:
            h = _sb_layer(h, batch, seq, mix_norm[layer], sb_w_qkv[slot], sb_w_out[slot])
        h = _moe_layer(h, ffn_norm[layer], moe_w_router[layer], moe_b_router[layer], moe_w_gate_up,
                       moe_b_gate_up[layer], moe_w_down, moe_b_down[layer], final_norm,
                       final=(layer == depth - 1), layer=layer)
    return h.reshape(batch, seq, d)
```

```python
import functools
import math

import jax
import jax.numpy as jnp
from jax import lax
from jax.experimental import pallas as pl
from jax.experimental.pallas import tpu as pltpu

F32 = jnp.float32
BF16 = jnp.bfloat16
I32 = jnp.int32
U32 = jnp.uint32

RMS_EPS = 1e-6
LRU_C = 8.0
POOL_WINDOWS = (2, 4, 8, 16)
SB_HEADS = 16
SB_HEADS_PER_STEP = 4
LOG2E = 1.4426950408889634
TOP_K = 4
SWIGLU_LIMIT = 7.0
SWIGLU_ALPHA = 1.702
N_MIXERS = 3

V7X_VMEM_BYTES = 64 * 1024 * 1024
VMEM_LIMIT_BYTES = V7X_VMEM_BYTES - 8 * 1024 * 1024
SUBLANES = 8
LANES = 128

MOE_BLOCK = 512
MOE_SORT_TILE = 256
RUN_ALIGN = SUBLANES
BIG_CHUNK = 4 * RUN_ALIGN
WAIT_CHUNK = 16 * RUN_ALIGN


def _params(semantics, vmem=VMEM_LIMIT_BYTES):
    return pltpu.CompilerParams(dimension_semantics=semantics, vmem_limit_bytes=vmem)


def _pick(n, pref):
    t = min(n, pref)
    while n % t:
        t //= 2
    return t


def _rms(h, g):
    ms = jnp.mean(h * h, axis=-1, keepdims=True)
    return h * lax.rsqrt(ms + RMS_EPS) * g


def _sigmoid(x):
    return 1.0 / (1.0 + jnp.exp(-x))


def _softplus(x):
    return jnp.maximum(x, 0.0) + jnp.log(1.0 + jnp.exp(-jnp.abs(x)))


def _gelu_tanh(x):
    c = math.sqrt(2.0 / math.pi)
    return 0.5 * x * (1.0 + jnp.tanh(c * (x + 0.044715 * (x * x * x))))


def _full(shape):
    n = len(shape)
    return pl.BlockSpec(shape, lambda *_: (0,) * n)


def _pack_pairs(x):
    n = x.shape[1] // 2
    lo = lax.bitcast_convert_type(x[:, :n], U32)
    hi = lax.bitcast_convert_type(x[:, n:], U32)
    return lax.shift_right_logical(lo, jnp.uint32(16)) | (hi & jnp.uint32(0xFFFF0000))


def _unpack_pairs(w):
    lo = lax.bitcast_convert_type(lax.shift_left(w, jnp.uint32(16)), F32)
    hi = lax.bitcast_convert_type(w & jnp.uint32(0xFFFF0000), F32)
    return jnp.concatenate([lo.astype(BF16), hi.astype(BF16)], axis=1)


def _lru_kernel(h_ref, g_ref, win_ref, cw_ref, cb_ref, wax_ref, ba_ref, bx_ref, ap_ref, wout_ref,
                o_ref, hcar_ref, ext_ref, y_ref, *, ts, w, bw):
    s = pl.program_id(1)

    @pl.when(s == 0)
    def _():
        hcar_ref[...] = jnp.zeros_like(hcar_ref)
        ext_ref[0:SUBLANES, :] = jnp.zeros((SUBLANES, w), F32)

    h = h_ref[...]
    u = _rms(h, g_ref[...]).astype(BF16)
    proj = jnp.dot(u, win_ref[...], preferred_element_type=F32)
    row8 = lax.broadcasted_iota(I32, (SUBLANES, bw), 0)
    kw = cw_ref.shape[0]

    for c in range(w // bw):
        lo, hi = c * bw, (c + 1) * bw
        gate = proj[:, lo:hi]
        xb = proj[:, w + lo:w + hi]
        ext_ref[SUBLANES:SUBLANES + ts, lo:hi] = xb
        xc = cb_ref[:, lo:hi] + xb * cw_ref[kw - 1:kw, lo:hi]
        for j in range(1, kw):
            xc = xc + ext_ref[SUBLANES - j:SUBLANES - j + ts, lo:hi] * cw_ref[kw - 1 - j:kw - j, lo:hi]
        ext_ref[0:SUBLANES, lo:hi] = xb[ts - SUBLANES:ts]

        res = jnp.dot(xc.astype(BF16), wax_ref[c], preferred_element_type=F32)
        r = _sigmoid(res[:, :bw] + ba_ref[:, lo:hi])
        i = _sigmoid(res[:, bw:] + bx_ref[:, lo:hi])
        log_a = (-LRU_C * _softplus(-ap_ref[:, lo:hi])) * r
        a = jnp.exp(log_a)
        b = jnp.sqrt(-jnp.tanh(log_a) * (a * a + 1.0)) * (i * xc)
        hprev = hcar_ref[:, lo:hi]
        groups = []
        for gi in range(ts // SUBLANES):
            ag = a[gi * SUBLANES:(gi + 1) * SUBLANES]
            bg = b[gi * SUBLANES:(gi + 1) * SUBLANES]
            d = 1
            while d < SUBLANES:
                keep = row8 >= d
                a_sh = jnp.where(keep, pltpu.roll(ag, d, axis=0), 1.0)
                b_sh = jnp.where(keep, pltpu.roll(bg, d, axis=0), 0.0)
                bg = ag * b_sh + bg
                ag = ag * a_sh
                d *= 2
            hg = ag * hprev + bg
            hprev = hg[SUBLANES - 1:SUBLANES]
            groups.append(hg)
        hs = jnp.concatenate(groups, axis=0)
        hcar_ref[:, lo:hi] = hprev
        y_ref[:, lo:hi] = (hs * _gelu_tanh(gate)).astype(BF16)

    o_ref[...] = h + jnp.dot(y_ref[...], wout_ref[...], preferred_element_type=F32)


def _lru_layer(h, batch, seq, g, w_in, conv_w, conv_b, w_a, b_a, w_x, b_x, a_param, w_out):
    t, d = h.shape
    w = w_in.shape[1] // 2
    nblk, bw, _ = w_a.shape
    ts = _pick(seq, 256)
    ns = seq // ts
    wax = jnp.concatenate([w_a, w_x], axis=-1).astype(BF16)
    row = lambda v: v.reshape(1, -1)
    kern = functools.partial(_lru_kernel, ts=ts, w=w, bw=bw)
    return pl.pallas_call(
        kern,
        grid=(batch, ns),
        in_specs=[
            pl.BlockSpec((ts, d), lambda b, s: (b * ns + s, 0)),
            _full((1, d)), _full((d, 2 * w)), _full(conv_w.shape), _full((1, w)),
            _full(wax.shape), _full((1, w)), _full((1, w)), _full((1, w)), _full((w, d)),
        ],
        out_specs=pl.BlockSpec((ts, d), lambda b, s: (b * ns + s, 0)),
        out_shape=jax.ShapeDtypeStruct((t, d), F32),
        scratch_shapes=[pltpu.VMEM((1, w), F32), pltpu.VMEM((SUBLANES + ts, w), F32), pltpu.VMEM((ts, w), BF16)],
        compiler_params=_params(("arbitrary", "arbitrary")),
        name="lru_mixer",
    )(h, row(g), w_in.astype(BF16), conv_w, row(conv_b), wax, row(b_a), row(b_x), row(a_param),
      w_out.astype(BF16))


def _pool_kernel(h_ref, g_ref, win_ref, wg_ref, sc_ref, wout_ref, o_ref, ext_ref, m_ref, *, ts, d, wins):
    s = pl.program_id(1)
    pad = wins[-1]
    gw = d // len(wins)

    @pl.when(s == 0)
    def _():
        ext_ref[0:pad, :] = jnp.zeros((pad, d), F32)

    h = h_ref[...]
    u = _rms(h, g_ref[...]).astype(BF16)
    v = jnp.dot(u, win_ref[...], preferred_element_type=F32)
    ext_ref[pad:pad + ts, :] = v
    n_avail = (s * ts + lax.broadcasted_iota(I32, (ts, 1), 0) + 1).astype(F32)

    cur = ext_ref[...]
    width = 1
    for gi, win in enumerate(wins):
        while width < win:
            cur = cur + pltpu.roll(cur, width, axis=0)
            width *= 2
        lo, hi = gi * gw, (gi + 1) * gw
        mean = cur[pad:, 0:gw] / jnp.minimum(n_avail, float(win))
        pooled = mean - v[:, lo:hi]
        mixed = jnp.dot(pooled.astype(BF16), wg_ref[gi], preferred_element_type=F32)
        m_ref[:, lo:hi] = (mixed * sc_ref[:, lo:hi]).astype(BF16)
        cur = cur[:, gw:]
    ext_ref[0:pad, :] = v[ts - pad:ts]
    o_ref[...] = h + jnp.dot(m_ref[...], wout_ref[...], preferred_element_type=F32)


def _pool_layer(h, batch, seq, g, w_in, w_group, scale, w_out):
    t, d = h.shape
    ts = _pick(seq, 256)
    ns = seq // ts
    pad = POOL_WINDOWS[-1]
    kern = functools.partial(_pool_kernel, ts=ts, d=d, wins=POOL_WINDOWS)
    return pl.pallas_call(
        kern,
        grid=(batch, ns),
        in_specs=[
            pl.BlockSpec((ts, d), lambda b, s: (b * ns + s, 0)),
            _full((1, d)), _full((d, d)), _full(w_group.shape), _full((1, d)), _full((d, d)),
        ],
        out_specs=pl.BlockSpec((ts, d), lambda b, s: (b * ns + s, 0)),
        out_shape=jax.ShapeDtypeStruct((t, d), F32),
        scratch_shapes=[pltpu.VMEM((ts + pad, d), F32), pltpu.VMEM((ts, d), BF16)],
        compiler_params=_params(("arbitrary", "arbitrary")),
        name="pool_mixer",
    )(h, g.reshape(1, d), w_in.astype(BF16), w_group.astype(BF16), scale.reshape(1, d), w_out.astype(BF16))


def _sb_proj_kernel(h_ref, g_ref, wqt_ref, wk_ref, wvt_ref, qt_ref, k_ref, vt_ref):
    u = _rms(h_ref[...], g_ref[...]).astype(BF16)
    nt = (((1,), (1,)), ((), ()))
    qt_ref[...] = lax.dot_general(wqt_ref[...], u, nt, preferred_element_type=F32).astype(BF16)
    k_ref[...] = jnp.dot(u, wk_ref[...], preferred_element_type=F32).astype(BF16)
    vt_ref[...] = lax.dot_general(wvt_ref[...], u, nt, preferred_element_type=F32).astype(BF16)


def _sb_attn_kernel(qt_ref, k_ref, vt_ref, o_ref, lb_ref, zs_ref, a_ref, *, bq, dh, nh):
    qb = pl.program_id(2)
    krow = lax.broadcasted_iota(I32, (bq, bq), 0)
    qcol = lax.broadcasted_iota(I32, (bq, bq), 1)
    upper = (qcol >= krow).astype(BF16)
    causal = krow < qcol

    def step(chains, masked, st):
        for c, (hh, j) in enumerate(chains):
            qh = qt_ref[hh * dh:(hh + 1) * dh, :]
            kh = k_ref[pl.ds(j * bq, bq), hh * dh:(hh + 1) * dh]
            z = jnp.dot(kh, qh, preferred_element_type=F32)
            t = jnp.exp2(jnp.abs(z) * (-LOG2E))
            l1m = jnp.minimum(-z, 0.0) - jnp.log(1.0 + t)
            zs_ref[c] = z
            lm = jnp.where(causal, l1m, 0.0) if masked else l1m
            lb_ref[c] = lm.astype(BF16)
        st = list(st)
        for c, (hh, j) in enumerate(chains):
            acc, carry = st[hh]
            suffix = jnp.dot(upper, lb_ref[c], preferred_element_type=F32)
            a = jnp.exp(zs_ref[c] + suffix + carry)
            if masked:
                a = jnp.where(causal, a, 0.0)
            a_ref[c] = a.astype(BF16)
            st[hh] = (acc, carry + suffix[0:1, :])
        for c, (hh, j) in enumerate(chains):
            acc, carry = st[hh]
            vth = vt_ref[hh * dh:(hh + 1) * dh, pl.ds(j * bq, bq)]
            st[hh] = (acc + jnp.dot(vth, a_ref[c], preferred_element_type=F32), carry)
        return st

    heads = range(nh)
    zero = (jnp.zeros((dh, bq), F32), jnp.zeros((1, bq), F32))
    st = step([(hh, qb) for hh in heads], True, [zero] * nh)

    def pair_body(i, st):
        j = qb - 1 - 2 * i
        return step([(hh, j) for hh in heads] + [(hh, j - 1) for hh in heads], False, st)

    def single_body(i, st):
        return step([(hh, 0) for hh in heads], False, st)

    st = lax.fori_loop(0, lax.shift_right_logical(qb, 1), pair_body, st)
    st = lax.fori_loop(0, qb & 1, single_body, st)
    o_ref[...] = jnp.concatenate([st[hh][0].T for hh in heads], axis=1).astype(BF16)


def _sb_out_kernel(h_ref, o_ref, w_ref, out_ref):
    out_ref[...] = h_ref[...] + jnp.dot(o_ref[...], w_ref[...], preferred_element_type=F32)


def _sb_layer(h, batch, seq, g, w_qkv, w_out):
    t, d = h.shape
    dh = d // SB_HEADS
    scale = 1.0 / math.sqrt(dh)
    wqt = (w_qkv[:, :d] * scale).T.astype(BF16)
    wk = w_qkv[:, d:2 * d].astype(BF16)
    wvt = w_qkv[:, 2 * d:].T.astype(BF16)
    tp = _pick(t, 512)
    qt, k, vt = pl.pallas_call(
        _sb_proj_kernel,
        grid=(t // tp,),
        in_specs=[pl.BlockSpec((tp, d), lambda i: (i, 0)), _full((1, d)), _full((d, d)), _full((d, d)),
                  _full((d, d))],
        out_specs=[pl.BlockSpec((d, tp), lambda i: (0, i)), pl.BlockSpec((tp, d), lambda i: (i, 0)),
                   pl.BlockSpec((d, tp), lambda i: (0, i))],
        out_shape=[jax.ShapeDtypeStruct((d, t), BF16), jax.ShapeDtypeStruct((t, d), BF16),
                   jax.ShapeDtypeStruct((d, t), BF16)],
        compiler_params=_params(("arbitrary",)),
        name="sb_qkv_proj",
    )(h, g.reshape(1, d), wqt, wk, wvt)

    bq = _pick(seq, 256)
    nq = seq // bq
    nh = SB_HEADS_PER_STEP
    grp = nh * dh
    nchain = 2 * nh
    o = pl.pallas_call(
        functools.partial(_sb_attn_kernel, bq=bq, dh=dh, nh=nh),
        grid=(batch, d // grp, nq),
        in_specs=[
            pl.BlockSpec((grp, bq), lambda b, p, q: (p, b * nq + q)),
            pl.BlockSpec((seq, grp), lambda b, p, q: (b, p)),
            pl.BlockSpec((grp, seq), lambda b, p, q: (p, b)),
        ],
        out_specs=pl.BlockSpec((bq, grp), lambda b, p, q: (b * nq + q, p)),
        out_shape=jax.ShapeDtypeStruct((t, d), BF16),
        scratch_shapes=[pltpu.VMEM((nchain, bq, bq), BF16), pltpu.VMEM((nchain, bq, bq), F32),
                        pltpu.VMEM((nchain, bq, bq), BF16)],
        compiler_params=_params(("arbitrary", "arbitrary", "arbitrary")),
        name="sb_attention",
    )(qt, k, vt)

    return pl.pallas_call(
        _sb_out_kernel,
        grid=(t // tp,),
        in_specs=[pl.BlockSpec((tp, d), lambda i: (i, 0)), pl.BlockSpec((tp, d), lambda i: (i, 0)),
                  _full((d, d))],
        out_specs=pl.BlockSpec((tp, d), lambda i: (i, 0)),
        out_shape=jax.ShapeDtypeStruct((t, d), F32),
        compiler_params=_params(("arbitrary",)),
        name="sb_out_proj",
    )(h, o, w_out.astype(BF16))


def _route_kernel(h_ref, g_ref, wr_hi_ref, wr_lo_ref, br_ref, idxt_ref, gatet_ref, nch_ref, *, n_exp, top_k, ts):
    u = _rms(h_ref[...], g_ref[...])
    u_hi = u.astype(BF16)
    u_lo = (u - u_hi.astype(F32)).astype(BF16)
    nt_dims = (((1,), (1,)), ((), ()))
    logits = (lax.dot_general(wr_hi_ref[...], u_hi, nt_dims, preferred_element_type=F32)
              + lax.dot_general(wr_lo_ref[...], u_hi, nt_dims, preferred_element_type=F32)
              + lax.dot_general(wr_hi_ref[...], u_lo, nt_dims, preferred_element_type=F32)) + br_ref[...]
    tm = logits.shape[1]
    sub = lax.broadcasted_iota(I32, (n_exp, tm), 0)
    krow = lax.broadcasted_iota(I32, (top_k, tm), 0)
    vals = logits
    top_v, top_i = [], []
    for _ in range(top_k):
        m = jnp.max(vals, axis=0, keepdims=True)
        sel = jnp.min(jnp.where(vals == m, sub, n_exp), axis=0, keepdims=True)
        top_v.append(m)
        top_i.append(sel)
        vals = jnp.where(sub == sel, -jnp.inf, vals)
    exps = [jnp.exp(v - top_v[0]) for v in top_v]
    denom = exps[0]
    for e in exps[1:]:
        denom = denom + e
    idx = jnp.zeros((top_k, tm), I32)
    gates = jnp.zeros((top_k, tm), F32)
    for k in range(top_k):
        idx = jnp.where(krow == k, top_i[k], idx)
        gates = jnp.where(krow == k, exps[k] / denom, gates)
    idxt_ref[...] = idx
    gatet_ref[...] = gates
    chosen = jnp.where(sub == top_i[0], 1.0, 0.0)
    for k in range(1, top_k):
        chosen = jnp.where(sub == top_i[k], 1.0, chosen)
    for s in range(tm // ts):
        cnt = jnp.sum(chosen[:, s * ts:(s + 1) * ts], axis=1, keepdims=True)
        nch_ref[s] = jnp.floor((cnt + (RUN_ALIGN - 1)) * (1.0 / RUN_ALIGN)).astype(I32)


def _rank_kernel(idx_ref, idxt_ref, nchall_ref, pos_ref, post_ref, blr_ref, bgr_ref, slr_ref, sgr_ref, cnt_ref,
                 bexp_ref, nxt_ref, meta_ref, run_ref, pstart_ref, *, n_exp, top_k, bm, nbp):
    i = pl.program_id(0)
    tm = idx_ref.shape[0]
    cpb = bm // RUN_ALIGN
    idx = idx_ref[...]
    lane = lax.broadcasted_iota(I32, (tm, n_exp), 1)
    ohs = [(idx[:, k:k + 1] == lane).astype(F32) for k in range(top_k)]
    oh = ohs[0]
    for o in ohs[1:]:
        oh = oh + o
    cnt = jnp.sum(oh, axis=0, keepdims=True)
    nch = jnp.floor((cnt + (RUN_ALIGN - 1)) * (1.0 / RUN_ALIGN))
    er = lax.broadcasted_iota(I32, (n_exp, n_exp), 0)
    ec = lax.broadcasted_iota(I32, (n_exp, n_exp), 1)

    def excl_cumsum_row(v):
        v8 = jnp.broadcast_to(v, (SUBLANES, n_exp)).astype(BF16)
        return jnp.dot(v8, (er < ec).astype(BF16), preferred_element_type=F32)[0:1, :]

    @pl.when(i == 0)
    def _():
        tot = jnp.sum(nchall_ref[...].astype(F32), axis=0)
        nb = jnp.floor((tot + (cpb - 1)) * (1.0 / cpb))
        bstart = excl_cumsum_row(nb)
        bend = bstart + nb
        pstart_ref[...] = bstart * cpb
        blk = lax.broadcasted_iota(I32, (nbp, n_exp), 0).astype(F32)
        be = jnp.sum((bend <= blk).astype(F32), axis=-1, keepdims=True)
        be = jnp.minimum(be, n_exp - 1)
        bexp_ref[...] = be.astype(I32)
        n_used = jnp.sum(nb, axis=-1, keepdims=True)
        lane_e = lax.broadcasted_iota(I32, (nbp, n_exp), 1).astype(F32)
        run_end = jnp.sum(jnp.where(lane_e == be, bend, 0.0), axis=-1, keepdims=True)
        nx = jnp.minimum(jnp.sum((bend <= run_end).astype(F32), axis=-1, keepdims=True), n_exp - 1)
        nxt_ref[...] = jnp.where(run_end < n_used, nx, -1.0).astype(I32)
        last = jnp.where(nb > 0, bend - 1.0, -1.0)
        mrow = lax.broadcasted_iota(I32, (SUBLANES, n_exp), 0)
        meta = jnp.where(mrow == 0, jnp.broadcast_to(n_used, (SUBLANES, n_exp)),
                         jnp.where(mrow == 1, jnp.broadcast_to(last, (SUBLANES, n_exp)), 0.0))
        meta_ref[...] = meta.astype(I32)
        run_ref[...] = jnp.zeros_like(run_ref)

    tr = lax.broadcasted_iota(I32, (tm, tm), 0)
    tc = lax.broadcasted_iota(I32, (tm, tm), 1)
    off = excl_cumsum_row(nch)
    excl = jnp.dot((tc < tr).astype(BF16), oh.astype(BF16), preferred_element_type=F32)
    base = excl + off * RUN_ALIGN
    kcol = lax.broadcasted_iota(I32, (tm, top_k), 1)
    pos = jnp.zeros((tm, top_k), I32)
    for k in range(top_k):
        pk = jnp.sum(ohs[k] * base, axis=-1, keepdims=True).astype(I32)
        pos = jnp.where(kcol == k, pk, pos)
    pos_ref[...] = pos
    idxt = idxt_ref[...]
    sub = lax.broadcasted_iota(I32, (n_exp, tm), 0)
    ohts = [(idxt[k:k + 1, :] == sub).astype(F32) for k in range(top_k)]
    oht = ohts[0]
    for o in ohts[1:]:
        oht = oht + o
    cnt_col = jnp.sum(oht, axis=1, keepdims=True)
    nch_col = jnp.floor((cnt_col + (RUN_ALIGN - 1)) * (1.0 / RUN_ALIGN))
    off_col = jnp.dot((ec < er).astype(BF16), jnp.broadcast_to(nch_col, (n_exp, LANES)).astype(BF16),
                      preferred_element_type=F32)[:, 0:1]
    exclt = jnp.dot(oht.astype(BF16), (tr < tc).astype(BF16), preferred_element_type=F32)
    baset = exclt + off_col * RUN_ALIGN
    krow = lax.broadcasted_iota(I32, (top_k, tm), 0)
    post = jnp.zeros((top_k, tm), I32)
    for k in range(top_k):
        pk = jnp.sum(ohts[k] * baset, axis=0, keepdims=True).astype(I32)
        post = jnp.where(krow == k, pk, post)
    post_ref[...] = post

    gst_row = pstart_ref[...] + run_ref[...]
    gst_col = jnp.sum(jnp.where(er == ec, jnp.broadcast_to(gst_row, (n_exp, n_exp)), 0.0), axis=1, keepdims=True)
    per_big = BIG_CHUNK // RUN_ALIGN
    nbig = jnp.floor(nch_col * (1.0 / per_big))
    nsm = nch_col - nbig * per_big

    def excl_cumsum_col(v):
        return jnp.dot((ec < er).astype(BF16), jnp.broadcast_to(v, (n_exp, LANES)).astype(BF16),
                       preferred_element_type=F32)[:, 0:1]

    def copy_list(count, loc0, glob0, stride, n):
        before = excl_cumsum_col(count)
        j = lax.broadcasted_iota(I32, (n_exp, n), 1).astype(F32)
        owner = jnp.sum(((before + count) <= j).astype(F32), axis=0, keepdims=True)
        mine = lax.broadcasted_iota(I32, (n_exp, n), 0).astype(F32) == owner
        pick = lambda v: jnp.sum(jnp.where(mine, v, 0.0), axis=0, keepdims=True)
        piece = (j[0:1, :] - pick(before)) * stride
        return (pick(loc0) + piece).astype(I32), (pick(glob0) + piece).astype(I32)

    blr_ref[0], bgr_ref[0] = copy_list(nbig, off_col, gst_col, per_big, blr_ref.shape[2])
    slr_ref[0], sgr_ref[0] = copy_list(nsm, off_col + nbig * per_big, gst_col + nbig * per_big, 1, slr_ref.shape[2])
    clane = lax.broadcasted_iota(I32, (1, cnt_ref.shape[2]), 1)
    cnt_ref[0] = jnp.where(clane == 0, jnp.sum(nbig, axis=0, keepdims=True),
                           jnp.where(clane == 1, jnp.sum(nsm, axis=0, keepdims=True), 0.0)).astype(I32)
    run_ref[...] = run_ref[...] + nch


def _copy_units(lists):
    cnt_ref = lists[0]
    return cnt_ref[0, 0, 0] * (BIG_CHUNK // RUN_ALIGN) + cnt_ref[0, 0, 1]


def _start_copies(lists, make_copy):
    cnt_ref, blr_ref, bgr_ref, slr_ref, sgr_ref = lists

    def start(nrows, loc_ref, glob_ref):
        def body(j, c):
            make_copy(nrows, pl.multiple_of(loc_ref[0, 0, j] * RUN_ALIGN, RUN_ALIGN),
                      pl.multiple_of(glob_ref[0, 0, j] * RUN_ALIGN, RUN_ALIGN)).start()
            return c
        return body

    lax.fori_loop(0, cnt_ref[0, 0, 0], start(BIG_CHUNK, blr_ref, bgr_ref), 0)
    lax.fori_loop(0, cnt_ref[0, 0, 1], start(RUN_ALIGN, slr_ref, sgr_ref), 0)


def _wait_units(n, make_copy):
    per_wait = WAIT_CHUNK // RUN_ALIGN

    def wait_big(j, c):
        make_copy(WAIT_CHUNK, 0, 0).wait()
        return c

    def wait_small(j, c):
        make_copy(RUN_ALIGN, 0, 0).wait()
        return c

    nbig = lax.shift_right_logical(n, per_wait.bit_length() - 1)
    lax.fori_loop(0, nbig, wait_big, 0)
    lax.fori_loop(nbig * per_wait, n, wait_small, 0)


def _dispatch_kernel(last_ref, cnt_ref, blr_ref, bgr_ref, slr_ref, sgr_ref, h_ref, g_ref, post_ref, xs_ref, sbuf,
                     zbuf, pending, sem, zsem, *, n_exp, top_k, bm):
    i = pl.program_id(0)
    nt = pl.num_programs(0)
    slot = i % 2
    rows = sbuf.shape[1]

    @pl.when(i == 0)
    def _():
        zbuf[...] = jnp.zeros_like(zbuf)

        def zero_copy(e):
            return pltpu.make_async_copy(zbuf, xs_ref.at[pl.ds(last_ref[e] * bm, bm)], zsem)

        def start(e, c):
            @pl.when(last_ref[e] >= 0)
            def _():
                zero_copy(e).start()
            return c

        def wait(e, c):
            @pl.when(last_ref[e] >= 0)
            def _():
                zero_copy(e).wait()
            return c

        lax.fori_loop(0, n_exp, start, 0)
        lax.fori_loop(0, n_exp, wait, 0)

    u = _rms(h_ref[...], g_ref[...]).astype(BF16)
    tm = u.shape[0]
    post = post_ref[...]
    q = lax.broadcasted_iota(I32, (rows, tm), 0)
    perm = jnp.zeros((rows, tm), F32)
    for k in range(top_k):
        perm = jnp.where(q == post[k:k + 1, :], 1.0, perm)
    sbuf[slot] = _pack_pairs(jnp.dot(perm.astype(BF16), u, preferred_element_type=F32))

    def make_copy(s, nrows, src_row, dst_row):
        return pltpu.make_async_copy(sbuf.at[s, pl.ds(src_row, nrows)], xs_ref.at[pl.ds(dst_row, nrows)], sem.at[s])

    lists = (cnt_ref, blr_ref, bgr_ref, slr_ref, sgr_ref)
    _start_copies(lists, functools.partial(make_copy, slot))
    n = _copy_units(lists)

    @pl.when(i > 0)
    def _():
        _wait_units(pending[0], functools.partial(make_copy, 1 - slot))

    pending[0] = n

    @pl.when(i == nt - 1)
    def _():
        _wait_units(n, functools.partial(make_copy, slot))


def _ffn_kernel(bexp_ref, nxt_ref, nused_ref, x_ref, wgu_hbm, bgu_ref, wd_hbm, bd_ref, y_ref, wgu_st, wd_st,
                wgu_bf, wd_bf, sem, *, f, layer):
    b = pl.program_id(0)
    prev = bexp_ref[jnp.maximum(b - 1, 0)]
    active = b < nused_ref[0]

    def fetch(e):
        return (pltpu.make_async_copy(wgu_hbm.at[layer, e], wgu_st, sem.at[0]),
                pltpu.make_async_copy(wd_hbm.at[layer, e], wd_st, sem.at[1]))

    @pl.when(b == 0)
    def _():
        for c in fetch(bexp_ref[0]):
            c.start()

    @pl.when(active & ((b == 0) | (bexp_ref[b] != prev)))
    def _():
        for c in fetch(bexp_ref[b]):
            c.wait()
        wgu_bf[...] = wgu_st[...].astype(BF16)
        wd_bf[...] = wd_st[...].astype(BF16)

        @pl.when(nxt_ref[b] >= 0)
        def _():
            for c in fetch(nxt_ref[b]):
                c.start()

    @pl.when(active)
    def _():
        x = _unpack_pairs(x_ref[...])
        gu = jnp.dot(x, wgu_bf[...], preferred_element_type=F32) + bgu_ref[0]
        gate = jnp.minimum(gu[:, :f], SWIGLU_LIMIT)
        up = jnp.clip(gu[:, f:], -SWIGLU_LIMIT, SWIGLU_LIMIT)
        glu = gate * _sigmoid(gate * SWIGLU_ALPHA)
        act = ((up + 1.0) * glu).astype(BF16)
        y = jnp.dot(act, wd_bf[...], preferred_element_type=F32) + bd_ref[0]
        y_ref[...] = _pack_pairs(y.astype(BF16).astype(F32))


def _combine_kernel(*refs, top_k, final):
    lists, lists_nx = refs[0:5], refs[5:10]
    h_ref, gate_ref, pos_ref, ys_ref, fg_ref, o_ref, ybuf, sem = refs[10:]
    i = pl.program_id(0)
    nt = pl.num_programs(0)
    slot = i % 2
    rows = ybuf.shape[1]

    def make_copy(s, nrows, dst_row, src_row):
        return pltpu.make_async_copy(ys_ref.at[pl.ds(src_row, nrows)], ybuf.at[s, pl.ds(dst_row, nrows)], sem.at[s])

    @pl.when(i == 0)
    def _():
        ybuf[...] = jnp.zeros_like(ybuf)
        _start_copies(lists, functools.partial(make_copy, 0))

    @pl.when(i + 1 < nt)
    def _():
        _start_copies(lists_nx, functools.partial(make_copy, 1 - slot))

    _wait_units(_copy_units(lists), functools.partial(make_copy, slot))

    gates = gate_ref[...]
    pos = pos_ref[...]
    tm, d = h_ref.shape
    q = lax.broadcasted_iota(I32, (tm, rows), 1)
    wsel = jnp.zeros((tm, rows), F32)
    for k in range(top_k):
        wsel = jnp.where(q == pos[:, k:k + 1], gates[:, k:k + 1], wsel)
    out = h_ref[...] + jnp.dot(wsel.astype(BF16), _unpack_pairs(ybuf[slot]), preferred_element_type=F32)
    if final:
        out = _rms(out, fg_ref[...])
    o_ref[...] = out


def _moe_layer(h, g, w_router, b_router, w_gate_up, b_gate_up, w_down, b_down, final_g, final, layer):
    t, d = h.shape
    n_exp = w_router.shape[1]
    f = w_down.shape[2]
    top_k = TOP_K
    dp = d // 2
    bm = MOE_BLOCK
    ts = _pick(t, MOE_SORT_TILE)
    nt = t // ts
    srows = -(-(ts * top_k + n_exp * RUN_ALIGN) // LANES) * LANES
    n_blocks = -(-(t * top_k + nt * n_exp * (RUN_ALIGN - 1)) // bm) + n_exp
    n_slots = n_blocks * bm
    nbp = -(-n_blocks // SUBLANES) * SUBLANES

    tm = ts * max(1, _pick(nt, 2))
    wrt = w_router.T
    wr_hi = wrt.astype(BF16)
    wr_lo = (wrt - wr_hi.astype(F32)).astype(BF16)
    idxt, gatest, ncht = pl.pallas_call(
        functools.partial(_route_kernel, n_exp=n_exp, top_k=top_k, ts=ts),
        grid=(t // tm,),
        in_specs=[pl.BlockSpec((tm, d), lambda i: (i, 0)), _full((1, d)), _full((n_exp, d)),
                  _full((n_exp, d)), _full((n_exp, 1))],
        out_specs=[pl.BlockSpec((top_k, tm), lambda i: (0, i)), pl.BlockSpec((top_k, tm), lambda i: (0, i)),
                   pl.BlockSpec((tm // ts, n_exp, 1), lambda i: (i, 0, 0))],
        out_shape=[jax.ShapeDtypeStruct((top_k, t), I32), jax.ShapeDtypeStruct((top_k, t), F32),
                   jax.ShapeDtypeStruct((nt, n_exp, 1), I32)],
        compiler_params=_params(("arbitrary",)),
        name="moe_route",
    )(h, g.reshape(1, d), wr_hi, wr_lo, b_router.reshape(n_exp, 1))
    idx, gates, nch = idxt.T, gatest.T, ncht.reshape(nt, 1, n_exp)

    per_big = BIG_CHUNK // RUN_ALIGN
    n_big = srows // BIG_CHUNK
    n_small = n_exp * (per_big - 1)
    list_len = (n_big, n_big, n_small, n_small, SUBLANES)
    rank_out = pl.pallas_call(
        functools.partial(_rank_kernel, n_exp=n_exp, top_k=top_k, bm=bm, nbp=nbp),
        grid=(nt,),
        in_specs=[pl.BlockSpec((ts, top_k), lambda i: (i, 0)), pl.BlockSpec((top_k, ts), lambda i: (0, i)),
                  _full((nt, 1, n_exp))],
        out_specs=[pl.BlockSpec((ts, top_k), lambda i: (i, 0)), pl.BlockSpec((top_k, ts), lambda i: (0, i))]
        + [pl.BlockSpec((1, 1, r), lambda i: (i, 0, 0)) for r in list_len]
        + [_full((nbp, 1)), _full((nbp, 1)), _full((SUBLANES, n_exp))],
        out_shape=[jax.ShapeDtypeStruct((t, top_k), I32), jax.ShapeDtypeStruct((top_k, t), I32)]
        + [jax.ShapeDtypeStruct((nt, 1, r), I32) for r in list_len]
        + [jax.ShapeDtypeStruct((nbp, 1), I32), jax.ShapeDtypeStruct((nbp, 1), I32),
           jax.ShapeDtypeStruct((SUBLANES, n_exp), I32)],
        scratch_shapes=[pltpu.VMEM((1, n_exp), F32), pltpu.VMEM((1, n_exp), F32)],
        compiler_params=_params(("arbitrary",)),
        name="moe_rank",
    )(idx, idxt, nch)
    pos, post = rank_out[0], rank_out[1]
    blr, bgr, slr, sgr, cnt = rank_out[2:7]
    bexp, nxt, meta = rank_out[7:]
    lists = [cnt, blr, bgr, slr, sgr]
    bexp = bexp.reshape(nbp)
    nxt = nxt.reshape(nbp)
    n_used = meta[0, 0:1]
    last_blk = meta[1]

    def smem_lists(step=0):
        return [pl.BlockSpec((1, 1, a.shape[2]), lambda i, *_: (jnp.minimum(i + step, nt - 1), 0, 0),
                             memory_space=pltpu.SMEM) for a in lists]

    xs = pl.pallas_call(
        functools.partial(_dispatch_kernel, n_exp=n_exp, top_k=top_k, bm=bm),
        grid_spec=pltpu.PrefetchScalarGridSpec(
            num_scalar_prefetch=1,
            grid=(nt,),
            in_specs=smem_lists() + [
                pl.BlockSpec((ts, d), lambda i, last: (i, 0)),
                pl.BlockSpec((1, d), lambda i, last: (0, 0)),
                pl.BlockSpec((top_k, ts), lambda i, last: (0, i)),
            ],
            out_specs=pl.BlockSpec(memory_space=pl.ANY),
            scratch_shapes=[pltpu.VMEM((2, srows, dp), U32), pltpu.VMEM((bm, dp), U32), pltpu.SMEM((1,), I32),
                            pltpu.SemaphoreType.DMA((2,)), pltpu.SemaphoreType.DMA(())],
        ),
        out_shape=jax.ShapeDtypeStruct((n_slots, dp), U32),
        compiler_params=_params(("arbitrary",)),
        name="moe_dispatch",
    )(last_blk, *lists, h, g.reshape(1, d), post)

    def blk(b, nused_ref):
        return jnp.minimum(b, nused_ref[0] - 1)

    ys = pl.pallas_call(
        functools.partial(_ffn_kernel, f=f, layer=layer),
        grid_spec=pltpu.PrefetchScalarGridSpec(
            num_scalar_prefetch=3,
            grid=(n_blocks,),
            in_specs=[
                pl.BlockSpec((bm, dp), lambda b, be, nx, nu: (blk(b, nu), 0)),
                pl.BlockSpec(memory_space=pl.ANY),
                pl.BlockSpec((1, 1, 2 * f), lambda b, be, nx, nu: (be[blk(b, nu)], 0, 0)),
                pl.BlockSpec(memory_space=pl.ANY),
                pl.BlockSpec((1, 1, d), lambda b, be, nx, nu: (be[blk(b, nu)], 0, 0)),
            ],
            out_specs=pl.BlockSpec((bm, dp), lambda b, be, nx, nu: (blk(b, nu), 0)),
            scratch_shapes=[pltpu.VMEM((d, 2 * f), F32), pltpu.VMEM((f, d), F32),
                            pltpu.VMEM((d, 2 * f), BF16), pltpu.VMEM((f, d), BF16), pltpu.SemaphoreType.DMA((2,))],
        ),
        out_shape=jax.ShapeDtypeStruct((n_slots, dp), U32),
        compiler_params=_params(("arbitrary",)),
        name="moe_ffn",
    )(bexp, nxt, n_used, xs, w_gate_up, b_gate_up.reshape(n_exp, 1, 2 * f), w_down, b_down.reshape(n_exp, 1, d))

    return pl.pallas_call(
        functools.partial(_combine_kernel, top_k=top_k, final=final),
        grid=(nt,),
        in_specs=smem_lists() + smem_lists(step=1) + [
            pl.BlockSpec((ts, d), lambda i: (i, 0)),
            pl.BlockSpec((ts, top_k), lambda i: (i, 0)),
            pl.BlockSpec((ts, top_k), lambda i: (i, 0)),
            pl.BlockSpec(memory_space=pl.ANY),
            _full((1, d)),
        ],
        out_specs=pl.BlockSpec((ts, d), lambda i: (i, 0)),
        out_shape=jax.ShapeDtypeStruct((t, d), F32),
        scratch_shapes=[pltpu.VMEM((2, srows, dp), U32), pltpu.SemaphoreType.DMA((2,))],
        compiler_params=_params(("arbitrary",)),
        name="moe_combine",
    )(*lists, *lists, h, gates, pos, ys, final_g.reshape(1, d))


def kernel(x, mix_norm, ffn_norm, final_norm, lru_w_in, lru_conv_w, lru_conv_b, lru_w_a, lru_b_a, lru_w_x, lru_b_x, lru_a_param, lru_w_out, pool_w_in, pool_w_group, pool_scale, pool_w_out, sb_w_qkv, sb_w_out, moe_w_router, moe_b_router, moe_w_gate_up, moe_b_gate_up, moe_w_down, moe_b_down):
    batch, seq, d = x.shape
    depth = mix_norm.shape[0]
    h = x.reshape(batch * seq, d)
    for layer in range(depth):
        kind = layer % N_MIXERS
        slot = layer // N_MIXERS
        if kind == 0:
            h = _lru_layer(h, batch, seq, mix_norm[layer], lru_w_in[slot], lru_conv_w[slot], lru_conv_b[slot],
                           lru_w_a[slot], lru_b_a[slot], lru_w_x[slot], lru_b_x[slot], lru_a_param[slot],
                           lru_w_out[slot])
        elif kind == 1:
            h = _pool_layer(h, batch, seq, mix_norm[layer], pool_w_in[slot], pool_w_group[slot], pool_scale[slot],
                            pool_w_out[slot])
        else:
            h = _sb_layer(h, batch, seq, mix_norm[layer], sb_w_qkv[slot], sb_w_out[slot])
        h = _moe_layer(h, ffn_norm[layer], moe_w_router[layer], moe_b_router[layer], moe_w_gate_up,
                       moe_b_gate_up[layer], moe_w_down, moe_b_down[layer], final_norm,
                       final=(layer == depth - 1), layer=layer)
    return h.reshape(batch, seq, d)
```

```python
import functools
import math

import jax
import jax.numpy as jnp
from jax import lax
from jax.experimental import pallas as pl
from jax.experimental.pallas import tpu as pltpu

F32 = jnp.float32
BF16 = jnp.bfloat16
I32 = jnp.int32
U32 = jnp.uint32

RMS_EPS = 1e-6
LRU_C = 8.0
POOL_WINDOWS = (2, 4, 8, 16)
SB_HEADS = 16
SB_HEADS_PER_STEP = 4
LOG2E = 1.4426950408889634
TOP_K = 4
SWIGLU_LIMIT = 7.0
SWIGLU_ALPHA = 1.702
N_MIXERS = 3

V7X_VMEM_BYTES = 64 * 1024 * 1024
VMEM_LIMIT_BYTES = V7X_VMEM_BYTES - 8 * 1024 * 1024
SUBLANES = 8
LANES = 128

MOE_BLOCK = 256
MOE_SORT_TILE = 256
RUN_ALIGN = SUBLANES
BIG_CHUNK = 4 * RUN_ALIGN
WAIT_CHUNK = 16 * RUN_ALIGN


def _params(semantics, vmem=VMEM_LIMIT_BYTES):
    return pltpu.CompilerParams(dimension_semantics=semantics, vmem_limit_bytes=vmem)


def _pick(n, pref):
    t = min(n, pref)
    while n % t:
        t //= 2
    return t


def _rms(h, g):
    ms = jnp.mean(h * h, axis=-1, keepdims=True)
    return h * lax.rsqrt(ms + RMS_EPS) * g


def _sigmoid(x):
    return 1.0 / (1.0 + jnp.exp(-x))


def _softplus(x):
    return jnp.maximum(x, 0.0) + jnp.log(1.0 + jnp.exp(-jnp.abs(x)))


def _gelu_tanh(x):
    c = math.sqrt(2.0 / math.pi)
    return 0.5 * x * (1.0 + jnp.tanh(c * (x + 0.044715 * (x * x * x))))


def _full(shape):
    n = len(shape)
    return pl.BlockSpec(shape, lambda *_: (0,) * n)


def _pack_pairs(x):
    n = x.shape[1] // 2
    lo = lax.bitcast_convert_type(x[:, :n], U32)
    hi = lax.bitcast_convert_type(x[:, n:], U32)
    return lax.shift_right_logical(lo, jnp.uint32(16)) | (hi & jnp.uint32(0xFFFF0000))


def _unpack_pairs(w):
    lo = lax.bitcast_convert_type(lax.shift_left(w, jnp.uint32(16)), F32)
    hi = lax.bitcast_convert_type(w & jnp.uint32(0xFFFF0000), F32)
    return jnp.concatenate([lo.astype(BF16), hi.astype(BF16)], axis=1)


def _lru_kernel(h_ref, g_ref, win_ref, cw_ref, cb_ref, wax_ref, ba_ref, bx_ref, ap_ref, wout_ref,
                o_ref, hcar_ref, ext_ref, y_ref, *, ts, w, bw):
    s = pl.program_id(1)

    @pl.when(s == 0)
    def _():
        hcar_ref[...] = jnp.zeros_like(hcar_ref)
        ext_ref[0:SUBLANES, :] = jnp.zeros((SUBLANES, w), F32)

    h = h_ref[...]
    u = _rms(h, g_ref[...]).astype(BF16)
    proj = jnp.dot(u, win_ref[...], preferred_element_type=F32)
    row8 = lax.broadcasted_iota(I32, (SUBLANES, bw), 0)
    kw = cw_ref.shape[0]

    for c in range(w // bw):
        lo, hi = c * bw, (c + 1) * bw
        gate = proj[:, lo:hi]
        xb = proj[:, w + lo:w + hi]
        ext_ref[SUBLANES:SUBLANES + ts, lo:hi] = xb
        xc = cb_ref[:, lo:hi] + xb * cw_ref[kw - 1:kw, lo:hi]
        for j in range(1, kw):
            xc = xc + ext_ref[SUBLANES - j:SUBLANES - j + ts, lo:hi] * cw_ref[kw - 1 - j:kw - j, lo:hi]
        ext_ref[0:SUBLANES, lo:hi] = xb[ts - SUBLANES:ts]

        res = jnp.dot(xc.astype(BF16), wax_ref[c], preferred_element_type=F32)
        r = _sigmoid(res[:, :bw] + ba_ref[:, lo:hi])
        i = _sigmoid(res[:, bw:] + bx_ref[:, lo:hi])
        log_a = (-LRU_C * _softplus(-ap_ref[:, lo:hi])) * r
        a = jnp.exp(log_a)
        b = jnp.sqrt(-jnp.tanh(log_a) * (a * a + 1.0)) * (i * xc)
        hprev = hcar_ref[:, lo:hi]
        groups = []
        for gi in range(ts // SUBLANES):
            ag = a[gi * SUBLANES:(gi + 1) * SUBLANES]
            bg = b[gi * SUBLANES:(gi + 1) * SUBLANES]
            d = 1
            while d < SUBLANES:
                keep = row8 >= d
                a_sh = jnp.where(keep, pltpu.roll(ag, d, axis=0), 1.0)
                b_sh = jnp.where(keep, pltpu.roll(bg, d, axis=0), 0.0)
                bg = ag * b_sh + bg
                ag = ag * a_sh
                d *= 2
            hg = ag * hprev + bg
            hprev = hg[SUBLANES - 1:SUBLANES]
            groups.append(hg)
        hs = jnp.concatenate(groups, axis=0)
        hcar_ref[:, lo:hi] = hprev
        y_ref[:, lo:hi] = (hs * _gelu_tanh(gate)).astype(BF16)

    o_ref[...] = h + jnp.dot(y_ref[...], wout_ref[...], preferred_element_type=F32)


def _lru_layer(h, batch, seq, g, w_in, conv_w, conv_b, w_a, b_a, w_x, b_x, a_param, w_out):
    t, d = h.shape
    w = w_in.shape[1] // 2
    nblk, bw, _ = w_a.shape
    ts = _pick(seq, 256)
    ns = seq // ts
    wax = jnp.concatenate([w_a, w_x], axis=-1).astype(BF16)
    row = lambda v: v.reshape(1, -1)
    kern = functools.partial(_lru_kernel, ts=ts, w=w, bw=bw)
    return pl.pallas_call(
        kern,
        grid=(batch, ns),
        in_specs=[
            pl.BlockSpec((ts, d), lambda b, s: (b * ns + s, 0)),
            _full((1, d)), _full((d, 2 * w)), _full(conv_w.shape), _full((1, w)),
            _full(wax.shape), _full((1, w)), _full((1, w)), _full((1, w)), _full((w, d)),
        ],
        out_specs=pl.BlockSpec((ts, d), lambda b, s: (b * ns + s, 0)),
        out_shape=jax.ShapeDtypeStruct((t, d), F32),
        scratch_shapes=[pltpu.VMEM((1, w), F32), pltpu.VMEM((SUBLANES + ts, w), F32), pltpu.VMEM((ts, w), BF16)],
        compiler_params=_params(("arbitrary", "arbitrary")),
        name="lru_mixer",
    )(h, row(g), w_in.astype(BF16), conv_w, row(conv_b), wax, row(b_a), row(b_x), row(a_param),
      w_out.astype(BF16))


def _pool_kernel(h_ref, g_ref, win_ref, wg_ref, sc_ref, wout_ref, o_ref, ext_ref, m_ref, *, ts, d, wins):
    s = pl.program_id(1)
    pad = wins[-1]
    gw = d // len(wins)

    @pl.when(s == 0)
    def _():
        ext_ref[0:pad, :] = jnp.zeros((pad, d), F32)

    h = h_ref[...]
    u = _rms(h, g_ref[...]).astype(BF16)
    v = jnp.dot(u, win_ref[...], preferred_element_type=F32)
    ext_ref[pad:pad + ts, :] = v
    n_avail = (s * ts + lax.broadcasted_iota(I32, (ts, 1), 0) + 1).astype(F32)

    cur = ext_ref[...]
    width = 1
    for gi, win in enumerate(wins):
        while width < win:
            cur = cur + pltpu.roll(cur, width, axis=0)
            width *= 2
        lo, hi = gi * gw, (gi + 1) * gw
        mean = cur[pad:, 0:gw] / jnp.minimum(n_avail, float(win))
        pooled = mean - v[:, lo:hi]
        mixed = jnp.dot(pooled.astype(BF16), wg_ref[gi], preferred_element_type=F32)
        m_ref[:, lo:hi] = (mixed * sc_ref[:, lo:hi]).astype(BF16)
        cur = cur[:, gw:]
    ext_ref[0:pad, :] = v[ts - pad:ts]
    o_ref[...] = h + jnp.dot(m_ref[...], wout_ref[...], preferred_element_type=F32)


def _pool_layer(h, batch, seq, g, w_in, w_group, scale, w_out):
    t, d = h.shape
    ts = _pick(seq, 256)
    ns = seq // ts
    pad = POOL_WINDOWS[-1]
    kern = functools.partial(_pool_kernel, ts=ts, d=d, wins=POOL_WINDOWS)
    return pl.pallas_call(
        kern,
        grid=(batch, ns),
        in_specs=[
            pl.BlockSpec((ts, d), lambda b, s: (b * ns + s, 0)),
            _full((1, d)), _full((d, d)), _full(w_group.shape), _full((1, d)), _full((d, d)),
        ],
        out_specs=pl.BlockSpec((ts, d), lambda b, s: (b * ns + s, 0)),
        out_shape=jax.ShapeDtypeStruct((t, d), F32),
        scratch_shapes=[pltpu.VMEM((ts + pad, d), F32), pltpu.VMEM((ts, d), BF16)],
        compiler_params=_params(("arbitrary", "arbitrary")),
        name="pool_mixer",
    )(h, g.reshape(1, d), w_in.astype(BF16), w_group.astype(BF16), scale.reshape(1, d), w_out.astype(BF16))


def _sb_proj_kernel(h_ref, g_ref, wqt_ref, wk_ref, wvt_ref, qt_ref, k_ref, vt_ref):
    u = _rms(h_ref[...], g_ref[...]).astype(BF16)
    nt = (((1,), (1,)), ((), ()))
    qt_ref[...] = lax.dot_general(wqt_ref[...], u, nt, preferred_element_type=F32).astype(BF16)
    k_ref[...] = jnp.dot(u, wk_ref[...], preferred_element_type=F32).astype(BF16)
    vt_ref[...] = lax.dot_general(wvt_ref[...], u, nt, preferred_element_type=F32).astype(BF16)


def _sb_attn_kernel(qt_ref, k_ref, vt_ref, o_ref, lb_ref, zs_ref, a_ref, *, bq, dh, nh):
    qb = pl.program_id(2)
    krow = lax.broadcasted_iota(I32, (bq, bq), 0)
    qcol = lax.broadcasted_iota(I32, (bq, bq), 1)
    upper = (qcol >= krow).astype(BF16)
    causal = krow < qcol

    def step(chains, masked, st):
        for c, (hh, j) in enumerate(chains):
            qh = qt_ref[hh * dh:(hh + 1) * dh, :]
            kh = k_ref[pl.ds(j * bq, bq), hh * dh:(hh + 1) * dh]
            z = jnp.dot(kh, qh, preferred_element_type=F32)
            t = jnp.exp2(jnp.abs(z) * (-LOG2E))
            l1m = jnp.minimum(-z, 0.0) - jnp.log(1.0 + t)
            zs_ref[c] = z
            lm = jnp.where(causal, l1m, 0.0) if masked else l1m
            lb_ref[c] = lm.astype(BF16)
        st = list(st)
        for c, (hh, j) in enumerate(chains):
            acc, carry = st[hh]
            suffix = jnp.dot(upper, lb_ref[c], preferred_element_type=F32)
            a = jnp.exp(zs_ref[c] + suffix + carry)
            if masked:
                a = jnp.where(causal, a, 0.0)
            a_ref[c] = a.astype(BF16)
            st[hh] = (acc, carry + suffix[0:1, :])
        for c, (hh, j) in enumerate(chains):
            acc, carry = st[hh]
            vth = vt_ref[hh * dh:(hh + 1) * dh, pl.ds(j * bq, bq)]
            st[hh] = (acc + jnp.dot(vth, a_ref[c], preferred_element_type=F32), carry)
        return st

    heads = range(nh)
    zero = (jnp.zeros((dh, bq), F32), jnp.zeros((1, bq), F32))
    st = step([(hh, qb) for hh in heads], True, [zero] * nh)

    def pair_body(i, st):
        j = qb - 1 - 2 * i
        return step([(hh, j) for hh in heads] + [(hh, j - 1) for hh in heads], False, st)

    def single_body(i, st):
        return step([(hh, 0) for hh in heads], False, st)

    st = lax.fori_loop(0, lax.shift_right_logical(qb, 1), pair_body, st)
    st = lax.fori_loop(0, qb & 1, single_body, st)
    o_ref[...] = jnp.concatenate([st[hh][0].T for hh in heads], axis=1).astype(BF16)


def _sb_out_kernel(h_ref, o_ref, w_ref, out_ref):
    out_ref[...] = h_ref[...] + jnp.dot(o_ref[...], w_ref[...], preferred_element_type=F32)


def _sb_layer(h, batch, seq, g, w_qkv, w_out):
    t, d = h.shape
    dh = d // SB_HEADS
    scale = 1.0 / math.sqrt(dh)
    wqt = (w_qkv[:, :d] * scale).T.astype(BF16)
    wk = w_qkv[:, d:2 * d].astype(BF16)
    wvt = w_qkv[:, 2 * d:].T.astype(BF16)
    tp = _pick(t, 512)
    qt, k, vt = pl.pallas_call(
        _sb_proj_kernel,
        grid=(t // tp,),
        in_specs=[pl.BlockSpec((tp, d), lambda i: (i, 0)), _full((1, d)), _full((d, d)), _full((d, d)),
                  _full((d, d))],
        out_specs=[pl.BlockSpec((d, tp), lambda i: (0, i)), pl.BlockSpec((tp, d), lambda i: (i, 0)),
                   pl.BlockSpec((d, tp), lambda i: (0, i))],
        out_shape=[jax.ShapeDtypeStruct((d, t), BF16), jax.ShapeDtypeStruct((t, d), BF16),
                   jax.ShapeDtypeStruct((d, t), BF16)],
        compiler_params=_params(("arbitrary",)),
        name="sb_qkv_proj",
    )(h, g.reshape(1, d), wqt, wk, wvt)

    bq = _pick(seq, 256)
    nq = seq // bq
    nh = SB_HEADS_PER_STEP
    grp = nh * dh
    nchain = 2 * nh
    o = pl.pallas_call(
        functools.partial(_sb_attn_kernel, bq=bq, dh=dh, nh=nh),
        grid=(batch, d // grp, nq),
        in_specs=[
            pl.BlockSpec((grp, bq), lambda b, p, q: (p, b * nq + q)),
            pl.BlockSpec((seq, grp), lambda b, p, q: (b, p)),
            pl.BlockSpec((grp, seq), lambda b, p, q: (p, b)),
        ],
        out_specs=pl.BlockSpec((bq, grp), lambda b, p, q: (b * nq + q, p)),
        out_shape=jax.ShapeDtypeStruct((t, d), BF16),
        scratch_shapes=[pltpu.VMEM((nchain, bq, bq), BF16), pltpu.VMEM((nchain, bq, bq), F32),
                        pltpu.VMEM((nchain, bq, bq), BF16)],
        compiler_params=_params(("arbitrary", "arbitrary", "arbitrary")),
        name="sb_attention",
    )(qt, k, vt)

    return pl.pallas_call(
        _sb_out_kernel,
        grid=(t // tp,),
        in_specs=[pl.BlockSpec((tp, d), lambda i: (i, 0)), pl.BlockSpec((tp, d), lambda i: (i, 0)),
                  _full((d, d))],
        out_specs=pl.BlockSpec((tp, d), lambda i: (i, 0)),
        out_shape=jax.ShapeDtypeStruct((t, d), F32),
        compiler_params=_params(("arbitrary",)),
        name="sb_out_proj",
    )(h, o, w_out.astype(BF16))


def _route_kernel(h_ref, g_ref, wr_hi_ref, wr_lo_ref, br_ref, idxt_ref, gatet_ref, nch_ref, *, n_exp, top_k, ts):
    u = _rms(h_ref[...], g_ref[...])
    u_hi = u.astype(BF16)
    u_lo = (u - u_hi.astype(F32)).astype(BF16)
    nt_dims = (((1,), (1,)), ((), ()))
    logits = (lax.dot_general(wr_hi_ref[...], u_hi, nt_dims, preferred_element_type=F32)
              + lax.dot_general(wr_lo_ref[...], u_hi, nt_dims, preferred_element_type=F32)
              + lax.dot_general(wr_hi_ref[...], u_lo, nt_dims, preferred_element_type=F32)) + br_ref[...]
    tm = logits.shape[1]
    sub = lax.broadcasted_iota(I32, (n_exp, tm), 0)
    krow = lax.broadcasted_iota(I32, (top_k, tm), 0)
    vals = logits
    top_v, top_i = [], []
    for _ in range(top_k):
        m = jnp.max(vals, axis=0, keepdims=True)
        sel = jnp.min(jnp.where(vals == m, sub, n_exp), axis=0, keepdims=True)
        top_v.append(m)
        top_i.append(sel)
        vals = jnp.where(sub == sel, -jnp.inf, vals)
    exps = [jnp.exp(v - top_v[0]) for v in top_v]
    denom = exps[0]
    for e in exps[1:]:
        denom = denom + e
    idx = jnp.zeros((top_k, tm), I32)
    gates = jnp.zeros((top_k, tm), F32)
    for k in range(top_k):
        idx = jnp.where(krow == k, top_i[k], idx)
        gates = jnp.where(krow == k, exps[k] / denom, gates)
    idxt_ref[...] = idx
    gatet_ref[...] = gates
    chosen = jnp.where(sub == top_i[0], 1.0, 0.0)
    for k in range(1, top_k):
        chosen = jnp.where(sub == top_i[k], 1.0, chosen)
    for s in range(tm // ts):
        cnt = jnp.sum(chosen[:, s * ts:(s + 1) * ts], axis=1, keepdims=True)
        nch_ref[s] = jnp.floor((cnt + (RUN_ALIGN - 1)) * (1.0 / RUN_ALIGN)).astype(I32)


def _rank_kernel(idx_ref, idxt_ref, nchall_ref, pos_ref, post_ref, blr_ref, bgr_ref, slr_ref, sgr_ref, cnt_ref,
                 bexp_ref, nxt_ref, sblk_ref, spair_ref, meta_ref, run_ref, pstart_ref, *, n_exp, top_k, bm, nbp):
    i = pl.program_id(0)
    tm = idx_ref.shape[0]
    cpb = bm // RUN_ALIGN
    idx = idx_ref[...]
    lane = lax.broadcasted_iota(I32, (tm, n_exp), 1)
    ohs = [(idx[:, k:k + 1] == lane).astype(F32) for k in range(top_k)]
    oh = ohs[0]
    for o in ohs[1:]:
        oh = oh + o
    cnt = jnp.sum(oh, axis=0, keepdims=True)
    nch = jnp.floor((cnt + (RUN_ALIGN - 1)) * (1.0 / RUN_ALIGN))
    er = lax.broadcasted_iota(I32, (n_exp, n_exp), 0)
    ec = lax.broadcasted_iota(I32, (n_exp, n_exp), 1)

    def excl_cumsum_row(v):
        v8 = jnp.broadcast_to(v, (SUBLANES, n_exp)).astype(BF16)
        return jnp.dot(v8, (er < ec).astype(BF16), preferred_element_type=F32)[0:1, :]

    @pl.when(i == 0)
    def _():
        tot = jnp.sum(nchall_ref[...].astype(F32), axis=0)
        nb = jnp.floor((tot + (cpb - 1)) * (1.0 / cpb))
        bstart = excl_cumsum_row(nb)
        bend = bstart + nb
        pstart_ref[...] = bstart * cpb
        nst = jnp.floor((nb + 1.0) * 0.5)
        sstart = excl_cumsum_row(nst)
        send = sstart + nst
        step = lax.broadcasted_iota(I32, (nbp, n_exp), 0).astype(F32)
        se = jnp.sum((send <= step).astype(F32), axis=-1, keepdims=True)
        se = jnp.minimum(se, n_exp - 1)
        bexp_ref[...] = se.astype(I32)
        mine = lax.broadcasted_iota(I32, (nbp, n_exp), 1).astype(F32) == se
        pick = lambda v: jnp.sum(jnp.where(mine, v, 0.0), axis=-1, keepdims=True)
        local = step[:, 0:1] - pick(sstart)
        sblk_ref[...] = (pick(bstart) + 2.0 * local).astype(I32)
        spair_ref[...] = jnp.where(2.0 * local + 1.0 < pick(nb), 1.0, 0.0).astype(I32)
        n_used = jnp.sum(nst, axis=-1, keepdims=True)
        run_end = pick(send)
        nx = jnp.minimum(jnp.sum((send <= run_end).astype(F32), axis=-1, keepdims=True), n_exp - 1)
        nxt_ref[...] = jnp.where(run_end < n_used, nx, -1.0).astype(I32)
        last = jnp.where(nb > 0, bend - 1.0, -1.0)
        mrow = lax.broadcasted_iota(I32, (SUBLANES, n_exp), 0)
        meta = jnp.where(mrow == 0, jnp.broadcast_to(n_used, (SUBLANES, n_exp)),
                         jnp.where(mrow == 1, jnp.broadcast_to(last, (SUBLANES, n_exp)), 0.0))
        meta_ref[...] = meta.astype(I32)
        run_ref[...] = jnp.zeros_like(run_ref)

    tr = lax.broadcasted_iota(I32, (tm, tm), 0)
    tc = lax.broadcasted_iota(I32, (tm, tm), 1)
    off = excl_cumsum_row(nch)
    excl = jnp.dot((tc < tr).astype(BF16), oh.astype(BF16), preferred_element_type=F32)
    base = excl + off * RUN_ALIGN
    kcol = lax.broadcasted_iota(I32, (tm, top_k), 1)
    pos = jnp.zeros((tm, top_k), I32)
    for k in range(top_k):
        pk = jnp.sum(ohs[k] * base, axis=-1, keepdims=True).astype(I32)
        pos = jnp.where(kcol == k, pk, pos)
    pos_ref[...] = pos
    idxt = idxt_ref[...]
    sub = lax.broadcasted_iota(I32, (n_exp, tm), 0)
    ohts = [(idxt[k:k + 1, :] == sub).astype(F32) for k in range(top_k)]
    oht = ohts[0]
    for o in ohts[1:]:
        oht = oht + o
    cnt_col = jnp.sum(oht, axis=1, keepdims=True)
    nch_col = jnp.floor((cnt_col + (RUN_ALIGN - 1)) * (1.0 / RUN_ALIGN))
    off_col = jnp.dot((ec < er).astype(BF16), jnp.broadcast_to(nch_col, (n_exp, LANES)).astype(BF16),
                      preferred_element_type=F32)[:, 0:1]
    exclt = jnp.dot(oht.astype(BF16), (tr < tc).astype(BF16), preferred_element_type=F32)
    baset = exclt + off_col * RUN_ALIGN
    krow = lax.broadcasted_iota(I32, (top_k, tm), 0)
    post = jnp.zeros((top_k, tm), I32)
    for k in range(top_k):
        pk = jnp.sum(ohts[k] * baset, axis=0, keepdims=True).astype(I32)
        post = jnp.where(krow == k, pk, post)
    post_ref[...] = post

    gst_row = pstart_ref[...] + run_ref[...]
    gst_col = jnp.sum(jnp.where(er == ec, jnp.broadcast_to(gst_row, (n_exp, n_exp)), 0.0), axis=1, keepdims=True)
    per_big = BIG_CHUNK // RUN_ALIGN
    nbig = jnp.floor(nch_col * (1.0 / per_big))
    nsm = nch_col - nbig * per_big

    def excl_cumsum_col(v):
        return jnp.dot((ec < er).astype(BF16), jnp.broadcast_to(v, (n_exp, LANES)).astype(BF16),
                       preferred_element_type=F32)[:, 0:1]

    def copy_list(count, loc0, glob0, stride, n):
        before = excl_cumsum_col(count)
        j = lax.broadcasted_iota(I32, (n_exp, n), 1).astype(F32)
        owner = jnp.sum(((before + count) <= j).astype(F32), axis=0, keepdims=True)
        mine = lax.broadcasted_iota(I32, (n_exp, n), 0).astype(F32) == owner
        pick = lambda v: jnp.sum(jnp.where(mine, v, 0.0), axis=0, keepdims=True)
        piece = (j[0:1, :] - pick(before)) * stride
        return (pick(loc0) + piece).astype(I32), (pick(glob0) + piece).astype(I32)

    blr_ref[0], bgr_ref[0] = copy_list(nbig, off_col, gst_col, per_big, blr_ref.shape[2])
    slr_ref[0], sgr_ref[0] = copy_list(nsm, off_col + nbig * per_big, gst_col + nbig * per_big, 1, slr_ref.shape[2])
    clane = lax.broadcasted_iota(I32, (1, cnt_ref.shape[2]), 1)
    cnt_ref[0] = jnp.where(clane == 0, jnp.sum(nbig, axis=0, keepdims=True),
                           jnp.where(clane == 1, jnp.sum(nsm, axis=0, keepdims=True), 0.0)).astype(I32)
    run_ref[...] = run_ref[...] + nch


def _copy_units(lists):
    cnt_ref = lists[0]
    return cnt_ref[0, 0, 0] * (BIG_CHUNK // RUN_ALIGN) + cnt_ref[0, 0, 1]


def _start_copies(lists, make_copy):
    cnt_ref, blr_ref, bgr_ref, slr_ref, sgr_ref = lists

    def start(nrows, loc_ref, glob_ref):
        def body(j, c):
            make_copy(nrows, pl.multiple_of(loc_ref[0, 0, j] * RUN_ALIGN, RUN_ALIGN),
                      pl.multiple_of(glob_ref[0, 0, j] * RUN_ALIGN, RUN_ALIGN)).start()
            return c
        return body

    lax.fori_loop(0, cnt_ref[0, 0, 0], start(BIG_CHUNK, blr_ref, bgr_ref), 0)
    lax.fori_loop(0, cnt_ref[0, 0, 1], start(RUN_ALIGN, slr_ref, sgr_ref), 0)


def _wait_units(n, make_copy):
    per_wait = WAIT_CHUNK // RUN_ALIGN

    def wait_big(j, c):
        make_copy(WAIT_CHUNK, 0, 0).wait()
        return c

    def wait_small(j, c):
        make_copy(RUN_ALIGN, 0, 0).wait()
        return c

    nbig = lax.shift_right_logical(n, per_wait.bit_length() - 1)
    lax.fori_loop(0, nbig, wait_big, 0)
    lax.fori_loop(nbig * per_wait, n, wait_small, 0)


def _dispatch_kernel(last_ref, cnt_ref, blr_ref, bgr_ref, slr_ref, sgr_ref, h_ref, g_ref, post_ref, xs_ref, sbuf,
                     zbuf, pending, sem, zsem, *, n_exp, top_k, bm):
    i = pl.program_id(0)
    nt = pl.num_programs(0)
    slot = i % 2
    rows = sbuf.shape[1]

    @pl.when(i == 0)
    def _():
        zbuf[...] = jnp.zeros_like(zbuf)

        def zero_copy(e):
            return pltpu.make_async_copy(zbuf, xs_ref.at[pl.ds(last_ref[e] * bm, bm)], zsem)

        def start(e, c):
            @pl.when(last_ref[e] >= 0)
            def _():
                zero_copy(e).start()
            return c

        def wait(e, c):
            @pl.when(last_ref[e] >= 0)
            def _():
                zero_copy(e).wait()
            return c

        lax.fori_loop(0, n_exp, start, 0)
        lax.fori_loop(0, n_exp, wait, 0)

    u = _rms(h_ref[...], g_ref[...]).astype(BF16)
    tm = u.shape[0]
    post = post_ref[...]
    q = lax.broadcasted_iota(I32, (rows, tm), 0)
    perm = jnp.zeros((rows, tm), F32)
    for k in range(top_k):
        perm = jnp.where(q == post[k:k + 1, :], 1.0, perm)
    sbuf[slot] = _pack_pairs(jnp.dot(perm.astype(BF16), u, preferred_element_type=F32))

    def make_copy(s, nrows, src_row, dst_row):
        return pltpu.make_async_copy(sbuf.at[s, pl.ds(src_row, nrows)], xs_ref.at[pl.ds(dst_row, nrows)], sem.at[s])

    lists = (cnt_ref, blr_ref, bgr_ref, slr_ref, sgr_ref)
    _start_copies(lists, functools.partial(make_copy, slot))
    n = _copy_units(lists)

    @pl.when(i > 0)
    def _():
        _wait_units(pending[0], functools.partial(make_copy, 1 - slot))

    pending[0] = n

    @pl.when(i == nt - 1)
    def _():
        _wait_units(n, functools.partial(make_copy, slot))


def _ffn_kernel(bexp_ref, nxt_ref, sblk_ref, spair_ref, nused_ref, xa_ref, xb_ref, wgu_hbm, bgu_ref, wd_hbm, bd_ref,
                ys_ref, wgu_st, wd_st, wgu_bf, wd_bf, ybuf, pend, sem, ysem, *, f, layer):
    b = pl.program_id(0)
    last_step = pl.num_programs(0) - 1
    bm = xa_ref.shape[0]
    slot = b % 2
    prev = bexp_ref[jnp.maximum(b - 1, 0)]
    active = b < nused_ref[0]
    here = jnp.minimum(b, nused_ref[0] - 1)
    pair = spair_ref[here] == 1
    row0 = pl.multiple_of(sblk_ref[here] * bm, bm)

    def out_copy(s, nrows):
        return pltpu.make_async_copy(ybuf.at[s, pl.ds(0, nrows)], ys_ref.at[pl.ds(row0, nrows)], ysem.at[s])

    def drain(s):
        for nblk in (1, 2):
            @pl.when(pend[s] == nblk)
            def _():
                out_copy(s, nblk * bm).wait()
        pend[s] = 0

    @pl.when(b == 0)
    def _():
        pend[0] = 0
        pend[1] = 0

    def fetch(e):
        return (pltpu.make_async_copy(wgu_hbm.at[layer, e], wgu_st, sem.at[0]),
                pltpu.make_async_copy(wd_hbm.at[layer, e], wd_st, sem.at[1]))

    @pl.when(b == 0)
    def _():
        for c in fetch(bexp_ref[0]):
            c.start()

    @pl.when(active & ((b == 0) | (bexp_ref[b] != prev)))
    def _():
        for c in fetch(bexp_ref[b]):
            c.wait()
        wgu_bf[...] = wgu_st[...].astype(BF16)
        wd_bf[...] = wd_st[...].astype(BF16)

        @pl.when(nxt_ref[b] >= 0)
        def _():
            for c in fetch(nxt_ref[b]):
                c.start()

    def ffn(xw):
        x = _unpack_pairs(xw)
        gu = jnp.dot(x, wgu_bf[...], preferred_element_type=F32) + bgu_ref[0]
        gate = jnp.minimum(gu[:, :f], SWIGLU_LIMIT)
        up = jnp.clip(gu[:, f:], -SWIGLU_LIMIT, SWIGLU_LIMIT)
        glu = gate * _sigmoid(gate * SWIGLU_ALPHA)
        act = ((up + 1.0) * glu).astype(BF16)
        y = jnp.dot(act, wd_bf[...], preferred_element_type=F32) + bd_ref[0]
        return _pack_pairs(y.astype(BF16).astype(F32))

    @pl.when(active)
    def _():
        drain(slot)

    @pl.when(active & pair)
    def _():
        ybuf[slot] = ffn(jnp.concatenate([xa_ref[...], xb_ref[...]], axis=0))
        out_copy(slot, 2 * bm).start()
        pend[slot] = 2

    @pl.when(active & jnp.logical_not(pair))
    def _():
        ybuf[slot, 0:bm] = ffn(xa_ref[...])
        out_copy(slot, bm).start()
        pend[slot] = 1

    @pl.when(b == last_step)
    def _():
        drain(0)
        drain(1)


def _combine_kernel(*refs, top_k, final):
    lists, lists_nx = refs[0:5], refs[5:10]
    h_ref, gate_ref, pos_ref, ys_ref, fg_ref, o_ref, ybuf, sem = refs[10:]
    i = pl.program_id(0)
    nt = pl.num_programs(0)
    slot = i % 2
    rows = ybuf.shape[1]

    def make_copy(s, nrows, dst_row, src_row):
        return pltpu.make_async_copy(ys_ref.at[pl.ds(src_row, nrows)], ybuf.at[s, pl.ds(dst_row, nrows)], sem.at[s])

    @pl.when(i == 0)
    def _():
        ybuf[...] = jnp.zeros_like(ybuf)
        _start_copies(lists, functools.partial(make_copy, 0))

    @pl.when(i + 1 < nt)
    def _():
        _start_copies(lists_nx, functools.partial(make_copy, 1 - slot))

    _wait_units(_copy_units(lists), functools.partial(make_copy, slot))

    gates = gate_ref[...]
    pos = pos_ref[...]
    tm, d = h_ref.shape
    q = lax.broadcasted_iota(I32, (tm, rows), 1)
    wsel = jnp.zeros((tm, rows), F32)
    for k in range(top_k):
        wsel = jnp.where(q == pos[:, k:k + 1], gates[:, k:k + 1], wsel)
    out = h_ref[...] + jnp.dot(wsel.astype(BF16), _unpack_pairs(ybuf[slot]), preferred_element_type=F32)
    if final:
        out = _rms(out, fg_ref[...])
    o_ref[...] = out


def _moe_layer(h, g, w_router, b_router, w_gate_up, b_gate_up, w_down, b_down, final_g, final, layer):
    t, d = h.shape
    n_exp = w_router.shape[1]
    f = w_down.shape[2]
    top_k = TOP_K
    dp = d // 2
    bm = MOE_BLOCK
    ts = _pick(t, MOE_SORT_TILE)
    nt = t // ts
    srows = -(-(ts * top_k + n_exp * RUN_ALIGN) // LANES) * LANES
    n_blocks = -(-(t * top_k + nt * n_exp * (RUN_ALIGN - 1)) // bm) + n_exp
    n_slots = n_blocks * bm
    n_steps = (n_blocks + n_exp + 1) // 2
    nbp = -(-n_steps // SUBLANES) * SUBLANES

    tm = ts * max(1, _pick(nt, 2))
    wrt = w_router.T
    wr_hi = wrt.astype(BF16)
    wr_lo = (wrt - wr_hi.astype(F32)).astype(BF16)
    idxt, gatest, ncht = pl.pallas_call(
        functools.partial(_route_kernel, n_exp=n_exp, top_k=top_k, ts=ts),
        grid=(t // tm,),
        in_specs=[pl.BlockSpec((tm, d), lambda i: (i, 0)), _full((1, d)), _full((n_exp, d)),
                  _full((n_exp, d)), _full((n_exp, 1))],
        out_specs=[pl.BlockSpec((top_k, tm), lambda i: (0, i)), pl.BlockSpec((top_k, tm), lambda i: (0, i)),
                   pl.BlockSpec((tm // ts, n_exp, 1), lambda i: (i, 0, 0))],
        out_shape=[jax.ShapeDtypeStruct((top_k, t), I32), jax.ShapeDtypeStruct((top_k, t), F32),
                   jax.ShapeDtypeStruct((nt, n_exp, 1), I32)],
        compiler_params=_params(("arbitrary",)),
        name="moe_route",
    )(h, g.reshape(1, d), wr_hi, wr_lo, b_router.reshape(n_exp, 1))
    idx, gates, nch = idxt.T, gatest.T, ncht.reshape(nt, 1, n_exp)

    per_big = BIG_CHUNK // RUN_ALIGN
    n_big = srows // BIG_CHUNK
    n_small = n_exp * (per_big - 1)
    list_len = (n_big, n_big, n_small, n_small, SUBLANES)
    rank_out = pl.pallas_call(
        functools.partial(_rank_kernel, n_exp=n_exp, top_k=top_k, bm=bm, nbp=nbp),
        grid=(nt,),
        in_specs=[pl.BlockSpec((ts, top_k), lambda i: (i, 0)), pl.BlockSpec((top_k, ts), lambda i: (0, i)),
                  _full((nt, 1, n_exp))],
        out_specs=[pl.BlockSpec((ts, top_k), lambda i: (i, 0)), pl.BlockSpec((top_k, ts), lambda i: (0, i))]
        + [pl.BlockSpec((1, 1, r), lambda i: (i, 0, 0)) for r in list_len]
        + [_full((nbp, 1))] * 4 + [_full((SUBLANES, n_exp))],
        out_shape=[jax.ShapeDtypeStruct((t, top_k), I32), jax.ShapeDtypeStruct((top_k, t), I32)]
        + [jax.ShapeDtypeStruct((nt, 1, r), I32) for r in list_len]
        + [jax.ShapeDtypeStruct((nbp, 1), I32)] * 4 + [jax.ShapeDtypeStruct((SUBLANES, n_exp), I32)],
        scratch_shapes=[pltpu.VMEM((1, n_exp), F32), pltpu.VMEM((1, n_exp), F32)],
        compiler_params=_params(("arbitrary",)),
        name="moe_rank",
    )(idx, idxt, nch)
    pos, post = rank_out[0], rank_out[1]
    blr, bgr, slr, sgr, cnt = rank_out[2:7]
    bexp, nxt, sblk, spair = (a.reshape(nbp) for a in rank_out[7:11])
    meta = rank_out[11]
    lists = [cnt, blr, bgr, slr, sgr]
    n_used = meta[0, 0:1]
    last_blk = meta[1]

    def smem_lists(step=0):
        return [pl.BlockSpec((1, 1, a.shape[2]), lambda i, *_: (jnp.minimum(i + step, nt - 1), 0, 0),
                             memory_space=pltpu.SMEM) for a in lists]

    xs = pl.pallas_call(
        functools.partial(_dispatch_kernel, n_exp=n_exp, top_k=top_k, bm=bm),
        grid_spec=pltpu.PrefetchScalarGridSpec(
            num_scalar_prefetch=1,
            grid=(nt,),
            in_specs=smem_lists() + [
                pl.BlockSpec((ts, d), lambda i, last: (i, 0)),
                pl.BlockSpec((1, d), lambda i, last: (0, 0)),
                pl.BlockSpec((top_k, ts), lambda i, last: (0, i)),
            ],
            out_specs=pl.BlockSpec(memory_space=pl.ANY),
            scratch_shapes=[pltpu.VMEM((2, srows, dp), U32), pltpu.VMEM((bm, dp), U32), pltpu.SMEM((1,), I32),
                            pltpu.SemaphoreType.DMA((2,)), pltpu.SemaphoreType.DMA(())],
        ),
        out_shape=jax.ShapeDtypeStruct((n_slots, dp), U32),
        compiler_params=_params(("arbitrary",)),
        name="moe_dispatch",
    )(last_blk, *lists, h, g.reshape(1, d), post)

    def stp(s, nu):
        return jnp.minimum(s, nu[0] - 1)

    def blk_a(s, be, nx, sb, sp, nu):
        return (sb[stp(s, nu)], 0)

    def blk_b(s, be, nx, sb, sp, nu):
        return (sb[stp(s, nu)] + sp[stp(s, nu)], 0)

    def of_expert(shape):
        return pl.BlockSpec(shape, lambda s, be, nx, sb, sp, nu: (be[stp(s, nu)],) + (0,) * (len(shape) - 1))

    ys = pl.pallas_call(
        functools.partial(_ffn_kernel, f=f, layer=layer),
        grid_spec=pltpu.PrefetchScalarGridSpec(
            num_scalar_prefetch=5,
            grid=(n_steps,),
            in_specs=[
                pl.BlockSpec((bm, dp), blk_a), pl.BlockSpec((bm, dp), blk_b),
                pl.BlockSpec(memory_space=pl.ANY), of_expert((1, 1, 2 * f)),
                pl.BlockSpec(memory_space=pl.ANY), of_expert((1, 1, d)),
            ],
            out_specs=pl.BlockSpec(memory_space=pl.ANY),
            scratch_shapes=[pltpu.VMEM((d, 2 * f), F32), pltpu.VMEM((f, d), F32),
                            pltpu.VMEM((d, 2 * f), BF16), pltpu.VMEM((f, d), BF16),
                            pltpu.VMEM((2, 2 * bm, dp), U32), pltpu.SMEM((2,), I32),
                            pltpu.SemaphoreType.DMA((2,)), pltpu.SemaphoreType.DMA((2,))],
        ),
        out_shape=jax.ShapeDtypeStruct((n_slots, dp), U32),
        compiler_params=_params(("arbitrary",)),
        name="moe_ffn",
    )(bexp, nxt, sblk, spair, n_used, xs, xs, w_gate_up, b_gate_up.reshape(n_exp, 1, 2 * f), w_down,
      b_down.reshape(n_exp, 1, d))

    return pl.pallas_call(
        functools.partial(_combine_kernel, top_k=top_k, final=final),
        grid=(nt,),
        in_specs=smem_lists() + smem_lists(step=1) + [
            pl.BlockSpec((ts, d), lambda i: (i, 0)),
            pl.BlockSpec((ts, top_k), lambda i: (i, 0)),
            pl.BlockSpec((ts, top_k), lambda i: (i, 0)),
            pl.BlockSpec(memory_space=pl.ANY),
            _full((1, d)),
        ],
        out_specs=pl.BlockSpec((ts, d), lambda i: (i, 0)),
        out_shape=jax.ShapeDtypeStruct((t, d), F32),
        scratch_shapes=[pltpu.VMEM((2, srows, dp), U32), pltpu.SemaphoreType.DMA((2,))],
        compiler_params=_params(("arbitrary",)),
        name="moe_combine",
    )(*lists, *lists, h, gates, pos, ys, final_g.reshape(1, d))


def kernel(x, mix_norm, ffn_norm, final_norm, lru_w_in, lru_conv_w, lru_conv_b, lru_w_a, lru_b_a, lru_w_x, lru_b_x, lru_a_param, lru_w_out, pool_w_in, pool_w_group, pool_scale, pool_w_out, sb_w_qkv, sb_w_out, moe_w_router, moe_b_router, moe_w_gate_up, moe_b_gate_up, moe_w_down, moe_b_down):
    batch, seq, d = x.shape
    depth = mix_norm.shape[0]
    h = x.reshape(batch * seq, d)
    for layer in range(depth):
        kind = layer % N_MIXERS
        slot = layer // N_MIXERS
        if kind == 0:
            h = _lru_layer(h, batch, seq, mix_norm[layer], lru_w_in[slot], lru_conv_w[slot], lru_conv_b[slot],
                           lru_w_a[slot], lru_b_a[slot], lru_w_x[slot], lru_b_x[slot], lru_a_param[slot],
                           lru_w_out[slot])
        elif kind == 1:
            h = _pool_layer(h, batch, seq, mix_norm[layer], pool_w_in[slot], pool_w_group[slot], pool_scale[slot],
                            pool_w_out[slot])
        else:
            h = _sb_layer(h, batch, seq, mix_norm[layer], sb_w_qkv[slot], sb_w_out[slot])
        h = _moe_layer(h, ffn_norm[layer], moe_w_router[layer], moe_b_router[layer], moe_w_gate_up,
                       moe_b_gate_up[layer], moe_w_down, moe_b_down[layer], final_norm,
                       final=(layer == depth - 1), layer=layer)
    return h.reshape(batch, seq, d)
```

```python
import functools
import math

import jax
import jax.numpy as jnp
from jax import lax
from jax.experimental import pallas as pl
from jax.experimental.pallas import tpu as pltpu

F32 = jnp.float32
BF16 = jnp.bfloat16
I32 = jnp.int32
U32 = jnp.uint32

RMS_EPS = 1e-6
LRU_C = 8.0
POOL_WINDOWS = (2, 4, 8, 16)
SB_HEADS = 16
SB_HEADS_PER_STEP = 8
LOG2E = 1.4426950408889634
TOP_K = 4
SWIGLU_LIMIT = 7.0
SWIGLU_ALPHA = 1.702
N_MIXERS = 3

V7X_VMEM_BYTES = 64 * 1024 * 1024
VMEM_LIMIT_BYTES = V7X_VMEM_BYTES - 8 * 1024 * 1024
SUBLANES = 8
LANES = 128

MOE_BLOCK = 256
MOE_SORT_TILE = 256
RUN_ALIGN = SUBLANES
BIG_CHUNK = 4 * RUN_ALIGN
WAIT_CHUNK = 16 * RUN_ALIGN


def _params(semantics, vmem=VMEM_LIMIT_BYTES):
    return pltpu.CompilerParams(dimension_semantics=semantics, vmem_limit_bytes=vmem)


def _pick(n, pref):
    t = min(n, pref)
    while n % t:
        t //= 2
    return t


def _rms(h, g):
    ms = jnp.mean(h * h, axis=-1, keepdims=True)
    return h * lax.rsqrt(ms + RMS_EPS) * g


def _sigmoid(x):
    return 1.0 / (1.0 + jnp.exp(-x))


def _softplus(x):
    return jnp.maximum(x, 0.0) + jnp.log(1.0 + jnp.exp(-jnp.abs(x)))


def _gelu_tanh(x):
    c = math.sqrt(2.0 / math.pi)
    return 0.5 * x * (1.0 + jnp.tanh(c * (x + 0.044715 * (x * x * x))))


def _full(shape):
    n = len(shape)
    return pl.BlockSpec(shape, lambda *_: (0,) * n)


def _pack_pairs(x):
    n = x.shape[1] // 2
    lo = lax.bitcast_convert_type(x[:, :n], U32)
    hi = lax.bitcast_convert_type(x[:, n:], U32)
    return lax.shift_right_logical(lo, jnp.uint32(16)) | (hi & jnp.uint32(0xFFFF0000))


def _unpack_pairs(w):
    lo = lax.bitcast_convert_type(lax.shift_left(w, jnp.uint32(16)), F32)
    hi = lax.bitcast_convert_type(w & jnp.uint32(0xFFFF0000), F32)
    return jnp.concatenate([lo.astype(BF16), hi.astype(BF16)], axis=1)


def _lru_kernel(h_ref, g_ref, win_ref, cw_ref, cb_ref, wax_ref, ba_ref, bx_ref, ap_ref, wout_ref,
                o_ref, hcar_ref, ext_ref, y_ref, *, ts, w, bw):
    s = pl.program_id(1)

    @pl.when(s == 0)
    def _():
        hcar_ref[...] = jnp.zeros_like(hcar_ref)
        ext_ref[0:SUBLANES, :] = jnp.zeros((SUBLANES, w), F32)

    h = h_ref[...]
    u = _rms(h, g_ref[...]).astype(BF16)
    proj = jnp.dot(u, win_ref[...], preferred_element_type=F32)
    row8 = lax.broadcasted_iota(I32, (SUBLANES, bw), 0)
    kw = cw_ref.shape[0]

    for c in range(w // bw):
        lo, hi = c * bw, (c + 1) * bw
        gate = proj[:, lo:hi]
        xb = proj[:, w + lo:w + hi]
        ext_ref[SUBLANES:SUBLANES + ts, lo:hi] = xb
        xc = cb_ref[:, lo:hi] + xb * cw_ref[kw - 1:kw, lo:hi]
        for j in range(1, kw):
            xc = xc + ext_ref[SUBLANES - j:SUBLANES - j + ts, lo:hi] * cw_ref[kw - 1 - j:kw - j, lo:hi]
        ext_ref[0:SUBLANES, lo:hi] = xb[ts - SUBLANES:ts]

        res = jnp.dot(xc.astype(BF16), wax_ref[c], preferred_element_type=F32)
        r = _sigmoid(res[:, :bw] + ba_ref[:, lo:hi])
        i = _sigmoid(res[:, bw:] + bx_ref[:, lo:hi])
        log_a = (-LRU_C * _softplus(-ap_ref[:, lo:hi])) * r
        a = jnp.exp(log_a)
        b = jnp.sqrt(-jnp.tanh(log_a) * (a * a + 1.0)) * (i * xc)
        hprev = hcar_ref[:, lo:hi]
        groups = []
        for gi in range(ts // SUBLANES):
            ag = a[gi * SUBLANES:(gi + 1) * SUBLANES]
            bg = b[gi * SUBLANES:(gi + 1) * SUBLANES]
            d = 1
            while d < SUBLANES:
                keep = row8 >= d
                a_sh = jnp.where(keep, pltpu.roll(ag, d, axis=0), 1.0)
                b_sh = jnp.where(keep, pltpu.roll(bg, d, axis=0), 0.0)
                bg = ag * b_sh + bg
                ag = ag * a_sh
                d *= 2
            hg = ag * hprev + bg
            hprev = hg[SUBLANES - 1:SUBLANES]
            groups.append(hg)
        hs = jnp.concatenate(groups, axis=0)
        hcar_ref[:, lo:hi] = hprev
        y_ref[:, lo:hi] = (hs * _gelu_tanh(gate)).astype(BF16)

    o_ref[...] = h + jnp.dot(y_ref[...], wout_ref[...], preferred_element_type=F32)


def _lru_layer(h, batch, seq, g, w_in, conv_w, conv_b, w_a, b_a, w_x, b_x, a_param, w_out):
    t, d = h.shape
    w = w_in.shape[1] // 2
    nblk, bw, _ = w_a.shape
    ts = _pick(seq, 512)
    ns = seq // ts
    wax = jnp.concatenate([w_a, w_x], axis=-1).astype(BF16)
    row = lambda v: v.reshape(1, -1)
    kern = functools.partial(_lru_kernel, ts=ts, w=w, bw=bw)
    return pl.pallas_call(
        kern,
        grid=(batch, ns),
        in_specs=[
            pl.BlockSpec((ts, d), lambda b, s: (b * ns + s, 0)),
            _full((1, d)), _full((d, 2 * w)), _full(conv_w.shape), _full((1, w)),
            _full(wax.shape), _full((1, w)), _full((1, w)), _full((1, w)), _full((w, d)),
        ],
        out_specs=pl.BlockSpec((ts, d), lambda b, s: (b * ns + s, 0)),
        out_shape=jax.ShapeDtypeStruct((t, d), F32),
        scratch_shapes=[pltpu.VMEM((1, w), F32), pltpu.VMEM((SUBLANES + ts, w), F32), pltpu.VMEM((ts, w), BF16)],
        compiler_params=_params(("arbitrary", "arbitrary")),
        name="lru_mixer",
    )(h, row(g), w_in.astype(BF16), conv_w, row(conv_b), wax, row(b_a), row(b_x), row(a_param),
      w_out.astype(BF16))


def _pool_kernel(h_ref, g_ref, win_ref, wg_ref, sc_ref, wout_ref, o_ref, ext_ref, m_ref, *, ts, d, wins):
    s = pl.program_id(1)
    pad = wins[-1]
    gw = d // len(wins)

    @pl.when(s == 0)
    def _():
        ext_ref[0:pad, :] = jnp.zeros((pad, d), F32)

    h = h_ref[...]
    u = _rms(h, g_ref[...]).astype(BF16)
    v = jnp.dot(u, win_ref[...], preferred_element_type=F32)
    ext_ref[pad:pad + ts, :] = v
    n_avail = (s * ts + lax.broadcasted_iota(I32, (ts, 1), 0) + 1).astype(F32)

    cur = ext_ref[...]
    width = 1
    for gi, win in enumerate(wins):
        while width < win:
            cur = cur + pltpu.roll(cur, width, axis=0)
            width *= 2
        lo, hi = gi * gw, (gi + 1) * gw
        mean = cur[pad:, 0:gw] / jnp.minimum(n_avail, float(win))
        pooled = mean - v[:, lo:hi]
        mixed = jnp.dot(pooled.astype(BF16), wg_ref[gi], preferred_element_type=F32)
        m_ref[:, lo:hi] = (mixed * sc_ref[:, lo:hi]).astype(BF16)
        cur = cur[:, gw:]
    ext_ref[0:pad, :] = v[ts - pad:ts]
    o_ref[...] = h + jnp.dot(m_ref[...], wout_ref[...], preferred_element_type=F32)


def _pool_layer(h, batch, seq, g, w_in, w_group, scale, w_out):
    t, d = h.shape
    ts = _pick(seq, 256)
    ns = seq // ts
    pad = POOL_WINDOWS[-1]
    kern = functools.partial(_pool_kernel, ts=ts, d=d, wins=POOL_WINDOWS)
    return pl.pallas_call(
        kern,
        grid=(batch, ns),
        in_specs=[
            pl.BlockSpec((ts, d), lambda b, s: (b * ns + s, 0)),
            _full((1, d)), _full((d, d)), _full(w_group.shape), _full((1, d)), _full((d, d)),
        ],
        out_specs=pl.BlockSpec((ts, d), lambda b, s: (b * ns + s, 0)),
        out_shape=jax.ShapeDtypeStruct((t, d), F32),
        scratch_shapes=[pltpu.VMEM((ts + pad, d), F32), pltpu.VMEM((ts, d), BF16)],
        compiler_params=_params(("arbitrary", "arbitrary")),
        name="pool_mixer",
    )(h, g.reshape(1, d), w_in.astype(BF16), w_group.astype(BF16), scale.reshape(1, d), w_out.astype(BF16))


def _sb_proj_kernel(h_ref, g_ref, wqt_ref, wk_ref, wvt_ref, qt_ref, k_ref, vt_ref):
    u = _rms(h_ref[...], g_ref[...]).astype(BF16)
    nt = (((1,), (1,)), ((), ()))
    qt_ref[...] = lax.dot_general(wqt_ref[...], u, nt, preferred_element_type=F32).astype(BF16)
    k_ref[...] = jnp.dot(u, wk_ref[...], preferred_element_type=F32).astype(BF16)
    vt_ref[...] = lax.dot_general(wvt_ref[...], u, nt, preferred_element_type=F32).astype(BF16)


def _sb_attn_kernel(qt_ref, k_ref, vt_ref, o_ref, lb_ref, zs_ref, a_ref, *, bq, dh, nh):
    qb = pl.program_id(2)
    krow = lax.broadcasted_iota(I32, (bq, bq), 0)
    qcol = lax.broadcasted_iota(I32, (bq, bq), 1)
    upper = (qcol >= krow).astype(BF16)
    causal = krow < qcol

    def step(chains, masked, st):
        for c, (hh, j) in enumerate(chains):
            qh = qt_ref[hh * dh:(hh + 1) * dh, :]
            kh = k_ref[pl.ds(j * bq, bq), hh * dh:(hh + 1) * dh]
            z = jnp.dot(kh, qh, preferred_element_type=F32)
            t = jnp.exp2(jnp.abs(z) * (-LOG2E))
            l1m = jnp.minimum(-z, 0.0) - jnp.log(1.0 + t)
            zs_ref[c] = z
            lm = jnp.where(causal, l1m, 0.0) if masked else l1m
            lb_ref[c] = lm.astype(BF16)
        st = list(st)
        for c, (hh, j) in enumerate(chains):
            acc, carry = st[hh]
            suffix = jnp.dot(upper, lb_ref[c], preferred_element_type=F32)
            a = jnp.exp(zs_ref[c] + suffix + carry)
            if masked:
                a = jnp.where(causal, a, 0.0)
            a_ref[c] = a.astype(BF16)
            st[hh] = (acc, carry + suffix[0:1, :])
        for c, (hh, j) in enumerate(chains):
            acc, carry = st[hh]
            vth = vt_ref[hh * dh:(hh + 1) * dh, pl.ds(j * bq, bq)]
            st[hh] = (acc + jnp.dot(vth, a_ref[c], preferred_element_type=F32), carry)
        return st

    heads = range(nh)
    zero = (jnp.zeros((dh, bq), F32), jnp.zeros((1, bq), F32))
    st = step([(hh, qb) for hh in heads], True, [zero] * nh)

    def pair_body(i, st):
        j = qb - 1 - 2 * i
        return step([(hh, j) for hh in heads] + [(hh, j - 1) for hh in heads], False, st)

    def single_body(i, st):
        return step([(hh, 0) for hh in heads], False, st)

    st = lax.fori_loop(0, lax.shift_right_logical(qb, 1), pair_body, st)
    st = lax.fori_loop(0, qb & 1, single_body, st)
    o_ref[...] = jnp.concatenate([st[hh][0].T for hh in heads], axis=1).astype(BF16)


def _sb_out_kernel(h_ref, o_ref, w_ref, out_ref):
    out_ref[...] = h_ref[...] + jnp.dot(o_ref[...], w_ref[...], preferred_element_type=F32)


def _sb_layer(h, batch, seq, g, w_qkv, w_out):
    t, d = h.shape
    dh = d // SB_HEADS
    scale = 1.0 / math.sqrt(dh)
    wqt = (w_qkv[:, :d] * scale).T.astype(BF16)
    wk = w_qkv[:, d:2 * d].astype(BF16)
    wvt = w_qkv[:, 2 * d:].T.astype(BF16)
    tp = _pick(t, 512)
    qt, k, vt = pl.pallas_call(
        _sb_proj_kernel,
        grid=(t // tp,),
        in_specs=[pl.BlockSpec((tp, d), lambda i: (i, 0)), _full((1, d)), _full((d, d)), _full((d, d)),
                  _full((d, d))],
        out_specs=[pl.BlockSpec((d, tp), lambda i: (0, i)), pl.BlockSpec((tp, d), lambda i: (i, 0)),
                   pl.BlockSpec((d, tp), lambda i: (0, i))],
        out_shape=[jax.ShapeDtypeStruct((d, t), BF16), jax.ShapeDtypeStruct((t, d), BF16),
                   jax.ShapeDtypeStruct((d, t), BF16)],
        compiler_params=_params(("arbitrary",)),
        name="sb_qkv_proj",
    )(h, g.reshape(1, d), wqt, wk, wvt)

    bq = _pick(seq, 256)
    nq = seq // bq
    nh = _pick(SB_HEADS, SB_HEADS_PER_STEP)
    grp = nh * dh
    nchain = 2 * nh
    o = pl.pallas_call(
        functools.partial(_sb_attn_kernel, bq=bq, dh=dh, nh=nh),
        grid=(batch, d // grp, nq),
        in_specs=[
            pl.BlockSpec((grp, bq), lambda b, p, q: (p, b * nq + q)),
            pl.BlockSpec((seq, grp), lambda b, p, q: (b, p)),
            pl.BlockSpec((grp, seq), lambda b, p, q: (p, b)),
        ],
        out_specs=pl.BlockSpec((bq, grp), lambda b, p, q: (b * nq + q, p)),
        out_shape=jax.ShapeDtypeStruct((t, d), BF16),
        scratch_shapes=[pltpu.VMEM((nchain, bq, bq), BF16), pltpu.VMEM((nchain, bq, bq), F32),
                        pltpu.VMEM((nchain, bq, bq), BF16)],
        compiler_params=_params(("arbitrary", "arbitrary", "arbitrary")),
        name="sb_attention",
    )(qt, k, vt)

    return pl.pallas_call(
        _sb_out_kernel,
        grid=(t // tp,),
        in_specs=[pl.BlockSpec((tp, d), lambda i: (i, 0)), pl.BlockSpec((tp, d), lambda i: (i, 0)),
                  _full((d, d))],
        out_specs=pl.BlockSpec((tp, d), lambda i: (i, 0)),
        out_shape=jax.ShapeDtypeStruct((t, d), F32),
        compiler_params=_params(("arbitrary",)),
        name="sb_out_proj",
    )(h, o, w_out.astype(BF16))


def _route_kernel(h_ref, g_ref, wr_hi_ref, wr_lo_ref, br_ref, idxt_ref, gatet_ref, nch_ref, *, n_exp, top_k, ts):
    u = _rms(h_ref[...], g_ref[...])
    u_hi = u.astype(BF16)
    u_lo = (u - u_hi.astype(F32)).astype(BF16)
    nt_dims = (((1,), (1,)), ((), ()))
    logits = (lax.dot_general(wr_hi_ref[...], u_hi, nt_dims, preferred_element_type=F32)
              + lax.dot_general(wr_lo_ref[...], u_hi, nt_dims, preferred_element_type=F32)
              + lax.dot_general(wr_hi_ref[...], u_lo, nt_dims, preferred_element_type=F32)) + br_ref[...]
    tm = logits.shape[1]
    sub = lax.broadcasted_iota(I32, (n_exp, tm), 0)
    krow = lax.broadcasted_iota(I32, (top_k, tm), 0)
    vals = logits
    top_v, top_i = [], []
    for _ in range(top_k):
        m = jnp.max(vals, axis=0, keepdims=True)
        sel = jnp.min(jnp.where(vals == m, sub, n_exp), axis=0, keepdims=True)
        top_v.append(m)
        top_i.append(sel)
        vals = jnp.where(sub == sel, -jnp.inf, vals)
    exps = [jnp.exp(v - top_v[0]) for v in top_v]
    denom = exps[0]
    for e in exps[1:]:
        denom = denom + e
    idx = jnp.zeros((top_k, tm), I32)
    gates = jnp.zeros((top_k, tm), F32)
    for k in range(top_k):
        idx = jnp.where(krow == k, top_i[k], idx)
        gates = jnp.where(krow == k, exps[k] / denom, gates)
    idxt_ref[...] = idx
    gatet_ref[...] = gates
    chosen = jnp.where(sub == top_i[0], 1.0, 0.0)
    for k in range(1, top_k):
        chosen = jnp.where(sub == top_i[k], 1.0, chosen)
    for s in range(tm // ts):
        cnt = jnp.sum(chosen[:, s * ts:(s + 1) * ts], axis=1, keepdims=True)
        nch_ref[s] = jnp.floor((cnt + (RUN_ALIGN - 1)) * (1.0 / RUN_ALIGN)).astype(I32)


def _rank_kernel(idx_ref, idxt_ref, nchall_ref, pos_ref, post_ref, blr_ref, bgr_ref, slr_ref, sgr_ref, cnt_ref,
                 bexp_ref, nxt_ref, sblk_ref, spair_ref, meta_ref, run_ref, pstart_ref, *, n_exp, top_k, bm, nbp):
    i = pl.program_id(0)
    tm = idx_ref.shape[0]
    cpb = bm // RUN_ALIGN
    idx = idx_ref[...]
    lane = lax.broadcasted_iota(I32, (tm, n_exp), 1)
    ohs = [(idx[:, k:k + 1] == lane).astype(F32) for k in range(top_k)]
    oh = ohs[0]
    for o in ohs[1:]:
        oh = oh + o
    cnt = jnp.sum(oh, axis=0, keepdims=True)
    nch = jnp.floor((cnt + (RUN_ALIGN - 1)) * (1.0 / RUN_ALIGN))
    er = lax.broadcasted_iota(I32, (n_exp, n_exp), 0)
    ec = lax.broadcasted_iota(I32, (n_exp, n_exp), 1)

    def excl_cumsum_row(v):
        v8 = jnp.broadcast_to(v, (SUBLANES, n_exp)).astype(BF16)
        return jnp.dot(v8, (er < ec).astype(BF16), preferred_element_type=F32)[0:1, :]

    @pl.when(i == 0)
    def _():
        tot = jnp.sum(nchall_ref[...].astype(F32), axis=0)
        nb = jnp.floor((tot + (cpb - 1)) * (1.0 / cpb))
        bstart = excl_cumsum_row(nb)
        bend = bstart + nb
        pstart_ref[...] = bstart * cpb
        nst = jnp.floor((nb + 1.0) * 0.5)
        sstart = excl_cumsum_row(nst)
        send = sstart + nst
        step = lax.broadcasted_iota(I32, (nbp, n_exp), 0).astype(F32)
        se = jnp.sum((send <= step).astype(F32), axis=-1, keepdims=True)
        se = jnp.minimum(se, n_exp - 1)
        bexp_ref[...] = se.astype(I32)
        mine = lax.broadcasted_iota(I32, (nbp, n_exp), 1).astype(F32) == se
        pick = lambda v: jnp.sum(jnp.where(mine, v, 0.0), axis=-1, keepdims=True)
        local = step[:, 0:1] - pick(sstart)
        sblk_ref[...] = (pick(bstart) + 2.0 * local).astype(I32)
        spair_ref[...] = jnp.where(2.0 * local + 1.0 < pick(nb), 1.0, 0.0).astype(I32)
        n_used = jnp.sum(nst, axis=-1, keepdims=True)
        run_end = pick(send)
        nx = jnp.minimum(jnp.sum((send <= run_end).astype(F32), axis=-1, keepdims=True), n_exp - 1)
        nxt_ref[...] = jnp.where(run_end < n_used, nx, -1.0).astype(I32)
        last = jnp.where(nb > 0, bend - 1.0, -1.0)
        mrow = lax.broadcasted_iota(I32, (SUBLANES, n_exp), 0)
        meta = jnp.where(mrow == 0, jnp.broadcast_to(n_used, (SUBLANES, n_exp)),
                         jnp.where(mrow == 1, jnp.broadcast_to(last, (SUBLANES, n_exp)), 0.0))
        meta_ref[...] = meta.astype(I32)
        run_ref[...] = jnp.zeros_like(run_ref)

    tr = lax.broadcasted_iota(I32, (tm, tm), 0)
    tc = lax.broadcasted_iota(I32, (tm, tm), 1)
    off = excl_cumsum_row(nch)
    excl = jnp.dot((tc < tr).astype(BF16), oh.astype(BF16), preferred_element_type=F32)
    base = excl + off * RUN_ALIGN
    kcol = lax.broadcasted_iota(I32, (tm, top_k), 1)
    pos = jnp.zeros((tm, top_k), I32)
    for k in range(top_k):
        pk = jnp.sum(ohs[k] * base, axis=-1, keepdims=True).astype(I32)
        pos = jnp.where(kcol == k, pk, pos)
    pos_ref[...] = pos
    idxt = idxt_ref[...]
    sub = lax.broadcasted_iota(I32, (n_exp, tm), 0)
    ohts = [(idxt[k:k + 1, :] == sub).astype(F32) for k in range(top_k)]
    oht = ohts[0]
    for o in ohts[1:]:
        oht = oht + o
    cnt_col = jnp.sum(oht, axis=1, keepdims=True)
    nch_col = jnp.floor((cnt_col + (RUN_ALIGN - 1)) * (1.0 / RUN_ALIGN))
    off_col = jnp.dot((ec < er).astype(BF16), jnp.broadcast_to(nch_col, (n_exp, LANES)).astype(BF16),
                      preferred_element_type=F32)[:, 0:1]
    exclt = jnp.dot(oht.astype(BF16), (tr < tc).astype(BF16), preferred_element_type=F32)
    baset = exclt + off_col * RUN_ALIGN
    krow = lax.broadcasted_iota(I32, (top_k, tm), 0)
    post = jnp.zeros((top_k, tm), I32)
    for k in range(top_k):
        pk = jnp.sum(ohts[k] * baset, axis=0, keepdims=True).astype(I32)
        post = jnp.where(krow == k, pk, post)
    post_ref[...] = post

    gst_row = pstart_ref[...] + run_ref[...]
    gst_col = jnp.sum(jnp.where(er == ec, jnp.broadcast_to(gst_row, (n_exp, n_exp)), 0.0), axis=1, keepdims=True)
    per_big = BIG_CHUNK // RUN_ALIGN
    nbig = jnp.floor(nch_col * (1.0 / per_big))
    nsm = nch_col - nbig * per_big

    def excl_cumsum_col(v):
        return jnp.dot((ec < er).astype(BF16), jnp.broadcast_to(v, (n_exp, LANES)).astype(BF16),
                       preferred_element_type=F32)[:, 0:1]

    def copy_list(count, loc0, glob0, stride, n):
        before = excl_cumsum_col(count)
        j = lax.broadcasted_iota(I32, (n_exp, n), 1).astype(F32)
        owner = jnp.sum(((before + count) <= j).astype(F32), axis=0, keepdims=True)
        mine = lax.broadcasted_iota(I32, (n_exp, n), 0).astype(F32) == owner
        pick = lambda v: jnp.sum(jnp.where(mine, v, 0.0), axis=0, keepdims=True)
        piece = (j[0:1, :] - pick(before)) * stride
        return (pick(loc0) + piece).astype(I32), (pick(glob0) + piece).astype(I32)

    blr_ref[0], bgr_ref[0] = copy_list(nbig, off_col, gst_col, per_big, blr_ref.shape[2])
    slr_ref[0], sgr_ref[0] = copy_list(nsm, off_col + nbig * per_big, gst_col + nbig * per_big, 1, slr_ref.shape[2])
    clane = lax.broadcasted_iota(I32, (1, cnt_ref.shape[2]), 1)
    cnt_ref[0] = jnp.where(clane == 0, jnp.sum(nbig, axis=0, keepdims=True),
                           jnp.where(clane == 1, jnp.sum(nsm, axis=0, keepdims=True), 0.0)).astype(I32)
    run_ref[...] = run_ref[...] + nch


def _copy_units(lists):
    cnt_ref = lists[0]
    return cnt_ref[0, 0, 0] * (BIG_CHUNK // RUN_ALIGN) + cnt_ref[0, 0, 1]


def _start_copies(lists, make_copy):
    cnt_ref, blr_ref, bgr_ref, slr_ref, sgr_ref = lists

    def start(nrows, loc_ref, glob_ref):
        def body(j, c):
            make_copy(nrows, pl.multiple_of(loc_ref[0, 0, j] * RUN_ALIGN, RUN_ALIGN),
                      pl.multiple_of(glob_ref[0, 0, j] * RUN_ALIGN, RUN_ALIGN)).start()
            return c
        return body

    lax.fori_loop(0, cnt_ref[0, 0, 0], start(BIG_CHUNK, blr_ref, bgr_ref), 0)
    lax.fori_loop(0, cnt_ref[0, 0, 1], start(RUN_ALIGN, slr_ref, sgr_ref), 0)


def _wait_units(n, make_copy):
    per_wait = WAIT_CHUNK // RUN_ALIGN

    def wait_big(j, c):
        make_copy(WAIT_CHUNK, 0, 0).wait()
        return c

    def wait_small(j, c):
        make_copy(RUN_ALIGN, 0, 0).wait()
        return c

    nbig = lax.shift_right_logical(n, per_wait.bit_length() - 1)
    lax.fori_loop(0, nbig, wait_big, 0)
    lax.fori_loop(nbig * per_wait, n, wait_small, 0)


def _dispatch_kernel(last_ref, cnt_ref, blr_ref, bgr_ref, slr_ref, sgr_ref, h_ref, g_ref, post_ref, xs_ref, sbuf,
                     zbuf, pending, sem, zsem, *, n_exp, top_k, bm):
    i = pl.program_id(0)
    nt = pl.num_programs(0)
    slot = i % 2
    rows = sbuf.shape[1]

    @pl.when(i == 0)
    def _():
        zbuf[...] = jnp.zeros_like(zbuf)

        def zero_copy(e):
            return pltpu.make_async_copy(zbuf, xs_ref.at[pl.ds(last_ref[e] * bm, bm)], zsem)

        def start(e, c):
            @pl.when(last_ref[e] >= 0)
            def _():
                zero_copy(e).start()
            return c

        def wait(e, c):
            @pl.when(last_ref[e] >= 0)
            def _():
                zero_copy(e).wait()
            return c

        lax.fori_loop(0, n_exp, start, 0)
        lax.fori_loop(0, n_exp, wait, 0)

    u = _rms(h_ref[...], g_ref[...]).astype(BF16)
    tm = u.shape[0]
    post = post_ref[...]
    q = lax.broadcasted_iota(I32, (rows, tm), 0)
    perm = jnp.zeros((rows, tm), F32)
    for k in range(top_k):
        perm = jnp.where(q == post[k:k + 1, :], 1.0, perm)
    sbuf[slot] = _pack_pairs(jnp.dot(perm.astype(BF16), u, preferred_element_type=F32))

    def make_copy(s, nrows, src_row, dst_row):
        return pltpu.make_async_copy(sbuf.at[s, pl.ds(src_row, nrows)], xs_ref.at[pl.ds(dst_row, nrows)], sem.at[s])

    lists = (cnt_ref, blr_ref, bgr_ref, slr_ref, sgr_ref)
    _start_copies(lists, functools.partial(make_copy, slot))
    n = _copy_units(lists)

    @pl.when(i > 0)
    def _():
        _wait_units(pending[0], functools.partial(make_copy, 1 - slot))

    pending[0] = n

    @pl.when(i == nt - 1)
    def _():
        _wait_units(n, functools.partial(make_copy, slot))


def _ffn_kernel(bexp_ref, nxt_ref, sblk_ref, spair_ref, nused_ref, xa_ref, xb_ref, wgu_hbm, bgu_ref, wd_hbm, bd_ref,
                ys_ref, wgu_st, wd_st, wgu_bf, wd_bf, ybuf, pend, sem, ysem, *, f, layer):
    b = pl.program_id(0)
    last_step = pl.num_programs(0) - 1
    bm = xa_ref.shape[0]
    slot = b % 2
    prev = bexp_ref[jnp.maximum(b - 1, 0)]
    active = b < nused_ref[0]
    here = jnp.minimum(b, nused_ref[0] - 1)
    pair = spair_ref[here] == 1
    row0 = pl.multiple_of(sblk_ref[here] * bm, bm)

    def out_copy(s, nrows):
        return pltpu.make_async_copy(ybuf.at[s, pl.ds(0, nrows)], ys_ref.at[pl.ds(row0, nrows)], ysem.at[s])

    def drain(s):
        for nblk in (1, 2):
            @pl.when(pend[s] == nblk)
            def _():
                out_copy(s, nblk * bm).wait()
        pend[s] = 0

    @pl.when(b == 0)
    def _():
        pend[0] = 0
        pend[1] = 0

    def fetch(e):
        return (pltpu.make_async_copy(wgu_hbm.at[layer, e], wgu_st, sem.at[0]),
                pltpu.make_async_copy(wd_hbm.at[layer, e], wd_st, sem.at[1]))

    @pl.when(b == 0)
    def _():
        for c in fetch(bexp_ref[0]):
            c.start()

    @pl.when(active & ((b == 0) | (bexp_ref[b] != prev)))
    def _():
        for c in fetch(bexp_ref[b]):
            c.wait()
        wgu_bf[...] = wgu_st[...].astype(BF16)
        wd_bf[...] = wd_st[...].astype(BF16)

        @pl.when(nxt_ref[b] >= 0)
        def _():
            for c in fetch(nxt_ref[b]):
                c.start()

    def ffn(xw):
        x = _unpack_pairs(xw)
        gu = jnp.dot(x, wgu_bf[...], preferred_element_type=F32) + bgu_ref[0]
        gate = jnp.minimum(gu[:, :f], SWIGLU_LIMIT)
        up = jnp.clip(gu[:, f:], -SWIGLU_LIMIT, SWIGLU_LIMIT)
        glu = gate * _sigmoid(gate * SWIGLU_ALPHA)
        act = ((up + 1.0) * glu).astype(BF16)
        y = jnp.dot(act, wd_bf[...], preferred_element_type=F32) + bd_ref[0]
        return _pack_pairs(y.astype(BF16).astype(F32))

    @pl.when(active)
    def _():
        drain(slot)

    @pl.when(active & pair)
    def _():
        ybuf[slot] = ffn(jnp.concatenate([xa_ref[...], xb_ref[...]], axis=0))
        out_copy(slot, 2 * bm).start()
        pend[slot] = 2

    @pl.when(active & jnp.logical_not(pair))
    def _():
        ybuf[slot, 0:bm] = ffn(xa_ref[...])
        out_copy(slot, bm).start()
        pend[slot] = 1

    @pl.when(b == last_step)
    def _():
        drain(0)
        drain(1)


def _combine_kernel(*refs, top_k, final):
    lists, lists_nx = refs[0:5], refs[5:10]
    h_ref, gate_ref, pos_ref, ys_ref, fg_ref, o_ref, ybuf, sem = refs[10:]
    i = pl.program_id(0)
    nt = pl.num_programs(0)
    slot = i % 2
    rows = ybuf.shape[1]

    def make_copy(s, nrows, dst_row, src_row):
        return pltpu.make_async_copy(ys_ref.at[pl.ds(src_row, nrows)], ybuf.at[s, pl.ds(dst_row, nrows)], sem.at[s])

    @pl.when(i == 0)
    def _():
        ybuf[...] = jnp.zeros_like(ybuf)
        _start_copies(lists, functools.partial(make_copy, 0))

    @pl.when(i + 1 < nt)
    def _():
        _start_copies(lists_nx, functools.partial(make_copy, 1 - slot))

    _wait_units(_copy_units(lists), functools.partial(make_copy, slot))

    gates = gate_ref[...]
    pos = pos_ref[...]
    tm, d = h_ref.shape
    q = lax.broadcasted_iota(I32, (tm, rows), 1)
    wsel = jnp.zeros((tm, rows), F32)
    for k in range(top_k):
        wsel = jnp.where(q == pos[:, k:k + 1], gates[:, k:k + 1], wsel)
    out = h_ref[...] + jnp.dot(wsel.astype(BF16), _unpack_pairs(ybuf[slot]), preferred_element_type=F32)
    if final:
        out = _rms(out, fg_ref[...])
    o_ref[...] = out


def _moe_layer(h, g, w_router, b_router, w_gate_up, b_gate_up, w_down, b_down, final_g, final, layer):
    t, d = h.shape
    n_exp = w_router.shape[1]
    f = w_down.shape[2]
    top_k = TOP_K
    dp = d // 2
    bm = MOE_BLOCK
    ts = _pick(t, MOE_SORT_TILE)
    nt = t // ts
    srows = -(-(ts * top_k + n_exp * RUN_ALIGN) // LANES) * LANES
    n_blocks = -(-(t * top_k + nt * n_exp * (RUN_ALIGN - 1)) // bm) + n_exp
    n_slots = n_blocks * bm
    n_steps = (n_blocks + n_exp + 1) // 2
    nbp = -(-n_steps // SUBLANES) * SUBLANES

    tm = ts * max(1, _pick(nt, 2))
    wrt = w_router.T
    wr_hi = wrt.astype(BF16)
    wr_lo = (wrt - wr_hi.astype(F32)).astype(BF16)
    idxt, gatest, ncht = pl.pallas_call(
        functools.partial(_route_kernel, n_exp=n_exp, top_k=top_k, ts=ts),
        grid=(t // tm,),
        in_specs=[pl.BlockSpec((tm, d), lambda i: (i, 0)), _full((1, d)), _full((n_exp, d)),
                  _full((n_exp, d)), _full((n_exp, 1))],
        out_specs=[pl.BlockSpec((top_k, tm), lambda i: (0, i)), pl.BlockSpec((top_k, tm), lambda i: (0, i)),
                   pl.BlockSpec((tm // ts, n_exp, 1), lambda i: (i, 0, 0))],
        out_shape=[jax.ShapeDtypeStruct((top_k, t), I32), jax.ShapeDtypeStruct((top_k, t), F32),
                   jax.ShapeDtypeStruct((nt, n_exp, 1), I32)],
        compiler_params=_params(("arbitrary",)),
        name="moe_route",
    )(h, g.reshape(1, d), wr_hi, wr_lo, b_router.reshape(n_exp, 1))
    idx, gates, nch = idxt.T, gatest.T, ncht.reshape(nt, 1, n_exp)

    per_big = BIG_CHUNK // RUN_ALIGN
    n_big = srows // BIG_CHUNK
    n_small = n_exp * (per_big - 1)
    list_len = (n_big, n_big, n_small, n_small, SUBLANES)
    rank_out = pl.pallas_call(
        functools.partial(_rank_kernel, n_exp=n_exp, top_k=top_k, bm=bm, nbp=nbp),
        grid=(nt,),
        in_specs=[pl.BlockSpec((ts, top_k), lambda i: (i, 0)), pl.BlockSpec((top_k, ts), lambda i: (0, i)),
                  _full((nt, 1, n_exp))],
        out_specs=[pl.BlockSpec((ts, top_k), lambda i: (i, 0)), pl.BlockSpec((top_k, ts), lambda i: (0, i))]
        + [pl.BlockSpec((1, 1, r), lambda i: (i, 0, 0)) for r in list_len]
        + [_full((nbp, 1))] * 4 + [_full((SUBLANES, n_exp))],
        out_shape=[jax.ShapeDtypeStruct((t, top_k), I32), jax.ShapeDtypeStruct((top_k, t), I32)]
        + [jax.ShapeDtypeStruct((nt, 1, r), I32) for r in list_len]
        + [jax.ShapeDtypeStruct((nbp, 1), I32)] * 4 + [jax.ShapeDtypeStruct((SUBLANES, n_exp), I32)],
        scratch_shapes=[pltpu.VMEM((1, n_exp), F32), pltpu.VMEM((1, n_exp), F32)],
        compiler_params=_params(("arbitrary",)),
        name="moe_rank",
    )(idx, idxt, nch)
    pos, post = rank_out[0], rank_out[1]
    blr, bgr, slr, sgr, cnt = rank_out[2:7]
    bexp, nxt, sblk, spair = (a.reshape(nbp) for a in rank_out[7:11])
    meta = rank_out[11]
    lists = [cnt, blr, bgr, slr, sgr]
    n_used = meta[0, 0:1]
    last_blk = meta[1]

    def smem_lists(step=0):
        return [pl.BlockSpec((1, 1, a.shape[2]), lambda i, *_: (jnp.minimum(i + step, nt - 1), 0, 0),
                             memory_space=pltpu.SMEM) for a in lists]

    xs = pl.pallas_call(
        functools.partial(_dispatch_kernel, n_exp=n_exp, top_k=top_k, bm=bm),
        grid_spec=pltpu.PrefetchScalarGridSpec(
            num_scalar_prefetch=1,
            grid=(nt,),
            in_specs=smem_lists() + [
                pl.BlockSpec((ts, d), lambda i, last: (i, 0)),
                pl.BlockSpec((1, d), lambda i, last: (0, 0)),
                pl.BlockSpec((top_k, ts), lambda i, last: (0, i)),
            ],
            out_specs=pl.BlockSpec(memory_space=pl.ANY),
            scratch_shapes=[pltpu.VMEM((2, srows, dp), U32), pltpu.VMEM((bm, dp), U32), pltpu.SMEM((1,), I32),
                            pltpu.SemaphoreType.DMA((2,)), pltpu.SemaphoreType.DMA(())],
        ),
        out_shape=jax.ShapeDtypeStruct((n_slots, dp), U32),
        compiler_params=_params(("arbitrary",)),
        name="moe_dispatch",
    )(last_blk, *lists, h, g.reshape(1, d), post)

    def stp(s, nu):
        return jnp.minimum(s, nu[0] - 1)

    def blk_a(s, be, nx, sb, sp, nu):
        return (sb[stp(s, nu)], 0)

    def blk_b(s, be, nx, sb, sp, nu):
        return (sb[stp(s, nu)] + sp[stp(s, nu)], 0)

    def of_expert(shape):
        return pl.BlockSpec(shape, lambda s, be, nx, sb, sp, nu: (be[stp(s, nu)],) + (0,) * (len(shape) - 1))

    ys = pl.pallas_call(
        functools.partial(_ffn_kernel, f=f, layer=layer),
        grid_spec=pltpu.PrefetchScalarGridSpec(
            num_scalar_prefetch=5,
            grid=(n_steps,),
            in_specs=[
                pl.BlockSpec((bm, dp), blk_a), pl.BlockSpec((bm, dp), blk_b),
                pl.BlockSpec(memory_space=pl.ANY), of_expert((1, 1, 2 * f)),
                pl.BlockSpec(memory_space=pl.ANY), of_expert((1, 1, d)),
            ],
            out_specs=pl.BlockSpec(memory_space=pl.ANY),
            scratch_shapes=[pltpu.VMEM((d, 2 * f), F32), pltpu.VMEM((f, d), F32),
                            pltpu.VMEM((d, 2 * f), BF16), pltpu.VMEM((f, d), BF16),
                            pltpu.VMEM((2, 2 * bm, dp), U32), pltpu.SMEM((2,), I32),
                            pltpu.SemaphoreType.DMA((2,)), pltpu.SemaphoreType.DMA((2,))],
        ),
        out_shape=jax.ShapeDtypeStruct((n_slots, dp), U32),
        compiler_params=_params(("arbitrary",)),
        name="moe_ffn",
    )(bexp, nxt, sblk, spair, n_used, xs, xs, w_gate_up, b_gate_up.reshape(n_exp, 1, 2 * f), w_down,
      b_down.reshape(n_exp, 1, d))

    return pl.pallas_call(
        functools.partial(_combine_kernel, top_k=top_k, final=final),
        grid=(nt,),
        in_specs=smem_lists() + smem_lists(step=1) + [
            pl.BlockSpec((ts, d), lambda i: (i, 0)),
            pl.BlockSpec((ts, top_k), lambda i: (i, 0)),
            pl.BlockSpec((ts, top_k), lambda i: (i, 0)),
            pl.BlockSpec(memory_space=pl.ANY),
            _full((1, d)),
        ],
        out_specs=pl.BlockSpec((ts, d), lambda i: (i, 0)),
        out_shape=jax.ShapeDtypeStruct((t, d), F32),
        scratch_shapes=[pltpu.VMEM((2, srows, dp), U32), pltpu.SemaphoreType.DMA((2,))],
        compiler_params=_params(("arbitrary",)),
        name="moe_combine",
    )(*lists, *lists, h, gates, pos, ys, final_g.reshape(1, d))


def kernel(x, mix_norm, ffn_norm, final_norm, lru_w_in, lru_conv_w, lru_conv_b, lru_w_a, lru_b_a, lru_w_x, lru_b_x, lru_a_param, lru_w_out, pool_w_in, pool_w_group, pool_scale, pool_w_out, sb_w_qkv, sb_w_out, moe_w_router, moe_b_router, moe_w_gate_up, moe_b_gate_up, moe_w_down, moe_b_down):
    batch, seq, d = x.shape
    depth = mix_norm.shape[0]
    h = x.reshape(batch * seq, d)
    for layer in range(depth):
        kind = layer % N_MIXERS
        slot = layer // N_MIXERS
        if kind == 0:
            h = _lru_layer(h, batch, seq, mix_norm[layer], lru_w_in[slot], lru_conv_w[slot], lru_conv_b[slot],
                           lru_w_a[slot], lru_b_a[slot], lru_w_x[slot], lru_b_x[slot], lru_a_param[slot],
                           lru_w_out[slot])
        elif kind == 1:
            h = _pool_layer(h, batch, seq, mix_norm[layer], pool_w_in[slot], pool_w_group[slot], pool_scale[slot],
                            pool_w_out[slot])
        else:
            h = _sb_layer(h, batch, seq, mix_norm[layer], sb_w_qkv[slot], sb_w_out[slot])
        h = _moe_layer(h, ffn_norm[layer], moe_w_router[layer], moe_b_router[layer], moe_w_gate_up,
                       moe_b_gate_up[layer], moe_w_down, moe_b_down[layer], final_norm,
                       final=(layer == depth - 1), layer=layer)
    return h.reshape(batch, seq, d)
```

```python
import functools
import math

import jax
import jax.numpy as jnp
from jax import lax
from jax.experimental import pallas as pl
from jax.experimental.pallas import tpu as pltpu

F32 = jnp.float32
BF16 = jnp.bfloat16
I32 = jnp.int32
U32 = jnp.uint32

RMS_EPS = 1e-6
LRU_C = 8.0
POOL_WINDOWS = (2, 4, 8, 16)
SB_HEADS = 16
LOG2E = 1.4426950408889634
TOP_K = 4
SWIGLU_LIMIT = 7.0
SWIGLU_ALPHA = 1.702
N_MIXERS = 3

V7X_VMEM_BYTES = 64 * 1024 * 1024
VMEM_LIMIT_BYTES = V7X_VMEM_BYTES - 8 * 1024 * 1024
SUBLANES = 8
LANES = 128

MOE_BLOCK = 256
MOE_SORT_TILE = 256
RUN_ALIGN = SUBLANES
BIG_CHUNK = 4 * RUN_ALIGN
WAIT_CHUNK = 16 * RUN_ALIGN


def _params(semantics, vmem=VMEM_LIMIT_BYTES):
    return pltpu.CompilerParams(dimension_semantics=semantics, vmem_limit_bytes=vmem)


def _pick(n, pref):
    t = min(n, pref)
    while n % t:
        t //= 2
    return t


def _rms(h, g):
    ms = jnp.mean(h * h, axis=-1, keepdims=True)
    return h * lax.rsqrt(ms + RMS_EPS) * g


def _sigmoid(x):
    return 1.0 / (1.0 + jnp.exp(-x))


def _softplus(x):
    return jnp.maximum(x, 0.0) + jnp.log(1.0 + jnp.exp(-jnp.abs(x)))


def _gelu_tanh(x):
    c = math.sqrt(2.0 / math.pi)
    return 0.5 * x * (1.0 + jnp.tanh(c * (x + 0.044715 * (x * x * x))))


def _full(shape):
    n = len(shape)
    return pl.BlockSpec(shape, lambda *_: (0,) * n)


def _pack_pairs(x):
    n = x.shape[1] // 2
    lo = lax.bitcast_convert_type(x[:, :n], U32)
    hi = lax.bitcast_convert_type(x[:, n:], U32)
    return lax.shift_right_logical(lo, jnp.uint32(16)) | (hi & jnp.uint32(0xFFFF0000))


def _unpack_pairs(w):
    lo = lax.bitcast_convert_type(lax.shift_left(w, jnp.uint32(16)), F32)
    hi = lax.bitcast_convert_type(w & jnp.uint32(0xFFFF0000), F32)
    return jnp.concatenate([lo.astype(BF16), hi.astype(BF16)], axis=1)


def _lru_kernel(h_ref, g_ref, win_ref, cw_ref, cb_ref, wax_ref, ba_ref, bx_ref, ap_ref, wout_ref,
                o_ref, hcar_ref, ext_ref, y_ref, *, ts, w, bw):
    s = pl.program_id(1)

    @pl.when(s == 0)
    def _():
        hcar_ref[...] = jnp.zeros_like(hcar_ref)
        ext_ref[0:SUBLANES, :] = jnp.zeros((SUBLANES, w), F32)

    h = h_ref[...]
    u = _rms(h, g_ref[...]).astype(BF16)
    proj = jnp.dot(u, win_ref[...], preferred_element_type=F32)
    row8 = lax.broadcasted_iota(I32, (SUBLANES, bw), 0)
    kw = cw_ref.shape[0]

    for c in range(w // bw):
        lo, hi = c * bw, (c + 1) * bw
        gate = proj[:, lo:hi]
        xb = proj[:, w + lo:w + hi]
        ext_ref[SUBLANES:SUBLANES + ts, lo:hi] = xb
        xc = cb_ref[:, lo:hi] + xb * cw_ref[kw - 1:kw, lo:hi]
        for j in range(1, kw):
            xc = xc + ext_ref[SUBLANES - j:SUBLANES - j + ts, lo:hi] * cw_ref[kw - 1 - j:kw - j, lo:hi]
        ext_ref[0:SUBLANES, lo:hi] = xb[ts - SUBLANES:ts]

        res = jnp.dot(xc.astype(BF16), wax_ref[c], preferred_element_type=F32)
        r = _sigmoid(res[:, :bw] + ba_ref[:, lo:hi])
        i = _sigmoid(res[:, bw:] + bx_ref[:, lo:hi])
        log_a = (-LRU_C * _softplus(-ap_ref[:, lo:hi])) * r
        a = jnp.exp(log_a)
        b = jnp.sqrt(-jnp.tanh(log_a) * (a * a + 1.0)) * (i * xc)
        hprev = hcar_ref[:, lo:hi]
        groups = []
        for gi in range(ts // SUBLANES):
            ag = a[gi * SUBLANES:(gi + 1) * SUBLANES]
            bg = b[gi * SUBLANES:(gi + 1) * SUBLANES]
            d = 1
            while d < SUBLANES:
                keep = row8 >= d
                a_sh = jnp.where(keep, pltpu.roll(ag, d, axis=0), 1.0)
                b_sh = jnp.where(keep, pltpu.roll(bg, d, axis=0), 0.0)
                bg = ag * b_sh + bg
                ag = ag * a_sh
                d *= 2
            hg = ag * hprev + bg
            hprev = hg[SUBLANES - 1:SUBLANES]
            groups.append(hg)
        hs = jnp.concatenate(groups, axis=0)
        hcar_ref[:, lo:hi] = hprev
        y_ref[:, lo:hi] = (hs * _gelu_tanh(gate)).astype(BF16)

    o_ref[...] = h + jnp.dot(y_ref[...], wout_ref[...], preferred_element_type=F32)


def _lru_layer(h, batch, seq, g, w_in, conv_w, conv_b, w_a, b_a, w_x, b_x, a_param, w_out):
    t, d = h.shape
    w = w_in.shape[1] // 2
    nblk, bw, _ = w_a.shape
    ts = _pick(seq, 512)
    ns = seq // ts
    wax = jnp.concatenate([w_a, w_x], axis=-1).astype(BF16)
    row = lambda v: v.reshape(1, -1)
    kern = functools.partial(_lru_kernel, ts=ts, w=w, bw=bw)
    return pl.pallas_call(
        kern,
        grid=(batch, ns),
        in_specs=[
            pl.BlockSpec((ts, d), lambda b, s: (b * ns + s, 0)),
            _full((1, d)), _full((d, 2 * w)), _full(conv_w.shape), _full((1, w)),
            _full(wax.shape), _full((1, w)), _full((1, w)), _full((1, w)), _full((w, d)),
        ],
        out_specs=pl.BlockSpec((ts, d), lambda b, s: (b * ns + s, 0)),
        out_shape=jax.ShapeDtypeStruct((t, d), F32),
        scratch_shapes=[pltpu.VMEM((1, w), F32), pltpu.VMEM((SUBLANES + ts, w), F32), pltpu.VMEM((ts, w), BF16)],
        compiler_params=_params(("arbitrary", "arbitrary")),
        name="lru_mixer",
    )(h, row(g), w_in.astype(BF16), conv_w, row(conv_b), wax, row(b_a), row(b_x), row(a_param),
      w_out.astype(BF16))


def _pool_kernel(h_ref, g_ref, win_ref, wg_ref, sc_ref, wout_ref, o_ref, ext_ref, m_ref, *, ts, d, wins):
    s = pl.program_id(1)
    pad = wins[-1]
    gw = d // len(wins)

    @pl.when(s == 0)
    def _():
        ext_ref[0:pad, :] = jnp.zeros((pad, d), F32)

    h = h_ref[...]
    u = _rms(h, g_ref[...]).astype(BF16)
    v = jnp.dot(u, win_ref[...], preferred_element_type=F32)
    ext_ref[pad:pad + ts, :] = v
    n_avail = (s * ts + lax.broadcasted_iota(I32, (ts, 1), 0) + 1).astype(F32)

    cur = ext_ref[...]
    width = 1
    for gi, win in enumerate(wins):
        while width < win:
            cur = cur + pltpu.roll(cur, width, axis=0)
            width *= 2
        lo, hi = gi * gw, (gi + 1) * gw
        mean = cur[pad:, 0:gw] / jnp.minimum(n_avail, float(win))
        pooled = mean - v[:, lo:hi]
        mixed = jnp.dot(pooled.astype(BF16), wg_ref[gi], preferred_element_type=F32)
        m_ref[:, lo:hi] = (mixed * sc_ref[:, lo:hi]).astype(BF16)
        cur = cur[:, gw:]
    ext_ref[0:pad, :] = v[ts - pad:ts]
    o_ref[...] = h + jnp.dot(m_ref[...], wout_ref[...], preferred_element_type=F32)


def _pool_layer(h, batch, seq, g, w_in, w_group, scale, w_out):
    t, d = h.shape
    ts = _pick(seq, 256)
    ns = seq // ts
    pad = POOL_WINDOWS[-1]
    kern = functools.partial(_pool_kernel, ts=ts, d=d, wins=POOL_WINDOWS)
    return pl.pallas_call(
        kern,
        grid=(batch, ns),
        in_specs=[
            pl.BlockSpec((ts, d), lambda b, s: (b * ns + s, 0)),
            _full((1, d)), _full((d, d)), _full(w_group.shape), _full((1, d)), _full((d, d)),
        ],
        out_specs=pl.BlockSpec((ts, d), lambda b, s: (b * ns + s, 0)),
        out_shape=jax.ShapeDtypeStruct((t, d), F32),
        scratch_shapes=[pltpu.VMEM((ts + pad, d), F32), pltpu.VMEM((ts, d), BF16)],
        compiler_params=_params(("arbitrary", "arbitrary")),
        name="pool_mixer",
    )(h, g.reshape(1, d), w_in.astype(BF16), w_group.astype(BF16), scale.reshape(1, d), w_out.astype(BF16))


def _sb_proj_kernel(h_ref, g_ref, wqt_ref, wk_ref, wvt_ref, qt_ref, k_ref, vt_ref):
    u = _rms(h_ref[...], g_ref[...]).astype(BF16)
    nt = (((1,), (1,)), ((), ()))
    qt_ref[...] = lax.dot_general(wqt_ref[...], u, nt, preferred_element_type=F32).astype(BF16)
    k_ref[...] = jnp.dot(u, wk_ref[...], preferred_element_type=F32).astype(BF16)
    vt_ref[...] = lax.dot_general(wvt_ref[...], u, nt, preferred_element_type=F32).astype(BF16)


def _sb_attn_kernel(qt_ref, k_ref, vt_ref, h_ref, wout_ref, o_ref, lb_ref, zs_ref, a_ref, *, bq, dh, nh):
    qb = pl.program_id(1)
    krow = lax.broadcasted_iota(I32, (bq, bq), 0)
    qcol = lax.broadcasted_iota(I32, (bq, bq), 1)
    upper = (qcol >= krow).astype(BF16)
    causal = krow < qcol

    def step(chains, masked, st):
        for c, (hh, j) in enumerate(chains):
            qh = qt_ref[hh * dh:(hh + 1) * dh, :]
            kh = k_ref[pl.ds(j * bq, bq), hh * dh:(hh + 1) * dh]
            z = jnp.dot(kh, qh, preferred_element_type=F32)
            t = jnp.exp2(jnp.abs(z) * (-LOG2E))
            l1m = jnp.minimum(-z, 0.0) - jnp.log(1.0 + t)
            zs_ref[c] = z
            lm = jnp.where(causal, l1m, 0.0) if masked else l1m
            lb_ref[c] = lm.astype(BF16)
        st = list(st)
        for c, (hh, j) in enumerate(chains):
            acc, carry = st[hh]
            suffix = jnp.dot(upper, lb_ref[c], preferred_element_type=F32)
            a = jnp.exp(zs_ref[c] + suffix + carry)
            if masked:
                a = jnp.where(causal, a, 0.0)
            a_ref[c] = a.astype(BF16)
            st[hh] = (acc, carry + suffix[0:1, :])
        for c, (hh, j) in enumerate(chains):
            acc, carry = st[hh]
            vth = vt_ref[hh * dh:(hh + 1) * dh, pl.ds(j * bq, bq)]
            st[hh] = (acc + jnp.dot(vth, a_ref[c], preferred_element_type=F32), carry)
        return st

    heads = range(nh)
    zero = (jnp.zeros((dh, bq), F32), jnp.zeros((1, bq), F32))
    st = step([(hh, qb) for hh in heads], True, [zero] * nh)

    def pair_body(i, st):
        j = qb - 1 - 2 * i
        return step([(hh, j) for hh in heads] + [(hh, j - 1) for hh in heads], False, st)

    def single_body(i, st):
        return step([(hh, 0) for hh in heads], False, st)

    st = lax.fori_loop(0, lax.shift_right_logical(qb, 1), pair_body, st)
    st = lax.fori_loop(0, qb & 1, single_body, st)
    o = jnp.concatenate([st[hh][0].T for hh in heads], axis=1).astype(BF16)
    o_ref[...] = h_ref[...] + jnp.dot(o, wout_ref[...], preferred_element_type=F32)


def _sb_layer(h, batch, seq, g, w_qkv, w_out):
    t, d = h.shape
    dh = d // SB_HEADS
    scale = 1.0 / math.sqrt(dh)
    wqt = (w_qkv[:, :d] * scale).T.astype(BF16)
    wk = w_qkv[:, d:2 * d].astype(BF16)
    wvt = w_qkv[:, 2 * d:].T.astype(BF16)
    tp = _pick(t, 512)
    qt, k, vt = pl.pallas_call(
        _sb_proj_kernel,
        grid=(t // tp,),
        in_specs=[pl.BlockSpec((tp, d), lambda i: (i, 0)), _full((1, d)), _full((d, d)), _full((d, d)),
                  _full((d, d))],
        out_specs=[pl.BlockSpec((d, tp), lambda i: (0, i)), pl.BlockSpec((tp, d), lambda i: (i, 0)),
                   pl.BlockSpec((d, tp), lambda i: (0, i))],
        out_shape=[jax.ShapeDtypeStruct((d, t), BF16), jax.ShapeDtypeStruct((t, d), BF16),
                   jax.ShapeDtypeStruct((d, t), BF16)],
        compiler_params=_params(("arbitrary",)),
        name="sb_qkv_proj",
    )(h, g.reshape(1, d), wqt, wk, wvt)

    bq = _pick(seq, 256)
    nq = seq // bq
    nh = SB_HEADS
    nchain = 2 * nh
    return pl.pallas_call(
        functools.partial(_sb_attn_kernel, bq=bq, dh=dh, nh=nh),
        grid=(batch, nq),
        in_specs=[
            pl.BlockSpec((d, bq), lambda b, q: (0, b * nq + q)),
            pl.BlockSpec((seq, d), lambda b, q: (b, 0)),
            pl.BlockSpec((d, seq), lambda b, q: (0, b)),
            pl.BlockSpec((bq, d), lambda b, q: (b * nq + q, 0)),
            _full((d, d)),
        ],
        out_specs=pl.BlockSpec((bq, d), lambda b, q: (b * nq + q, 0)),
        out_shape=jax.ShapeDtypeStruct((t, d), F32),
        scratch_shapes=[pltpu.VMEM((nchain, bq, bq), BF16), pltpu.VMEM((nchain, bq, bq), F32),
                        pltpu.VMEM((nchain, bq, bq), BF16)],
        compiler_params=_params(("arbitrary", "arbitrary")),
        name="sb_attention",
    )(qt, k, vt, h, w_out.astype(BF16))


def _route_kernel(h_ref, g_ref, wr_hi_ref, wr_lo_ref, br_ref, idxt_ref, gatet_ref, nch_ref, *, n_exp, top_k, ts):
    u = _rms(h_ref[...], g_ref[...])
    u_hi = u.astype(BF16)
    u_lo = (u - u_hi.astype(F32)).astype(BF16)
    nt_dims = (((1,), (1,)), ((), ()))
    logits = (lax.dot_general(wr_hi_ref[...], u_hi, nt_dims, preferred_element_type=F32)
              + lax.dot_general(wr_lo_ref[...], u_hi, nt_dims, preferred_element_type=F32)
              + lax.dot_general(wr_hi_ref[...], u_lo, nt_dims, preferred_element_type=F32)) + br_ref[...]
    tm = logits.shape[1]
    sub = lax.broadcasted_iota(I32, (n_exp, tm), 0)
    krow = lax.broadcasted_iota(I32, (top_k, tm), 0)
    vals = logits
    top_v, top_i = [], []
    for _ in range(top_k):
        m = jnp.max(vals, axis=0, keepdims=True)
        sel = jnp.min(jnp.where(vals == m, sub, n_exp), axis=0, keepdims=True)
        top_v.append(m)
        top_i.append(sel)
        vals = jnp.where(sub == sel, -jnp.inf, vals)
    exps = [jnp.exp(v - top_v[0]) for v in top_v]
    denom = exps[0]
    for e in exps[1:]:
        denom = denom + e
    idx = jnp.zeros((top_k, tm), I32)
    gates = jnp.zeros((top_k, tm), F32)
    for k in range(top_k):
        idx = jnp.where(krow == k, top_i[k], idx)
        gates = jnp.where(krow == k, exps[k] / denom, gates)
    idxt_ref[...] = idx
    gatet_ref[...] = gates
    chosen = jnp.where(sub == top_i[0], 1.0, 0.0)
    for k in range(1, top_k):
        chosen = jnp.where(sub == top_i[k], 1.0, chosen)
    for s in range(tm // ts):
        cnt = jnp.sum(chosen[:, s * ts:(s + 1) * ts], axis=1, keepdims=True)
        nch_ref[s] = jnp.floor((cnt + (RUN_ALIGN - 1)) * (1.0 / RUN_ALIGN)).astype(I32)


def _rank_kernel(idx_ref, idxt_ref, nchall_ref, pos_ref, post_ref, blr_ref, bgr_ref, slr_ref, sgr_ref, cnt_ref,
                 bexp_ref, nxt_ref, sblk_ref, spair_ref, meta_ref, run_ref, pstart_ref, *, n_exp, top_k, bm, nbp):
    i = pl.program_id(0)
    tm = idx_ref.shape[0]
    cpb = bm // RUN_ALIGN
    idx = idx_ref[...]
    lane = lax.broadcasted_iota(I32, (tm, n_exp), 1)
    ohs = [(idx[:, k:k + 1] == lane).astype(F32) for k in range(top_k)]
    oh = ohs[0]
    for o in ohs[1:]:
        oh = oh + o
    cnt = jnp.sum(oh, axis=0, keepdims=True)
    nch = jnp.floor((cnt + (RUN_ALIGN - 1)) * (1.0 / RUN_ALIGN))
    er = lax.broadcasted_iota(I32, (n_exp, n_exp), 0)
    ec = lax.broadcasted_iota(I32, (n_exp, n_exp), 1)

    def excl_cumsum_row(v):
        v8 = jnp.broadcast_to(v, (SUBLANES, n_exp)).astype(BF16)
        return jnp.dot(v8, (er < ec).astype(BF16), preferred_element_type=F32)[0:1, :]

    @pl.when(i == 0)
    def _():
        tot = jnp.sum(nchall_ref[...].astype(F32), axis=0)
        nb = jnp.floor((tot + (cpb - 1)) * (1.0 / cpb))
        bstart = excl_cumsum_row(nb)
        bend = bstart + nb
        pstart_ref[...] = bstart * cpb
        nst = jnp.floor((nb + 1.0) * 0.5)
        sstart = excl_cumsum_row(nst)
        send = sstart + nst
        step = lax.broadcasted_iota(I32, (nbp, n_exp), 0).astype(F32)
        se = jnp.sum((send <= step).astype(F32), axis=-1, keepdims=True)
        se = jnp.minimum(se, n_exp - 1)
        bexp_ref[...] = se.astype(I32)
        mine = lax.broadcasted_iota(I32, (nbp, n_exp), 1).astype(F32) == se
        pick = lambda v: jnp.sum(jnp.where(mine, v, 0.0), axis=-1, keepdims=True)
        local = step[:, 0:1] - pick(sstart)
        sblk_ref[...] = (pick(bstart) + 2.0 * local).astype(I32)
        spair_ref[...] = jnp.where(2.0 * local + 1.0 < pick(nb), 1.0, 0.0).astype(I32)
        n_used = jnp.sum(nst, axis=-1, keepdims=True)
        run_end = pick(send)
        nx = jnp.minimum(jnp.sum((send <= run_end).astype(F32), axis=-1, keepdims=True), n_exp - 1)
        nxt_ref[...] = jnp.where(run_end < n_used, nx, -1.0).astype(I32)
        last = jnp.where(nb > 0, bend - 1.0, -1.0)
        mrow = lax.broadcasted_iota(I32, (SUBLANES, n_exp), 0)
        meta = jnp.where(mrow == 0, jnp.broadcast_to(n_used, (SUBLANES, n_exp)),
                         jnp.where(mrow == 1, jnp.broadcast_to(last, (SUBLANES, n_exp)), 0.0))
        meta_ref[...] = meta.astype(I32)
        run_ref[...] = jnp.zeros_like(run_ref)

    tr = lax.broadcasted_iota(I32, (tm, tm), 0)
    tc = lax.broadcasted_iota(I32, (tm, tm), 1)
    off = excl_cumsum_row(nch)
    excl = jnp.dot((tc < tr).astype(BF16), oh.astype(BF16), preferred_element_type=F32)
    base = excl + off * RUN_ALIGN
    kcol = lax.broadcasted_iota(I32, (tm, top_k), 1)
    pos = jnp.zeros((tm, top_k), I32)
    for k in range(top_k):
        pk = jnp.sum(ohs[k] * base, axis=-1, keepdims=True).astype(I32)
        pos = jnp.where(kcol == k, pk, pos)
    pos_ref[...] = pos
    idxt = idxt_ref[...]
    sub = lax.broadcasted_iota(I32, (n_exp, tm), 0)
    ohts = [(idxt[k:k + 1, :] == sub).astype(F32) for k in range(top_k)]
    oht = ohts[0]
    for o in ohts[1:]:
        oht = oht + o
    cnt_col = jnp.sum(oht, axis=1, keepdims=True)
    nch_col = jnp.floor((cnt_col + (RUN_ALIGN - 1)) * (1.0 / RUN_ALIGN))
    off_col = jnp.dot((ec < er).astype(BF16), jnp.broadcast_to(nch_col, (n_exp, LANES)).astype(BF16),
                      preferred_element_type=F32)[:, 0:1]
    exclt = jnp.dot(oht.astype(BF16), (tr < tc).astype(BF16), preferred_element_type=F32)
    baset = exclt + off_col * RUN_ALIGN
    krow = lax.broadcasted_iota(I32, (top_k, tm), 0)
    post = jnp.zeros((top_k, tm), I32)
    for k in range(top_k):
        pk = jnp.sum(ohts[k] * baset, axis=0, keepdims=True).astype(I32)
        post = jnp.where(krow == k, pk, post)
    post_ref[...] = post

    gst_row = pstart_ref[...] + run_ref[...]
    gst_col = jnp.sum(jnp.where(er == ec, jnp.broadcast_to(gst_row, (n_exp, n_exp)), 0.0), axis=1, keepdims=True)
    per_big = BIG_CHUNK // RUN_ALIGN
    nbig = jnp.floor(nch_col * (1.0 / per_big))
    nsm = nch_col - nbig * per_big

    def excl_cumsum_col(v):
        return jnp.dot((ec < er).astype(BF16), jnp.broadcast_to(v, (n_exp, LANES)).astype(BF16),
                       preferred_element_type=F32)[:, 0:1]

    def copy_list(count, loc0, glob0, stride, n):
        before = excl_cumsum_col(count)
        j = lax.broadcasted_iota(I32, (n_exp, n), 1).astype(F32)
        owner = jnp.sum(((before + count) <= j).astype(F32), axis=0, keepdims=True)
        mine = lax.broadcasted_iota(I32, (n_exp, n), 0).astype(F32) == owner
        pick = lambda v: jnp.sum(jnp.where(mine, v, 0.0), axis=0, keepdims=True)
        piece = (j[0:1, :] - pick(before)) * stride
        return (pick(loc0) + piece).astype(I32), (pick(glob0) + piece).astype(I32)

    blr_ref[0], bgr_ref[0] = copy_list(nbig, off_col, gst_col, per_big, blr_ref.shape[2])
    slr_ref[0], sgr_ref[0] = copy_list(nsm, off_col + nbig * per_big, gst_col + nbig * per_big, 1, slr_ref.shape[2])
    clane = lax.broadcasted_iota(I32, (1, cnt_ref.shape[2]), 1)
    cnt_ref[0] = jnp.where(clane == 0, jnp.sum(nbig, axis=0, keepdims=True),
                           jnp.where(clane == 1, jnp.sum(nsm, axis=0, keepdims=True), 0.0)).astype(I32)
    run_ref[...] = run_ref[...] + nch


def _copy_units(lists):
    cnt_ref = lists[0]
    return cnt_ref[0, 0, 0] * (BIG_CHUNK // RUN_ALIGN) + cnt_ref[0, 0, 1]


def _start_copies(lists, make_copy):
    cnt_ref, blr_ref, bgr_ref, slr_ref, sgr_ref = lists

    def start(nrows, loc_ref, glob_ref):
        def body(j, c):
            make_copy(nrows, pl.multiple_of(loc_ref[0, 0, j] * RUN_ALIGN, RUN_ALIGN),
                      pl.multiple_of(glob_ref[0, 0, j] * RUN_ALIGN, RUN_ALIGN)).start()
            return c
        return body

    lax.fori_loop(0, cnt_ref[0, 0, 0], start(BIG_CHUNK, blr_ref, bgr_ref), 0)
    lax.fori_loop(0, cnt_ref[0, 0, 1], start(RUN_ALIGN, slr_ref, sgr_ref), 0)


def _wait_units(n, make_copy):
    per_wait = WAIT_CHUNK // RUN_ALIGN

    def wait_big(j, c):
        make_copy(WAIT_CHUNK, 0, 0).wait()
        return c

    def wait_small(j, c):
        make_copy(RUN_ALIGN, 0, 0).wait()
        return c

    nbig = lax.shift_right_logical(n, per_wait.bit_length() - 1)
    lax.fori_loop(0, nbig, wait_big, 0)
    lax.fori_loop(nbig * per_wait, n, wait_small, 0)


def _dispatch_kernel(last_ref, cnt_ref, blr_ref, bgr_ref, slr_ref, sgr_ref, h_ref, g_ref, post_ref, xs_ref, sbuf,
                     zbuf, pending, sem, zsem, *, n_exp, top_k, bm):
    i = pl.program_id(0)
    nt = pl.num_programs(0)
    slot = i % 2
    rows = sbuf.shape[1]

    @pl.when(i == 0)
    def _():
        zbuf[...] = jnp.zeros_like(zbuf)

        def zero_copy(e):
            return pltpu.make_async_copy(zbuf, xs_ref.at[pl.ds(last_ref[e] * bm, bm)], zsem)

        def start(e, c):
            @pl.when(last_ref[e] >= 0)
            def _():
                zero_copy(e).start()
            return c

        def wait(e, c):
            @pl.when(last_ref[e] >= 0)
            def _():
                zero_copy(e).wait()
            return c

        lax.fori_loop(0, n_exp, start, 0)
        lax.fori_loop(0, n_exp, wait, 0)

    u = _rms(h_ref[...], g_ref[...]).astype(BF16)
    tm = u.shape[0]
    post = post_ref[...]
    q = lax.broadcasted_iota(I32, (rows, tm), 0)
    perm = jnp.zeros((rows, tm), F32)
    for k in range(top_k):
        perm = jnp.where(q == post[k:k + 1, :], 1.0, perm)
    sbuf[slot] = _pack_pairs(jnp.dot(perm.astype(BF16), u, preferred_element_type=F32))

    def make_copy(s, nrows, src_row, dst_row):
        return pltpu.make_async_copy(sbuf.at[s, pl.ds(src_row, nrows)], xs_ref.at[pl.ds(dst_row, nrows)], sem.at[s])

    lists = (cnt_ref, blr_ref, bgr_ref, slr_ref, sgr_ref)
    _start_copies(lists, functools.partial(make_copy, slot))
    n = _copy_units(lists)

    @pl.when(i > 0)
    def _():
        _wait_units(pending[0], functools.partial(make_copy, 1 - slot))

    pending[0] = n

    @pl.when(i == nt - 1)
    def _():
        _wait_units(n, functools.partial(make_copy, slot))


def _ffn_kernel(bexp_ref, nxt_ref, sblk_ref, spair_ref, nused_ref, xa_ref, xb_ref, wgu_hbm, bgu_ref, wd_hbm, bd_ref,
                ys_ref, wgu_st, wd_st, wgu_bf, wd_bf, ybuf, pend, sem, ysem, *, f, layer):
    b = pl.program_id(0)
    last_step = pl.num_programs(0) - 1
    bm = xa_ref.shape[0]
    slot = b % 2
    prev = bexp_ref[jnp.maximum(b - 1, 0)]
    active = b < nused_ref[0]
    here = jnp.minimum(b, nused_ref[0] - 1)
    pair = spair_ref[here] == 1
    row0 = pl.multiple_of(sblk_ref[here] * bm, bm)

    def out_copy(s, nrows):
        return pltpu.make_async_copy(ybuf.at[s, pl.ds(0, nrows)], ys_ref.at[pl.ds(row0, nrows)], ysem.at[s])

    def drain(s):
        for nblk in (1, 2):
            @pl.when(pend[s] == nblk)
            def _():
                out_copy(s, nblk * bm).wait()
        pend[s] = 0

    @pl.when(b == 0)
    def _():
        pend[0] = 0
        pend[1] = 0

    def fetch(e):
        return (pltpu.make_async_copy(wgu_hbm.at[layer, e], wgu_st, sem.at[0]),
                pltpu.make_async_copy(wd_hbm.at[layer, e], wd_st, sem.at[1]))

    @pl.when(b == 0)
    def _():
        for c in fetch(bexp_ref[0]):
            c.start()

    @pl.when(active & ((b == 0) | (bexp_ref[b] != prev)))
    def _():
        for c in fetch(bexp_ref[b]):
            c.wait()
        wgu_bf[...] = wgu_st[...].astype(BF16)
        wd_bf[...] = wd_st[...].astype(BF16)

        @pl.when(nxt_ref[b] >= 0)
        def _():
            for c in fetch(nxt_ref[b]):
                c.start()

    def ffn(xw):
        x = _unpack_pairs(xw)
        gu = jnp.dot(x, wgu_bf[...], preferred_element_type=F32) + bgu_ref[0]
        gate = jnp.minimum(gu[:, :f], SWIGLU_LIMIT)
        up = jnp.clip(gu[:, f:], -SWIGLU_LIMIT, SWIGLU_LIMIT)
        glu = gate * _sigmoid(gate * SWIGLU_ALPHA)
        act = ((up + 1.0) * glu).astype(BF16)
        y = jnp.dot(act, wd_bf[...], preferred_element_type=F32) + bd_ref[0]
        return _pack_pairs(y.astype(BF16).astype(F32))

    @pl.when(active)
    def _():
        drain(slot)

    @pl.when(active & pair)
    def _():
        ybuf[slot] = ffn(jnp.concatenate([xa_ref[...], xb_ref[...]], axis=0))
        out_copy(slot, 2 * bm).start()
        pend[slot] = 2

    @pl.when(active & jnp.logical_not(pair))
    def _():
        ybuf[slot, 0:bm] = ffn(xa_ref[...])
        out_copy(slot, bm).start()
        pend[slot] = 1

    @pl.when(b == last_step)
    def _():
        drain(0)
        drain(1)


def _combine_kernel(*refs, top_k, final):
    lists, lists_nx = refs[0:5], refs[5:10]
    h_ref, gate_ref, pos_ref, ys_ref, fg_ref, o_ref, ybuf, sem = refs[10:]
    i = pl.program_id(0)
    nt = pl.num_programs(0)
    slot = i % 2
    rows = ybuf.shape[1]

    def make_copy(s, nrows, dst_row, src_row):
        return pltpu.make_async_copy(ys_ref.at[pl.ds(src_row, nrows)], ybuf.at[s, pl.ds(dst_row, nrows)], sem.at[s])

    @pl.when(i == 0)
    def _():
        ybuf[...] = jnp.zeros_like(ybuf)
        _start_copies(lists, functools.partial(make_copy, 0))

    @pl.when(i + 1 < nt)
    def _():
        _start_copies(lists_nx, functools.partial(make_copy, 1 - slot))

    _wait_units(_copy_units(lists), functools.partial(make_copy, slot))

    gates = gate_ref[...]
    pos = pos_ref[...]
    tm, d = h_ref.shape
    q = lax.broadcasted_iota(I32, (tm, rows), 1)
    wsel = jnp.zeros((tm, rows), F32)
    for k in range(top_k):
        wsel = jnp.where(q == pos[:, k:k + 1], gates[:, k:k + 1], wsel)
    out = h_ref[...] + jnp.dot(wsel.astype(BF16), _unpack_pairs(ybuf[slot]), preferred_element_type=F32)
    if final:
        out = _rms(out, fg_ref[...])
    o_ref[...] = out


def _moe_layer(h, g, w_router, b_router, w_gate_up, b_gate_up, w_down, b_down, final_g, final, layer):
    t, d = h.shape
    n_exp = w_router.shape[1]
    f = w_down.shape[2]
    top_k = TOP_K
    dp = d // 2
    bm = MOE_BLOCK
    ts = _pick(t, MOE_SORT_TILE)
    nt = t // ts
    srows = -(-(ts * top_k + n_exp * RUN_ALIGN) // LANES) * LANES
    n_blocks = -(-(t * top_k + nt * n_exp * (RUN_ALIGN - 1)) // bm) + n_exp
    n_slots = n_blocks * bm
    n_steps = (n_blocks + n_exp + 1) // 2
    nbp = -(-n_steps // SUBLANES) * SUBLANES

    tm = ts * max(1, _pick(nt, 2))
    wrt = w_router.T
    wr_hi = wrt.astype(BF16)
    wr_lo = (wrt - wr_hi.astype(F32)).astype(BF16)
    idxt, gatest, ncht = pl.pallas_call(
        functools.partial(_route_kernel, n_exp=n_exp, top_k=top_k, ts=ts),
        grid=(t // tm,),
        in_specs=[pl.BlockSpec((tm, d), lambda i: (i, 0)), _full((1, d)), _full((n_exp, d)),
                  _full((n_exp, d)), _full((n_exp, 1))],
        out_specs=[pl.BlockSpec((top_k, tm), lambda i: (0, i)), pl.BlockSpec((top_k, tm), lambda i: (0, i)),
                   pl.BlockSpec((tm // ts, n_exp, 1), lambda i: (i, 0, 0))],
        out_shape=[jax.ShapeDtypeStruct((top_k, t), I32), jax.ShapeDtypeStruct((top_k, t), F32),
                   jax.ShapeDtypeStruct((nt, n_exp, 1), I32)],
        compiler_params=_params(("arbitrary",)),
        name="moe_route",
    )(h, g.reshape(1, d), wr_hi, wr_lo, b_router.reshape(n_exp, 1))
    idx, gates, nch = idxt.T, gatest.T, ncht.reshape(nt, 1, n_exp)

    per_big = BIG_CHUNK // RUN_ALIGN
    n_big = srows // BIG_CHUNK
    n_small = n_exp * (per_big - 1)
    list_len = (n_big, n_big, n_small, n_small, SUBLANES)
    rank_out = pl.pallas_call(
        functools.partial(_rank_kernel, n_exp=n_exp, top_k=top_k, bm=bm, nbp=nbp),
        grid=(nt,),
        in_specs=[pl.BlockSpec((ts, top_k), lambda i: (i, 0)), pl.BlockSpec((top_k, ts), lambda i: (0, i)),
                  _full((nt, 1, n_exp))],
        out_specs=[pl.BlockSpec((ts, top_k), lambda i: (i, 0)), pl.BlockSpec((top_k, ts), lambda i: (0, i))]
        + [pl.BlockSpec((1, 1, r), lambda i: (i, 0, 0)) for r in list_len]
        + [_full((nbp, 1))] * 4 + [_full((SUBLANES, n_exp))],
        out_shape=[jax.ShapeDtypeStruct((t, top_k), I32), jax.ShapeDtypeStruct((top_k, t), I32)]
        + [jax.ShapeDtypeStruct((nt, 1, r), I32) for r in list_len]
        + [jax.ShapeDtypeStruct((nbp, 1), I32)] * 4 + [jax.ShapeDtypeStruct((SUBLANES, n_exp), I32)],
        scratch_shapes=[pltpu.VMEM((1, n_exp), F32), pltpu.VMEM((1, n_exp), F32)],
        compiler_params=_params(("arbitrary",)),
        name="moe_rank",
    )(idx, idxt, nch)
    pos, post = rank_out[0], rank_out[1]
    blr, bgr, slr, sgr, cnt = rank_out[2:7]
    bexp, nxt, sblk, spair = (a.reshape(nbp) for a in rank_out[7:11])
    meta = rank_out[11]
    lists = [cnt, blr, bgr, slr, sgr]
    n_used = meta[0, 0:1]
    last_blk = meta[1]

    def smem_lists(step=0):
        return [pl.BlockSpec((1, 1, a.shape[2]), lambda i, *_: (jnp.minimum(i + step, nt - 1), 0, 0),
                             memory_space=pltpu.SMEM) for a in lists]

    xs = pl.pallas_call(
        functools.partial(_dispatch_kernel, n_exp=n_exp, top_k=top_k, bm=bm),
        grid_spec=pltpu.PrefetchScalarGridSpec(
            num_scalar_prefetch=1,
            grid=(nt,),
            in_specs=smem_lists() + [
                pl.BlockSpec((ts, d), lambda i, last: (i, 0)),
                pl.BlockSpec((1, d), lambda i, last: (0, 0)),
                pl.BlockSpec((top_k, ts), lambda i, last: (0, i)),
            ],
            out_specs=pl.BlockSpec(memory_space=pl.ANY),
            scratch_shapes=[pltpu.VMEM((2, srows, dp), U32), pltpu.VMEM((bm, dp), U32), pltpu.SMEM((1,), I32),
                            pltpu.SemaphoreType.DMA((2,)), pltpu.SemaphoreType.DMA(())],
        ),
        out_shape=jax.ShapeDtypeStruct((n_slots, dp), U32),
        compiler_params=_params(("arbitrary",)),
        name="moe_dispatch",
    )(last_blk, *lists, h, g.reshape(1, d), post)

    def stp(s, nu):
        return jnp.minimum(s, nu[0] - 1)

    def blk_a(s, be, nx, sb, sp, nu):
        return (sb[stp(s, nu)], 0)

    def blk_b(s, be, nx, sb, sp, nu):
        return (sb[stp(s, nu)] + sp[stp(s, nu)], 0)

    def of_expert(shape):
        return pl.BlockSpec(shape, lambda s, be, nx, sb, sp, nu: (be[stp(s, nu)],) + (0,) * (len(shape) - 1))

    ys = pl.pallas_call(
        functools.partial(_ffn_kernel, f=f, layer=layer),
        grid_spec=pltpu.PrefetchScalarGridSpec(
            num_scalar_prefetch=5,
            grid=(n_steps,),
            in_specs=[
                pl.BlockSpec((bm, dp), blk_a), pl.BlockSpec((bm, dp), blk_b),
                pl.BlockSpec(memory_space=pl.ANY), of_expert((1, 1, 2 * f)),
                pl.BlockSpec(memory_space=pl.ANY), of_expert((1, 1, d)),
            ],
            out_specs=pl.BlockSpec(memory_space=pl.ANY),
            scratch_shapes=[pltpu.VMEM((d, 2 * f), F32), pltpu.VMEM((f, d), F32),
                            pltpu.VMEM((d, 2 * f), BF16), pltpu.VMEM((f, d), BF16),
                            pltpu.VMEM((2, 2 * bm, dp), U32), pltpu.SMEM((2,), I32),
                            pltpu.SemaphoreType.DMA((2,)), pltpu.SemaphoreType.DMA((2,))],
        ),
        out_shape=jax.ShapeDtypeStruct((n_slots, dp), U32),
        compiler_params=_params(("arbitrary",)),
        name="moe_ffn",
    )(bexp, nxt, sblk, spair, n_used, xs, xs, w_gate_up, b_gate_up.reshape(n_exp, 1, 2 * f), w_down,
      b_down.reshape(n_exp, 1, d))

    return pl.pallas_call(
        functools.partial(_combine_kernel, top_k=top_k, final=final),
        grid=(nt,),
        in_specs=smem_lists() + smem_lists(step=1) + [
            pl.BlockSpec((ts, d), lambda i: (i, 0)),
            pl.BlockSpec((ts, top_k), lambda i: (i, 0)),
            pl.BlockSpec((ts, top_k), lambda i: (i, 0)),
            pl.BlockSpec(memory_space=pl.ANY),
            _full((1, d)),
        ],
        out_specs=pl.BlockSpec((ts, d), lambda i: (i, 0)),
        out_shape=jax.ShapeDtypeStruct((t, d), F32),
        scratch_shapes=[pltpu.VMEM((2, srows, dp), U32), pltpu.SemaphoreType.DMA((2,))],
        compiler_params=_params(("arbitrary",)),
        name="moe_combine",
    )(*lists, *lists, h, gates, pos, ys, final_g.reshape(1, d))


def kernel(x, mix_norm, ffn_norm, final_norm, lru_w_in, lru_conv_w, lru_conv_b, lru_w_a, lru_b_a, lru_w_x, lru_b_x, lru_a_param, lru_w_out, pool_w_in, pool_w_group, pool_scale, pool_w_out, sb_w_qkv, sb_w_out, moe_w_router, moe_b_router, moe_w_gate_up, moe_b_gate_up, moe_w_down, moe_b_down):
    batch, seq, d = x.shape
    depth = mix_norm.shape[0]
    h = x.reshape(batch * seq, d)
    for layer in range(depth):
        kind = layer % N_MIXERS
        slot = layer // N_MIXERS
        if kind == 0:
            h = _lru_layer(h, batch, seq, mix_norm[layer], lru_w_in[slot], lru_conv_w[slot], lru_conv_b[slot],
                           lru_w_a[slot], lru_b_a[slot], lru_w_x[slot], lru_b_x[slot], lru_a_param[slot],
                           lru_w_out[slot])
        elif kind == 1:
            h = _pool_layer(h, batch, seq, mix_norm[layer], pool_w_in[slot], pool_w_group[slot], pool_scale[slot],
                            pool_w_out[slot])
        else:
            h = _sb_layer(h, batch, seq, mix_norm[layer], sb_w_qkv[slot], sb_w_out[slot])
        h = _moe_layer(h, ffn_norm[layer], moe_w_router[layer], moe_b_router[layer], moe_w_gate_up,
                       moe_b_gate_up[layer], moe_w_down, moe_b_down[layer], final_norm,
                       final=(layer == depth - 1), layer=layer)
    return h.reshape(batch, seq, d)
```

```python
import functools
import math

import jax
import jax.numpy as jnp
from jax import lax
from jax.experimental import pallas as pl
from jax.experimental.pallas import tpu as pltpu

F32 = jnp.float32
BF16 = jnp.bfloat16
I32 = jnp.int32
U32 = jnp.uint32

RMS_EPS = 1e-6
LRU_C = 8.0
POOL_WINDOWS = (2, 4, 8, 16)
SB_HEADS = 16
LOG2E = 1.4426950408889634
TOP_K = 4
SWIGLU_LIMIT = 7.0
SWIGLU_ALPHA = 1.702
N_MIXERS = 3

V7X_VMEM_BYTES = 64 * 1024 * 1024
VMEM_LIMIT_BYTES = V7X_VMEM_BYTES - 8 * 1024 * 1024
SUBLANES = 8
LANES = 128

MOE_BLOCK = 256
MOE_SORT_TILE = 256
RUN_ALIGN = SUBLANES
BIG_CHUNK = 4 * RUN_ALIGN
WAIT_CHUNK = 16 * RUN_ALIGN


def _params(semantics, vmem=VMEM_LIMIT_BYTES):
    return pltpu.CompilerParams(dimension_semantics=semantics, vmem_limit_bytes=vmem)


def _pick(n, pref):
    t = min(n, pref)
    while n % t:
        t //= 2
    return t


def _rms(h, g):
    ms = jnp.mean(h * h, axis=-1, keepdims=True)
    return h * lax.rsqrt(ms + RMS_EPS) * g


def _sigmoid(x):
    return 1.0 / (1.0 + jnp.exp(-x))


def _softplus(x):
    return jnp.maximum(x, 0.0) + jnp.log(1.0 + jnp.exp(-jnp.abs(x)))


def _gelu_tanh(x):
    c = math.sqrt(2.0 / math.pi)
    return 0.5 * x * (1.0 + jnp.tanh(c * (x + 0.044715 * (x * x * x))))


def _full(shape):
    n = len(shape)
    return pl.BlockSpec(shape, lambda *_: (0,) * n)


def _pack_pairs(x):
    n = x.shape[1] // 2
    lo = lax.bitcast_convert_type(x[:, :n], U32)
    hi = lax.bitcast_convert_type(x[:, n:], U32)
    return lax.shift_right_logical(lo, jnp.uint32(16)) | (hi & jnp.uint32(0xFFFF0000))


def _unpack_pairs(w):
    lo = lax.bitcast_convert_type(lax.shift_left(w, jnp.uint32(16)), F32)
    hi = lax.bitcast_convert_type(w & jnp.uint32(0xFFFF0000), F32)
    return jnp.concatenate([lo.astype(BF16), hi.astype(BF16)], axis=1)


def _lru_kernel(h_ref, g_ref, win_ref, cw_ref, cb_ref, wax_ref, ba_ref, bx_ref, ap_ref, wout_ref,
                o_ref, hcar_ref, ext_ref, y_ref, *, ts, w, bw):
    s = pl.program_id(1)

    @pl.when(s == 0)
    def _():
        hcar_ref[...] = jnp.zeros_like(hcar_ref)
        ext_ref[0:SUBLANES, :] = jnp.zeros((SUBLANES, w), F32)

    h = h_ref[...]
    u = _rms(h, g_ref[...]).astype(BF16)
    proj = jnp.dot(u, win_ref[...], preferred_element_type=F32)
    row8 = lax.broadcasted_iota(I32, (SUBLANES, bw), 0)
    kw = cw_ref.shape[0]

    for c in range(w // bw):
        lo, hi = c * bw, (c + 1) * bw
        gate = proj[:, lo:hi]
        xb = proj[:, w + lo:w + hi]
        ext_ref[SUBLANES:SUBLANES + ts, lo:hi] = xb
        xc = cb_ref[:, lo:hi] + xb * cw_ref[kw - 1:kw, lo:hi]
        for j in range(1, kw):
            xc = xc + ext_ref[SUBLANES - j:SUBLANES - j + ts, lo:hi] * cw_ref[kw - 1 - j:kw - j, lo:hi]
        ext_ref[0:SUBLANES, lo:hi] = xb[ts - SUBLANES:ts]

        res = jnp.dot(xc.astype(BF16), wax_ref[c], preferred_element_type=F32)
        r = _sigmoid(res[:, :bw] + ba_ref[:, lo:hi])
        i = _sigmoid(res[:, bw:] + bx_ref[:, lo:hi])
        log_a = (-LRU_C * _softplus(-ap_ref[:, lo:hi])) * r
        a = jnp.exp(log_a)
        b = jnp.sqrt(-jnp.tanh(log_a) * (a * a + 1.0)) * (i * xc)
        hprev = hcar_ref[:, lo:hi]
        groups = []
        for gi in range(ts // SUBLANES):
            ag = a[gi * SUBLANES:(gi + 1) * SUBLANES]
            bg = b[gi * SUBLANES:(gi + 1) * SUBLANES]
            d = 1
            while d < SUBLANES:
                keep = row8 >= d
                a_sh = jnp.where(keep, pltpu.roll(ag, d, axis=0), 1.0)
                b_sh = jnp.where(keep, pltpu.roll(bg, d, axis=0), 0.0)
                bg = ag * b_sh + bg
                ag = ag * a_sh
                d *= 2
            hg = ag * hprev + bg
            hprev = hg[SUBLANES - 1:SUBLANES]
            groups.append(hg)
        hs = jnp.concatenate(groups, axis=0)
        hcar_ref[:, lo:hi] = hprev
        y_ref[:, lo:hi] = (hs * _gelu_tanh(gate)).astype(BF16)

    o_ref[...] = h + jnp.dot(y_ref[...], wout_ref[...], preferred_element_type=F32)


def _lru_layer(h, batch, seq, g, w_in, conv_w, conv_b, w_a, b_a, w_x, b_x, a_param, w_out):
    t, d = h.shape
    w = w_in.shape[1] // 2
    nblk, bw, _ = w_a.shape
    ts = _pick(seq, 512)
    ns = seq // ts
    wax = jnp.concatenate([w_a, w_x], axis=-1).astype(BF16)
    row = lambda v: v.reshape(1, -1)
    kern = functools.partial(_lru_kernel, ts=ts, w=w, bw=bw)
    return pl.pallas_call(
        kern,
        grid=(batch, ns),
        in_specs=[
            pl.BlockSpec((ts, d), lambda b, s: (b * ns + s, 0)),
            _full((1, d)), _full((d, 2 * w)), _full(conv_w.shape), _full((1, w)),
            _full(wax.shape), _full((1, w)), _full((1, w)), _full((1, w)), _full((w, d)),
        ],
        out_specs=pl.BlockSpec((ts, d), lambda b, s: (b * ns + s, 0)),
        out_shape=jax.ShapeDtypeStruct((t, d), F32),
        scratch_shapes=[pltpu.VMEM((1, w), F32), pltpu.VMEM((SUBLANES + ts, w), F32), pltpu.VMEM((ts, w), BF16)],
        compiler_params=_params(("arbitrary", "arbitrary")),
        name="lru_mixer",
    )(h, row(g), w_in.astype(BF16), conv_w, row(conv_b), wax, row(b_a), row(b_x), row(a_param),
      w_out.astype(BF16))


def _pool_kernel(h_ref, g_ref, win_ref, wg_ref, sc_ref, wout_ref, o_ref, ext_ref, m_ref, *, ts, d, wins):
    s = pl.program_id(1)
    pad = wins[-1]
    gw = d // len(wins)

    @pl.when(s == 0)
    def _():
        ext_ref[0:pad, :] = jnp.zeros((pad, d), F32)

    h = h_ref[...]
    u = _rms(h, g_ref[...]).astype(BF16)
    v = jnp.dot(u, win_ref[...], preferred_element_type=F32)
    ext_ref[pad:pad + ts, :] = v
    n_avail = (s * ts + lax.broadcasted_iota(I32, (ts, 1), 0) + 1).astype(F32)

    cur = ext_ref[...]
    width = 1
    for gi, win in enumerate(wins):
        while width < win:
            cur = cur + pltpu.roll(cur, width, axis=0)
            width *= 2
        lo, hi = gi * gw, (gi + 1) * gw
        mean = cur[pad:, 0:gw] / jnp.minimum(n_avail, float(win))
        pooled = mean - v[:, lo:hi]
        mixed = jnp.dot(pooled.astype(BF16), wg_ref[gi], preferred_element_type=F32)
        m_ref[:, lo:hi] = (mixed * sc_ref[:, lo:hi]).astype(BF16)
        cur = cur[:, gw:]
    ext_ref[0:pad, :] = v[ts - pad:ts]
    o_ref[...] = h + jnp.dot(m_ref[...], wout_ref[...], preferred_element_type=F32)


def _pool_layer(h, batch, seq, g, w_in, w_group, scale, w_out):
    t, d = h.shape
    ts = _pick(seq, 512)
    ns = seq // ts
    pad = POOL_WINDOWS[-1]
    kern = functools.partial(_pool_kernel, ts=ts, d=d, wins=POOL_WINDOWS)
    return pl.pallas_call(
        kern,
        grid=(batch, ns),
        in_specs=[
            pl.BlockSpec((ts, d), lambda b, s: (b * ns + s, 0)),
            _full((1, d)), _full((d, d)), _full(w_group.shape), _full((1, d)), _full((d, d)),
        ],
        out_specs=pl.BlockSpec((ts, d), lambda b, s: (b * ns + s, 0)),
        out_shape=jax.ShapeDtypeStruct((t, d), F32),
        scratch_shapes=[pltpu.VMEM((ts + pad, d), F32), pltpu.VMEM((ts, d), BF16)],
        compiler_params=_params(("arbitrary", "arbitrary")),
        name="pool_mixer",
    )(h, g.reshape(1, d), w_in.astype(BF16), w_group.astype(BF16), scale.reshape(1, d), w_out.astype(BF16))


def _sb_proj_kernel(h_ref, g_ref, wqt_ref, wk_ref, wvt_ref, qt_ref, k_ref, vt_ref):
    u = _rms(h_ref[...], g_ref[...]).astype(BF16)
    nt = (((1,), (1,)), ((), ()))
    qt_ref[...] = lax.dot_general(wqt_ref[...], u, nt, preferred_element_type=F32).astype(BF16)
    k_ref[...] = jnp.dot(u, wk_ref[...], preferred_element_type=F32).astype(BF16)
    vt_ref[...] = lax.dot_general(wvt_ref[...], u, nt, preferred_element_type=F32).astype(BF16)


def _sb_attn_kernel(qt_ref, k_ref, vt_ref, h_ref, wout_ref, o_ref, lb_ref, zs_ref, a_ref, *, bq, dh, nh):
    qb = pl.program_id(1)
    krow = lax.broadcasted_iota(I32, (bq, bq), 0)
    qcol = lax.broadcasted_iota(I32, (bq, bq), 1)
    upper = (qcol >= krow).astype(BF16)
    causal = krow < qcol

    def step(chains, masked, st):
        for c, (hh, j) in enumerate(chains):
            qh = qt_ref[hh * dh:(hh + 1) * dh, :]
            kh = k_ref[pl.ds(j * bq, bq), hh * dh:(hh + 1) * dh]
            z = jnp.dot(kh, qh, preferred_element_type=F32)
            t = jnp.exp2(jnp.abs(z) * (-LOG2E))
            l1m = jnp.minimum(-z, 0.0) - jnp.log(1.0 + t)
            zs_ref[c] = z
            lm = jnp.where(causal, l1m, 0.0) if masked else l1m
            lb_ref[c] = lm.astype(BF16)
        st = list(st)
        for c, (hh, j) in enumerate(chains):
            acc, carry = st[hh]
            suffix = jnp.dot(upper, lb_ref[c], preferred_element_type=F32)
            a = jnp.exp(zs_ref[c] + suffix + carry)
            if masked:
                a = jnp.where(causal, a, 0.0)
            a_ref[c] = a.astype(BF16)
            st[hh] = (acc, carry + suffix[0:1, :])
        for c, (hh, j) in enumerate(chains):
            acc, carry = st[hh]
            vth = vt_ref[hh * dh:(hh + 1) * dh, pl.ds(j * bq, bq)]
            st[hh] = (acc + jnp.dot(vth, a_ref[c], preferred_element_type=F32), carry)
        return st

    heads = range(nh)
    zero = (jnp.zeros((dh, bq), F32), jnp.zeros((1, bq), F32))
    st = step([(hh, qb) for hh in heads], True, [zero] * nh)

    def pair_body(i, st):
        j = qb - 1 - 2 * i
        return step([(hh, j) for hh in heads] + [(hh, j - 1) for hh in heads], False, st)

    def single_body(i, st):
        return step([(hh, 0) for hh in heads], False, st)

    st = lax.fori_loop(0, lax.shift_right_logical(qb, 1), pair_body, st)
    st = lax.fori_loop(0, qb & 1, single_body, st)
    o = jnp.concatenate([st[hh][0].T for hh in heads], axis=1).astype(BF16)
    o_ref[...] = h_ref[...] + jnp.dot(o, wout_ref[...], preferred_element_type=F32)


def _sb_layer(h, batch, seq, g, w_qkv, w_out):
    t, d = h.shape
    dh = d // SB_HEADS
    scale = 1.0 / math.sqrt(dh)
    wqt = (w_qkv[:, :d] * scale).T.astype(BF16)
    wk = w_qkv[:, d:2 * d].astype(BF16)
    wvt = w_qkv[:, 2 * d:].T.astype(BF16)
    tp = _pick(t, 512)
    qt, k, vt = pl.pallas_call(
        _sb_proj_kernel,
        grid=(t // tp,),
        in_specs=[pl.BlockSpec((tp, d), lambda i: (i, 0)), _full((1, d)), _full((d, d)), _full((d, d)),
                  _full((d, d))],
        out_specs=[pl.BlockSpec((d, tp), lambda i: (0, i)), pl.BlockSpec((tp, d), lambda i: (i, 0)),
                   pl.BlockSpec((d, tp), lambda i: (0, i))],
        out_shape=[jax.ShapeDtypeStruct((d, t), BF16), jax.ShapeDtypeStruct((t, d), BF16),
                   jax.ShapeDtypeStruct((d, t), BF16)],
        compiler_params=_params(("arbitrary",)),
        name="sb_qkv_proj",
    )(h, g.reshape(1, d), wqt, wk, wvt)

    bq = _pick(seq, 256)
    nq = seq // bq
    nh = SB_HEADS
    nchain = 2 * nh
    return pl.pallas_call(
        functools.partial(_sb_attn_kernel, bq=bq, dh=dh, nh=nh),
        grid=(batch, nq),
        in_specs=[
            pl.BlockSpec((d, bq), lambda b, q: (0, b * nq + q)),
            pl.BlockSpec((seq, d), lambda b, q: (b, 0)),
            pl.BlockSpec((d, seq), lambda b, q: (0, b)),
            pl.BlockSpec((bq, d), lambda b, q: (b * nq + q, 0)),
            _full((d, d)),
        ],
        out_specs=pl.BlockSpec((bq, d), lambda b, q: (b * nq + q, 0)),
        out_shape=jax.ShapeDtypeStruct((t, d), F32),
        scratch_shapes=[pltpu.VMEM((nchain, bq, bq), BF16), pltpu.VMEM((nchain, bq, bq), F32),
                        pltpu.VMEM((nchain, bq, bq), BF16)],
        compiler_params=_params(("arbitrary", "arbitrary")),
        name="sb_attention",
    )(qt, k, vt, h, w_out.astype(BF16))


def _route_kernel(h_ref, g_ref, wr_hi_ref, wr_lo_ref, br_ref, idxt_ref, gatet_ref, nch_ref, *, n_exp, top_k, ts):
    u = _rms(h_ref[...], g_ref[...])
    u_hi = u.astype(BF16)
    u_lo = (u - u_hi.astype(F32)).astype(BF16)
    nt_dims = (((1,), (1,)), ((), ()))
    logits = (lax.dot_general(wr_hi_ref[...], u_hi, nt_dims, preferred_element_type=F32)
              + lax.dot_general(wr_lo_ref[...], u_hi, nt_dims, preferred_element_type=F32)
              + lax.dot_general(wr_hi_ref[...], u_lo, nt_dims, preferred_element_type=F32)) + br_ref[...]
    tm = logits.shape[1]
    sub = lax.broadcasted_iota(I32, (n_exp, tm), 0)
    krow = lax.broadcasted_iota(I32, (top_k, tm), 0)
    vals = logits
    top_v, top_i = [], []
    for _ in range(top_k):
        m = jnp.max(vals, axis=0, keepdims=True)
        sel = jnp.min(jnp.where(vals == m, sub, n_exp), axis=0, keepdims=True)
        top_v.append(m)
        top_i.append(sel)
        vals = jnp.where(sub == sel, -jnp.inf, vals)
    exps = [jnp.exp(v - top_v[0]) for v in top_v]
    denom = exps[0]
    for e in exps[1:]:
        denom = denom + e
    idx = jnp.zeros((top_k, tm), I32)
    gates = jnp.zeros((top_k, tm), F32)
    for k in range(top_k):
        idx = jnp.where(krow == k, top_i[k], idx)
        gates = jnp.where(krow == k, exps[k] / denom, gates)
    idxt_ref[...] = idx
    gatet_ref[...] = gates
    chosen = jnp.where(sub == top_i[0], 1.0, 0.0)
    for k in range(1, top_k):
        chosen = jnp.where(sub == top_i[k], 1.0, chosen)
    for s in range(tm // ts):
        cnt = jnp.sum(chosen[:, s * ts:(s + 1) * ts], axis=1, keepdims=True)
        nch_ref[s] = jnp.floor((cnt + (RUN_ALIGN - 1)) * (1.0 / RUN_ALIGN)).astype(I32)


def _rank_kernel(idx_ref, idxt_ref, nchall_ref, pos_ref, post_ref, blr_ref, bgr_ref, slr_ref, sgr_ref, cnt_ref,
                 bexp_ref, nxt_ref, sblk_ref, spair_ref, meta_ref, run_ref, pstart_ref, *, n_exp, top_k, bm, nbp):
    i = pl.program_id(0)
    tm = idx_ref.shape[0]
    cpb = bm // RUN_ALIGN
    idx = idx_ref[...]
    lane = lax.broadcasted_iota(I32, (tm, n_exp), 1)
    ohs = [(idx[:, k:k + 1] == lane).astype(F32) for k in range(top_k)]
    oh = ohs[0]
    for o in ohs[1:]:
        oh = oh + o
    cnt = jnp.sum(oh, axis=0, keepdims=True)
    nch = jnp.floor((cnt + (RUN_ALIGN - 1)) * (1.0 / RUN_ALIGN))
    er = lax.broadcasted_iota(I32, (n_exp, n_exp), 0)
    ec = lax.broadcasted_iota(I32, (n_exp, n_exp), 1)

    def excl_cumsum_row(v):
        v8 = jnp.broadcast_to(v, (SUBLANES, n_exp)).astype(BF16)
        return jnp.dot(v8, (er < ec).astype(BF16), preferred_element_type=F32)[0:1, :]

    @pl.when(i == 0)
    def _():
        tot = jnp.sum(nchall_ref[...].astype(F32), axis=0)
        nb = jnp.floor((tot + (cpb - 1)) * (1.0 / cpb))
        bstart = excl_cumsum_row(nb)
        bend = bstart + nb
        pstart_ref[...] = bstart * cpb
        nst = jnp.floor((nb + 1.0) * 0.5)
        sstart = excl_cumsum_row(nst)
        send = sstart + nst
        step = lax.broadcasted_iota(I32, (nbp, n_exp), 0).astype(F32)
        se = jnp.sum((send <= step).astype(F32), axis=-1, keepdims=True)
        se = jnp.minimum(se, n_exp - 1)
        bexp_ref[...] = se.astype(I32)
        mine = lax.broadcasted_iota(I32, (nbp, n_exp), 1).astype(F32) == se
        pick = lambda v: jnp.sum(jnp.where(mine, v, 0.0), axis=-1, keepdims=True)
        local = step[:, 0:1] - pick(sstart)
        sblk_ref[...] = (pick(bstart) + 2.0 * local).astype(I32)
        spair_ref[...] = jnp.where(2.0 * local + 1.0 < pick(nb), 1.0, 0.0).astype(I32)
        n_used = jnp.sum(nst, axis=-1, keepdims=True)
        run_end = pick(send)
        nx = jnp.minimum(jnp.sum((send <= run_end).astype(F32), axis=-1, keepdims=True), n_exp - 1)
        nxt_ref[...] = jnp.where(run_end < n_used, nx, -1.0).astype(I32)
        last = jnp.where(nb > 0, bend - 1.0, -1.0)
        mrow = lax.broadcasted_iota(I32, (SUBLANES, n_exp), 0)
        meta = jnp.where(mrow == 0, jnp.broadcast_to(n_used, (SUBLANES, n_exp)),
                         jnp.where(mrow == 1, jnp.broadcast_to(last, (SUBLANES, n_exp)), 0.0))
        meta_ref[...] = meta.astype(I32)
        run_ref[...] = jnp.zeros_like(run_ref)

    tr = lax.broadcasted_iota(I32, (tm, tm), 0)
    tc = lax.broadcasted_iota(I32, (tm, tm), 1)
    off = excl_cumsum_row(nch)
    excl = jnp.dot((tc < tr).astype(BF16), oh.astype(BF16), preferred_element_type=F32)
    base = excl + off * RUN_ALIGN
    kcol = lax.broadcasted_iota(I32, (tm, top_k), 1)
    pos = jnp.zeros((tm, top_k), I32)
    for k in range(top_k):
        pk = jnp.sum(ohs[k] * base, axis=-1, keepdims=True).astype(I32)
        pos = jnp.where(kcol == k, pk, pos)
    pos_ref[...] = pos
    idxt = idxt_ref[...]
    sub = lax.broadcasted_iota(I32, (n_exp, tm), 0)
    ohts = [(idxt[k:k + 1, :] == sub).astype(F32) for k in range(top_k)]
    oht = ohts[0]
    for o in ohts[1:]:
        oht = oht + o
    cnt_col = jnp.sum(oht, axis=1, keepdims=True)
    nch_col = jnp.floor((cnt_col + (RUN_ALIGN - 1)) * (1.0 / RUN_ALIGN))
    off_col = jnp.dot((ec < er).astype(BF16), jnp.broadcast_to(nch_col, (n_exp, LANES)).astype(BF16),
                      preferred_element_type=F32)[:, 0:1]
    exclt = jnp.dot(oht.astype(BF16), (tr < tc).astype(BF16), preferred_element_type=F32)
    baset = exclt + off_col * RUN_ALIGN
    krow = lax.broadcasted_iota(I32, (top_k, tm), 0)
    post = jnp.zeros((top_k, tm), I32)
    for k in range(top_k):
        pk = jnp.sum(ohts[k] * baset, axis=0, keepdims=True).astype(I32)
        post = jnp.where(krow == k, pk, post)
    post_ref[...] = post

    gst_row = pstart_ref[...] + run_ref[...]
    gst_col = jnp.sum(jnp.where(er == ec, jnp.broadcast_to(gst_row, (n_exp, n_exp)), 0.0), axis=1, keepdims=True)
    per_big = BIG_CHUNK // RUN_ALIGN
    nbig = jnp.floor(nch_col * (1.0 / per_big))
    nsm = nch_col - nbig * per_big

    def excl_cumsum_col(v):
        return jnp.dot((ec < er).astype(BF16), jnp.broadcast_to(v, (n_exp, LANES)).astype(BF16),
                       preferred_element_type=F32)[:, 0:1]

    def copy_list(count, loc0, glob0, stride, n):
        before = excl_cumsum_col(count)
        j = lax.broadcasted_iota(I32, (n_exp, n), 1).astype(F32)
        owner = jnp.sum(((before + count) <= j).astype(F32), axis=0, keepdims=True)
        mine = lax.broadcasted_iota(I32, (n_exp, n), 0).astype(F32) == owner
        pick = lambda v: jnp.sum(jnp.where(mine, v, 0.0), axis=0, keepdims=True)
        piece = (j[0:1, :] - pick(before)) * stride
        return (pick(loc0) + piece).astype(I32), (pick(glob0) + piece).astype(I32)

    blr_ref[0], bgr_ref[0] = copy_list(nbig, off_col, gst_col, per_big, blr_ref.shape[2])
    slr_ref[0], sgr_ref[0] = copy_list(nsm, off_col + nbig * per_big, gst_col + nbig * per_big, 1, slr_ref.shape[2])
    clane = lax.broadcasted_iota(I32, (1, cnt_ref.shape[2]), 1)
    cnt_ref[0] = jnp.where(clane == 0, jnp.sum(nbig, axis=0, keepdims=True),
                           jnp.where(clane == 1, jnp.sum(nsm, axis=0, keepdims=True), 0.0)).astype(I32)
    run_ref[...] = run_ref[...] + nch


def _copy_units(lists):
    cnt_ref = lists[0]
    return cnt_ref[0, 0, 0] * (BIG_CHUNK // RUN_ALIGN) + cnt_ref[0, 0, 1]


def _start_copies(lists, make_copy):
    cnt_ref, blr_ref, bgr_ref, slr_ref, sgr_ref = lists

    def start(nrows, loc_ref, glob_ref):
        def body(j, c):
            make_copy(nrows, pl.multiple_of(loc_ref[0, 0, j] * RUN_ALIGN, RUN_ALIGN),
                      pl.multiple_of(glob_ref[0, 0, j] * RUN_ALIGN, RUN_ALIGN)).start()
            return c
        return body

    lax.fori_loop(0, cnt_ref[0, 0, 0], start(BIG_CHUNK, blr_ref, bgr_ref), 0)
    lax.fori_loop(0, cnt_ref[0, 0, 1], start(RUN_ALIGN, slr_ref, sgr_ref), 0)


def _wait_units(n, make_copy):
    per_wait = WAIT_CHUNK // RUN_ALIGN

    def wait_big(j, c):
        make_copy(WAIT_CHUNK, 0, 0).wait()
        return c

    def wait_small(j, c):
        make_copy(RUN_ALIGN, 0, 0).wait()
        return c

    nbig = lax.shift_right_logical(n, per_wait.bit_length() - 1)
    lax.fori_loop(0, nbig, wait_big, 0)
    lax.fori_loop(nbig * per_wait, n, wait_small, 0)


def _dispatch_kernel(last_ref, cnt_ref, blr_ref, bgr_ref, slr_ref, sgr_ref, h_ref, g_ref, post_ref, xs_ref, sbuf,
                     zbuf, pending, sem, zsem, *, n_exp, top_k, bm):
    i = pl.program_id(0)
    nt = pl.num_programs(0)
    slot = i % 2
    rows = sbuf.shape[1]

    @pl.when(i == 0)
    def _():
        zbuf[...] = jnp.zeros_like(zbuf)

        def zero_copy(e):
            return pltpu.make_async_copy(zbuf, xs_ref.at[pl.ds(last_ref[e] * bm, bm)], zsem)

        def start(e, c):
            @pl.when(last_ref[e] >= 0)
            def _():
                zero_copy(e).start()
            return c

        def wait(e, c):
            @pl.when(last_ref[e] >= 0)
            def _():
                zero_copy(e).wait()
            return c

        lax.fori_loop(0, n_exp, start, 0)
        lax.fori_loop(0, n_exp, wait, 0)

    u = _rms(h_ref[...], g_ref[...]).astype(BF16)
    tm = u.shape[0]
    post = post_ref[...]
    q = lax.broadcasted_iota(I32, (rows, tm), 0)
    perm = jnp.zeros((rows, tm), F32)
    for k in range(top_k):
        perm = jnp.where(q == post[k:k + 1, :], 1.0, perm)
    sbuf[slot] = _pack_pairs(jnp.dot(perm.astype(BF16), u, preferred_element_type=F32))

    def make_copy(s, nrows, src_row, dst_row):
        return pltpu.make_async_copy(sbuf.at[s, pl.ds(src_row, nrows)], xs_ref.at[pl.ds(dst_row, nrows)], sem.at[s])

    lists = (cnt_ref, blr_ref, bgr_ref, slr_ref, sgr_ref)
    _start_copies(lists, functools.partial(make_copy, slot))
    n = _copy_units(lists)

    @pl.when(i > 0)
    def _():
        _wait_units(pending[0], functools.partial(make_copy, 1 - slot))

    pending[0] = n

    @pl.when(i == nt - 1)
    def _():
        _wait_units(n, functools.partial(make_copy, slot))


def _ffn_kernel(bexp_ref, nxt_ref, sblk_ref, spair_ref, nused_ref, xa_ref, xb_ref, wgu_hbm, bgu_ref, wd_hbm, bd_ref,
                ys_ref, wgu_st, wd_st, wgu_bf, wd_bf, ybuf, pend, sem, ysem, *, f, layer):
    b = pl.program_id(0)
    last_step = pl.num_programs(0) - 1
    bm = xa_ref.shape[0]
    slot = b % 2
    prev = bexp_ref[jnp.maximum(b - 1, 0)]
    active = b < nused_ref[0]
    here = jnp.minimum(b, nused_ref[0] - 1)
    pair = spair_ref[here] == 1
    row0 = pl.multiple_of(sblk_ref[here] * bm, bm)

    def out_copy(s, nrows):
        return pltpu.make_async_copy(ybuf.at[s, pl.ds(0, nrows)], ys_ref.at[pl.ds(row0, nrows)], ysem.at[s])

    def drain(s):
        for nblk in (1, 2):
            @pl.when(pend[s] == nblk)
            def _():
                out_copy(s, nblk * bm).wait()
        pend[s] = 0

    @pl.when(b == 0)
    def _():
        pend[0] = 0
        pend[1] = 0

    def fetch(e):
        return (pltpu.make_async_copy(wgu_hbm.at[layer, e], wgu_st, sem.at[0]),
                pltpu.make_async_copy(wd_hbm.at[layer, e], wd_st, sem.at[1]))

    @pl.when(b == 0)
    def _():
        for c in fetch(bexp_ref[0]):
            c.start()

    @pl.when(active & ((b == 0) | (bexp_ref[b] != prev)))
    def _():
        for c in fetch(bexp_ref[b]):
            c.wait()
        wgu_bf[...] = wgu_st[...].astype(BF16)
        wd_bf[...] = wd_st[...].astype(BF16)

        @pl.when(nxt_ref[b] >= 0)
        def _():
            for c in fetch(nxt_ref[b]):
                c.start()

    def ffn(xw):
        x = _unpack_pairs(xw)
        gu = jnp.dot(x, wgu_bf[...], preferred_element_type=F32) + bgu_ref[0]
        gate = jnp.minimum(gu[:, :f], SWIGLU_LIMIT)
        up = jnp.clip(gu[:, f:], -SWIGLU_LIMIT, SWIGLU_LIMIT)
        glu = gate * _sigmoid(gate * SWIGLU_ALPHA)
        act = ((up + 1.0) * glu).astype(BF16)
        y = jnp.dot(act, wd_bf[...], preferred_element_type=F32) + bd_ref[0]
        return _pack_pairs(y.astype(BF16).astype(F32))

    @pl.when(active)
    def _():
        drain(slot)

    @pl.when(active & pair)
    def _():
        ybuf[slot] = ffn(jnp.concatenate([xa_ref[...], xb_ref[...]], axis=0))
        out_copy(slot, 2 * bm).start()
        pend[slot] = 2

    @pl.when(active & jnp.logical_not(pair))
    def _():
        ybuf[slot, 0:bm] = ffn(xa_ref[...])
        out_copy(slot, bm).start()
        pend[slot] = 1

    @pl.when(b == last_step)
    def _():
        drain(0)
        drain(1)


def _combine_kernel(*refs, top_k, final):
    lists, lists_nx = refs[0:5], refs[5:10]
    h_ref, gate_ref, pos_ref, ys_ref, fg_ref, o_ref, ybuf, sem = refs[10:]
    i = pl.program_id(0)
    nt = pl.num_programs(0)
    slot = i % 2
    rows = ybuf.shape[1]

    def make_copy(s, nrows, dst_row, src_row):
        return pltpu.make_async_copy(ys_ref.at[pl.ds(src_row, nrows)], ybuf.at[s, pl.ds(dst_row, nrows)], sem.at[s])

    @pl.when(i == 0)
    def _():
        ybuf[...] = jnp.zeros_like(ybuf)
        _start_copies(lists, functools.partial(make_copy, 0))

    @pl.when(i + 1 < nt)
    def _():
        _start_copies(lists_nx, functools.partial(make_copy, 1 - slot))

    _wait_units(_copy_units(lists), functools.partial(make_copy, slot))

    gates = gate_ref[...]
    pos = pos_ref[...]
    tm, d = h_ref.shape
    q = lax.broadcasted_iota(I32, (tm, rows), 1)
    wsel = jnp.zeros((tm, rows), F32)
    for k in range(top_k):
        wsel = jnp.where(q == pos[:, k:k + 1], gates[:, k:k + 1], wsel)
    out = h_ref[...] + jnp.dot(wsel.astype(BF16), _unpack_pairs(ybuf[slot]), preferred_element_type=F32)
    if final:
        out = _rms(out, fg_ref[...])
    o_ref[...] = out


def _moe_layer(h, g, w_router, b_router, w_gate_up, b_gate_up, w_down, b_down, final_g, final, layer):
    t, d = h.shape
    n_exp = w_router.shape[1]
    f = w_down.shape[2]
    top_k = TOP_K
    dp = d // 2
    bm = MOE_BLOCK
    ts = _pick(t, MOE_SORT_TILE)
    nt = t // ts
    srows = -(-(ts * top_k + n_exp * RUN_ALIGN) // LANES) * LANES
    n_blocks = -(-(t * top_k + nt * n_exp * (RUN_ALIGN - 1)) // bm) + n_exp
    n_slots = n_blocks * bm
    n_steps = (n_blocks + n_exp + 1) // 2
    nbp = -(-n_steps // SUBLANES) * SUBLANES

    tm = ts * max(1, _pick(nt, 4))
    wrt = w_router.T
    wr_hi = wrt.astype(BF16)
    wr_lo = (wrt - wr_hi.astype(F32)).astype(BF16)
    idxt, gatest, ncht = pl.pallas_call(
        functools.partial(_route_kernel, n_exp=n_exp, top_k=top_k, ts=ts),
        grid=(t // tm,),
        in_specs=[pl.BlockSpec((tm, d), lambda i: (i, 0)), _full((1, d)), _full((n_exp, d)),
                  _full((n_exp, d)), _full((n_exp, 1))],
        out_specs=[pl.BlockSpec((top_k, tm), lambda i: (0, i)), pl.BlockSpec((top_k, tm), lambda i: (0, i)),
                   pl.BlockSpec((tm // ts, n_exp, 1), lambda i: (i, 0, 0))],
        out_shape=[jax.ShapeDtypeStruct((top_k, t), I32), jax.ShapeDtypeStruct((top_k, t), F32),
                   jax.ShapeDtypeStruct((nt, n_exp, 1), I32)],
        compiler_params=_params(("arbitrary",)),
        name="moe_route",
    )(h, g.reshape(1, d), wr_hi, wr_lo, b_router.reshape(n_exp, 1))
    idx, gates, nch = idxt.T, gatest.T, ncht.reshape(nt, 1, n_exp)

    per_big = BIG_CHUNK // RUN_ALIGN
    n_big = srows // BIG_CHUNK
    n_small = n_exp * (per_big - 1)
    list_len = (n_big, n_big, n_small, n_small, SUBLANES)
    rank_out = pl.pallas_call(
        functools.partial(_rank_kernel, n_exp=n_exp, top_k=top_k, bm=bm, nbp=nbp),
        grid=(nt,),
        in_specs=[pl.BlockSpec((ts, top_k), lambda i: (i, 0)), pl.BlockSpec((top_k, ts), lambda i: (0, i)),
                  _full((nt, 1, n_exp))],
        out_specs=[pl.BlockSpec((ts, top_k), lambda i: (i, 0)), pl.BlockSpec((top_k, ts), lambda i: (0, i))]
        + [pl.BlockSpec((1, 1, r), lambda i: (i, 0, 0)) for r in list_len]
        + [_full((nbp, 1))] * 4 + [_full((SUBLANES, n_exp))],
        out_shape=[jax.ShapeDtypeStruct((t, top_k), I32), jax.ShapeDtypeStruct((top_k, t), I32)]
        + [jax.ShapeDtypeStruct((nt, 1, r), I32) for r in list_len]
        + [jax.ShapeDtypeStruct((nbp, 1), I32)] * 4 + [jax.ShapeDtypeStruct((SUBLANES, n_exp), I32)],
        scratch_shapes=[pltpu.VMEM((1, n_exp), F32), pltpu.VMEM((1, n_exp), F32)],
        compiler_params=_params(("arbitrary",)),
        name="moe_rank",
    )(idx, idxt, nch)
    pos, post = rank_out[0], rank_out[1]
    blr, bgr, slr, sgr, cnt = rank_out[2:7]
    bexp, nxt, sblk, spair = (a.reshape(nbp) for a in rank_out[7:11])
    meta = rank_out[11]
    lists = [cnt, blr, bgr, slr, sgr]
    n_used = meta[0, 0:1]
    last_blk = meta[1]

    def smem_lists(step=0):
        return [pl.BlockSpec((1, 1, a.shape[2]), lambda i, *_: (jnp.minimum(i + step, nt - 1), 0, 0),
                             memory_space=pltpu.SMEM) for a in lists]

    xs = pl.pallas_call(
        functools.partial(_dispatch_kernel, n_exp=n_exp, top_k=top_k, bm=bm),
        grid_spec=pltpu.PrefetchScalarGridSpec(
            num_scalar_prefetch=1,
            grid=(nt,),
            in_specs=smem_lists() + [
                pl.BlockSpec((ts, d), lambda i, last: (i, 0)),
                pl.BlockSpec((1, d), lambda i, last: (0, 0)),
                pl.BlockSpec((top_k, ts), lambda i, last: (0, i)),
            ],
            out_specs=pl.BlockSpec(memory_space=pl.ANY),
            scratch_shapes=[pltpu.VMEM((2, srows, dp), U32), pltpu.VMEM((bm, dp), U32), pltpu.SMEM((1,), I32),
                            pltpu.SemaphoreType.DMA((2,)), pltpu.SemaphoreType.DMA(())],
        ),
        out_shape=jax.ShapeDtypeStruct((n_slots, dp), U32),
        compiler_params=_params(("arbitrary",)),
        name="moe_dispatch",
    )(last_blk, *lists, h, g.reshape(1, d), post)

    def stp(s, nu):
        return jnp.minimum(s, nu[0] - 1)

    def blk_a(s, be, nx, sb, sp, nu):
        return (sb[stp(s, nu)], 0)

    def blk_b(s, be, nx, sb, sp, nu):
        return (sb[stp(s, nu)] + sp[stp(s, nu)], 0)

    def of_expert(shape):
        return pl.BlockSpec(shape, lambda s, be, nx, sb, sp, nu: (be[stp(s, nu)],) + (0,) * (len(shape) - 1))

    ys = pl.pallas_call(
        functools.partial(_ffn_kernel, f=f, layer=layer),
        grid_spec=pltpu.PrefetchScalarGridSpec(
            num_scalar_prefetch=5,
            grid=(n_steps,),
            in_specs=[
                pl.BlockSpec((bm, dp), blk_a), pl.BlockSpec((bm, dp), blk_b),
                pl.BlockSpec(memory_space=pl.ANY), of_expert((1, 1, 2 * f)),
                pl.BlockSpec(memory_space=pl.ANY), of_expert((1, 1, d)),
            ],
            out_specs=pl.BlockSpec(memory_space=pl.ANY),
            scratch_shapes=[pltpu.VMEM((d, 2 * f), F32), pltpu.VMEM((f, d), F32),
                            pltpu.VMEM((d, 2 * f), BF16), pltpu.VMEM((f, d), BF16),
                            pltpu.VMEM((2, 2 * bm, dp), U32), pltpu.SMEM((2,), I32),
                            pltpu.SemaphoreType.DMA((2,)), pltpu.SemaphoreType.DMA((2,))],
        ),
        out_shape=jax.ShapeDtypeStruct((n_slots, dp), U32),
        compiler_params=_params(("arbitrary",)),
        name="moe_ffn",
    )(bexp, nxt, sblk, spair, n_used, xs, xs, w_gate_up, b_gate_up.reshape(n_exp, 1, 2 * f), w_down,
      b_down.reshape(n_exp, 1, d))

    return pl.pallas_call(
        functools.partial(_combine_kernel, top_k=top_k, final=final),
        grid=(nt,),
        in_specs=smem_lists() + smem_lists(step=1) + [
            pl.BlockSpec((ts, d), lambda i: (i, 0)),
            pl.BlockSpec((ts, top_k), lambda i: (i, 0)),
            pl.BlockSpec((ts, top_k), lambda i: (i, 0)),
            pl.BlockSpec(memory_space=pl.ANY),
            _full((1, d)),
        ],
        out_specs=pl.BlockSpec((ts, d), lambda i: (i, 0)),
        out_shape=jax.ShapeDtypeStruct((t, d), F32),
        scratch_shapes=[pltpu.VMEM((2, srows, dp), U32), pltpu.SemaphoreType.DMA((2,))],
        compiler_params=_params(("arbitrary",)),
        name="moe_combine",
    )(*lists, *lists, h, gates, pos, ys, final_g.reshape(1, d))


def kernel(x, mix_norm, ffn_norm, final_norm, lru_w_in, lru_conv_w, lru_conv_b, lru_w_a, lru_b_a, lru_w_x, lru_b_x, lru_a_param, lru_w_out, pool_w_in, pool_w_group, pool_scale, pool_w_out, sb_w_qkv, sb_w_out, moe_w_router, moe_b_router, moe_w_gate_up, moe_b_gate_up, moe_w_down, moe_b_down):
    batch, seq, d = x.shape
    depth = mix_norm.shape[0]
    h = x.reshape(batch * seq, d)
    for layer in range(depth):
        kind = layer % N_MIXERS
        slot = layer // N_MIXERS
        if kind == 0:
            h = _lru_layer(h, batch, seq, mix_norm[layer], lru_w_in[slot], lru_conv_w[slot], lru_conv_b[slot],
                           lru_w_a[slot], lru_b_a[slot], lru_w_x[slot], lru_b_x[slot], lru_a_param[slot],
                           lru_w_out[slot])
        elif kind == 1:
            h = _pool_layer(h, batch, seq, mix_norm[layer], pool_w_in[slot], pool_w_group[slot], pool_scale[slot],
                            pool_w_out[slot])
        else:
            h = _sb_layer(h, batch, seq, mix_norm[layer], sb_w_qkv[slot], sb_w_out[slot])
        h = _moe_layer(h, ffn_norm[layer], moe_w_router[layer], moe_b_router[layer], moe_w_gate_up,
                       moe_b_gate_up[layer], moe_w_down, moe_b_down[layer], final_norm,
                       final=(layer == depth - 1), layer=layer)
    return h.reshape(batch, seq, d)
```

```python
import functools
import math

import jax
import jax.numpy as jnp
from jax import lax
from jax.experimental import pallas as pl
from jax.experimental.pallas import tpu as pltpu

F32 = jnp.float32
BF16 = jnp.bfloat16
I32 = jnp.int32
U32 = jnp.uint32

RMS_EPS = 1e-6
LRU_C = 8.0
POOL_WINDOWS = (2, 4, 8, 16)
SB_HEADS = 16
LOG2E = 1.4426950408889634
TOP_K = 4
SWIGLU_LIMIT = 7.0
SWIGLU_ALPHA = 1.702
N_MIXERS = 3

V7X_VMEM_BYTES = 64 * 1024 * 1024
VMEM_LIMIT_BYTES = V7X_VMEM_BYTES - 8 * 1024 * 1024
SUBLANES = 8
LANES = 128

MOE_BLOCK = 256
MOE_SORT_TILE = 256
RUN_ALIGN = SUBLANES
BIG_CHUNK = 4 * RUN_ALIGN
WAIT_CHUNK = 16 * RUN_ALIGN


def _params(semantics, vmem=VMEM_LIMIT_BYTES):
    return pltpu.CompilerParams(dimension_semantics=semantics, vmem_limit_bytes=vmem)


def _pick(n, pref):
    t = min(n, pref)
    while n % t:
        t //= 2
    return t


def _rms(h, g):
    ms = jnp.mean(h * h, axis=-1, keepdims=True)
    return h * lax.rsqrt(ms + RMS_EPS) * g


def _sigmoid(x):
    return 1.0 / (1.0 + jnp.exp(-x))


def _softplus(x):
    return jnp.maximum(x, 0.0) + jnp.log(1.0 + jnp.exp(-jnp.abs(x)))


def _gelu_tanh(x):
    c = math.sqrt(2.0 / math.pi)
    return 0.5 * x * (1.0 + jnp.tanh(c * (x + 0.044715 * (x * x * x))))


def _full(shape):
    n = len(shape)
    return pl.BlockSpec(shape, lambda *_: (0,) * n)


def _pack_pairs(x):
    n = x.shape[1] // 2
    lo = lax.bitcast_convert_type(x[:, :n], U32)
    hi = lax.bitcast_convert_type(x[:, n:], U32)
    return lax.shift_right_logical(lo, jnp.uint32(16)) | (hi & jnp.uint32(0xFFFF0000))


def _unpack_pairs(w):
    lo = lax.bitcast_convert_type(lax.shift_left(w, jnp.uint32(16)), F32)
    hi = lax.bitcast_convert_type(w & jnp.uint32(0xFFFF0000), F32)
    return jnp.concatenate([lo.astype(BF16), hi.astype(BF16)], axis=1)


def _lru_kernel(h_ref, g_ref, win_ref, cw_ref, cb_ref, wax_ref, ba_ref, bx_ref, ap_ref, wout_ref,
                o_ref, hcar_ref, ext_ref, y_ref, *, ts, w, bw):
    s = pl.program_id(1)

    @pl.when(s == 0)
    def _():
        hcar_ref[...] = jnp.zeros_like(hcar_ref)
        ext_ref[0:SUBLANES, :] = jnp.zeros((SUBLANES, w), F32)

    h = h_ref[...]
    u = _rms(h, g_ref[...]).astype(BF16)
    proj = jnp.dot(u, win_ref[...], preferred_element_type=F32)
    row8 = lax.broadcasted_iota(I32, (SUBLANES, bw), 0)
    kw = cw_ref.shape[0]

    for c in range(w // bw):
        lo, hi = c * bw, (c + 1) * bw
        gate = proj[:, lo:hi]
        xb = proj[:, w + lo:w + hi]
        ext_ref[SUBLANES:SUBLANES + ts, lo:hi] = xb
        xc = cb_ref[:, lo:hi] + xb * cw_ref[kw - 1:kw, lo:hi]
        for j in range(1, kw):
            xc = xc + ext_ref[SUBLANES - j:SUBLANES - j + ts, lo:hi] * cw_ref[kw - 1 - j:kw - j, lo:hi]
        ext_ref[0:SUBLANES, lo:hi] = xb[ts - SUBLANES:ts]

        res = jnp.dot(xc.astype(BF16), wax_ref[c], preferred_element_type=F32)
        r = _sigmoid(res[:, :bw] + ba_ref[:, lo:hi])
        i = _sigmoid(res[:, bw:] + bx_ref[:, lo:hi])
        log_a = (-LRU_C * _softplus(-ap_ref[:, lo:hi])) * r
        a = jnp.exp(log_a)
        b = jnp.sqrt(-jnp.tanh(log_a) * (a * a + 1.0)) * (i * xc)
        hprev = hcar_ref[:, lo:hi]
        groups = []
        for gi in range(ts // SUBLANES):
            ag = a[gi * SUBLANES:(gi + 1) * SUBLANES]
            bg = b[gi * SUBLANES:(gi + 1) * SUBLANES]
            d = 1
            while d < SUBLANES:
                keep = row8 >= d
                a_sh = jnp.where(keep, pltpu.roll(ag, d, axis=0), 1.0)
                b_sh = jnp.where(keep, pltpu.roll(bg, d, axis=0), 0.0)
                bg = ag * b_sh + bg
                ag = ag * a_sh
                d *= 2
            hg = ag * hprev + bg
            hprev = hg[SUBLANES - 1:SUBLANES]
            groups.append(hg)
        hs = jnp.concatenate(groups, axis=0)
        hcar_ref[:, lo:hi] = hprev
        y_ref[:, lo:hi] = (hs * _gelu_tanh(gate)).astype(BF16)

    o_ref[...] = h + jnp.dot(y_ref[...], wout_ref[...], preferred_element_type=F32)


def _lru_layer(h, batch, seq, g, w_in, conv_w, conv_b, w_a, b_a, w_x, b_x, a_param, w_out):
    t, d = h.shape
    w = w_in.shape[1] // 2
    nblk, bw, _ = w_a.shape
    ts = _pick(seq, 512)
    ns = seq // ts
    wax = jnp.concatenate([w_a, w_x], axis=-1).astype(BF16)
    row = lambda v: v.reshape(1, -1)
    kern = functools.partial(_lru_kernel, ts=ts, w=w, bw=bw)
    return pl.pallas_call(
        kern,
        grid=(batch, ns),
        in_specs=[
            pl.BlockSpec((ts, d), lambda b, s: (b * ns + s, 0)),
            _full((1, d)), _full((d, 2 * w)), _full(conv_w.shape), _full((1, w)),
            _full(wax.shape), _full((1, w)), _full((1, w)), _full((1, w)), _full((w, d)),
        ],
        out_specs=pl.BlockSpec((ts, d), lambda b, s: (b * ns + s, 0)),
        out_shape=jax.ShapeDtypeStruct((t, d), F32),
        scratch_shapes=[pltpu.VMEM((1, w), F32), pltpu.VMEM((SUBLANES + ts, w), F32), pltpu.VMEM((ts, w), BF16)],
        compiler_params=_params(("arbitrary", "arbitrary")),
        name="lru_mixer",
    )(h, row(g), w_in.astype(BF16), conv_w, row(conv_b), wax, row(b_a), row(b_x), row(a_param),
      w_out.astype(BF16))


def _pool_kernel(h_ref, g_ref, win_ref, wg_ref, sc_ref, wout_ref, o_ref, ext_ref, m_ref, *, ts, d, wins):
    s = pl.program_id(1)
    pad = wins[-1]
    gw = d // len(wins)

    @pl.when(s == 0)
    def _():
        ext_ref[0:pad, :] = jnp.zeros((pad, d), F32)

    h = h_ref[...]
    u = _rms(h, g_ref[...]).astype(BF16)
    v = jnp.dot(u, win_ref[...], preferred_element_type=F32)
    ext_ref[pad:pad + ts, :] = v
    n_avail = (s * ts + lax.broadcasted_iota(I32, (ts, 1), 0) + 1).astype(F32)

    cur = ext_ref[...]
    width = 1
    for gi, win in enumerate(wins):
        while width < win:
            cur = cur + pltpu.roll(cur, width, axis=0)
            width *= 2
        lo, hi = gi * gw, (gi + 1) * gw
        mean = cur[pad:, 0:gw] / jnp.minimum(n_avail, float(win))
        pooled = mean - v[:, lo:hi]
        mixed = jnp.dot(pooled.astype(BF16), wg_ref[gi], preferred_element_type=F32)
        m_ref[:, lo:hi] = (mixed * sc_ref[:, lo:hi]).astype(BF16)
        cur = cur[:, gw:]
    ext_ref[0:pad, :] = v[ts - pad:ts]
    o_ref[...] = h + jnp.dot(m_ref[...], wout_ref[...], preferred_element_type=F32)


def _pool_layer(h, batch, seq, g, w_in, w_group, scale, w_out):
    t, d = h.shape
    ts = _pick(seq, 512)
    ns = seq // ts
    pad = POOL_WINDOWS[-1]
    kern = functools.partial(_pool_kernel, ts=ts, d=d, wins=POOL_WINDOWS)
    return pl.pallas_call(
        kern,
        grid=(batch, ns),
        in_specs=[
            pl.BlockSpec((ts, d), lambda b, s: (b * ns + s, 0)),
            _full((1, d)), _full((d, d)), _full(w_group.shape), _full((1, d)), _full((d, d)),
        ],
        out_specs=pl.BlockSpec((ts, d), lambda b, s: (b * ns + s, 0)),
        out_shape=jax.ShapeDtypeStruct((t, d), F32),
        scratch_shapes=[pltpu.VMEM((ts + pad, d), F32), pltpu.VMEM((ts, d), BF16)],
        compiler_params=_params(("arbitrary", "arbitrary")),
        name="pool_mixer",
    )(h, g.reshape(1, d), w_in.astype(BF16), w_group.astype(BF16), scale.reshape(1, d), w_out.astype(BF16))


def _sb_proj_kernel(h_ref, g_ref, wqt_ref, wk_ref, wvt_ref, qt_ref, k_ref, vt_ref):
    u = _rms(h_ref[...], g_ref[...]).astype(BF16)
    nt = (((1,), (1,)), ((), ()))
    qt_ref[...] = lax.dot_general(wqt_ref[...], u, nt, preferred_element_type=F32).astype(BF16)
    k_ref[...] = jnp.dot(u, wk_ref[...], preferred_element_type=F32).astype(BF16)
    vt_ref[...] = lax.dot_general(wvt_ref[...], u, nt, preferred_element_type=F32).astype(BF16)


def _sb_attn_kernel(qt_ref, k_ref, vt_ref, h_ref, wout_ref, o_ref, lb_ref, zs_ref, a_ref, *, bq, dh, nh):
    qb = pl.program_id(1)
    krow = lax.broadcasted_iota(I32, (bq, bq), 0)
    qcol = lax.broadcasted_iota(I32, (bq, bq), 1)
    upper = (qcol >= krow).astype(BF16)
    causal = krow < qcol

    def step(chains, masked, st):
        for c, (hh, j) in enumerate(chains):
            qh = qt_ref[hh * dh:(hh + 1) * dh, :]
            kh = k_ref[pl.ds(j * bq, bq), hh * dh:(hh + 1) * dh]
            z = jnp.dot(kh, qh, preferred_element_type=F32)
            t = jnp.exp2(jnp.abs(z) * (-LOG2E))
            l1m = jnp.minimum(-z, 0.0) - jnp.log(1.0 + t)
            zs_ref[c] = z
            lm = jnp.where(causal, l1m, 0.0) if masked else l1m
            lb_ref[c] = lm.astype(BF16)
        st = list(st)
        for c, (hh, j) in enumerate(chains):
            acc, carry = st[hh]
            suffix = jnp.dot(upper, lb_ref[c], preferred_element_type=F32)
            a = jnp.exp(zs_ref[c] + suffix + carry)
            if masked:
                a = jnp.where(causal, a, 0.0)
            a_ref[c] = a.astype(BF16)
            st[hh] = (acc, carry + suffix[0:1, :])
        for c, (hh, j) in enumerate(chains):
            acc, carry = st[hh]
            vth = vt_ref[hh * dh:(hh + 1) * dh, pl.ds(j * bq, bq)]
            st[hh] = (acc + jnp.dot(vth, a_ref[c], preferred_element_type=F32), carry)
        return st

    heads = range(nh)
    zero = (jnp.zeros((dh, bq), F32), jnp.zeros((1, bq), F32))
    st = step([(hh, qb) for hh in heads], True, [zero] * nh)

    def pair_body(i, st):
        j = qb - 1 - 2 * i
        return step([(hh, j) for hh in heads] + [(hh, j - 1) for hh in heads], False, st)

    def single_body(i, st):
        return step([(hh, 0) for hh in heads], False, st)

    st = lax.fori_loop(0, lax.shift_right_logical(qb, 1), pair_body, st)
    st = lax.fori_loop(0, qb & 1, single_body, st)
    o = jnp.concatenate([st[hh][0].T for hh in heads], axis=1).astype(BF16)
    o_ref[...] = h_ref[...] + jnp.dot(o, wout_ref[...], preferred_element_type=F32)


def _sb_layer(h, batch, seq, g, w_qkv, w_out):
    t, d = h.shape
    dh = d // SB_HEADS
    scale = 1.0 / math.sqrt(dh)
    wqt = (w_qkv[:, :d] * scale).T.astype(BF16)
    wk = w_qkv[:, d:2 * d].astype(BF16)
    wvt = w_qkv[:, 2 * d:].T.astype(BF16)
    tp = _pick(t, 1024)
    qt, k, vt = pl.pallas_call(
        _sb_proj_kernel,
        grid=(t // tp,),
        in_specs=[pl.BlockSpec((tp, d), lambda i: (i, 0)), _full((1, d)), _full((d, d)), _full((d, d)),
                  _full((d, d))],
        out_specs=[pl.BlockSpec((d, tp), lambda i: (0, i)), pl.BlockSpec((tp, d), lambda i: (i, 0)),
                   pl.BlockSpec((d, tp), lambda i: (0, i))],
        out_shape=[jax.ShapeDtypeStruct((d, t), BF16), jax.ShapeDtypeStruct((t, d), BF16),
                   jax.ShapeDtypeStruct((d, t), BF16)],
        compiler_params=_params(("arbitrary",)),
        name="sb_qkv_proj",
    )(h, g.reshape(1, d), wqt, wk, wvt)

    bq = _pick(seq, 256)
    nq = seq // bq
    nh = SB_HEADS
    nchain = 2 * nh
    return pl.pallas_call(
        functools.partial(_sb_attn_kernel, bq=bq, dh=dh, nh=nh),
        grid=(batch, nq),
        in_specs=[
            pl.BlockSpec((d, bq), lambda b, q: (0, b * nq + q)),
            pl.BlockSpec((seq, d), lambda b, q: (b, 0)),
            pl.BlockSpec((d, seq), lambda b, q: (0, b)),
            pl.BlockSpec((bq, d), lambda b, q: (b * nq + q, 0)),
            _full((d, d)),
        ],
        out_specs=pl.BlockSpec((bq, d), lambda b, q: (b * nq + q, 0)),
        out_shape=jax.ShapeDtypeStruct((t, d), F32),
        scratch_shapes=[pltpu.VMEM((nchain, bq, bq), BF16), pltpu.VMEM((nchain, bq, bq), F32),
                        pltpu.VMEM((nchain, bq, bq), BF16)],
        compiler_params=_params(("arbitrary", "arbitrary")),
        name="sb_attention",
    )(qt, k, vt, h, w_out.astype(BF16))


def _route_kernel(h_ref, g_ref, wr_hi_ref, wr_lo_ref, br_ref, idxt_ref, gatet_ref, nch_ref, *, n_exp, top_k, ts):
    u = _rms(h_ref[...], g_ref[...])
    u_hi = u.astype(BF16)
    u_lo = (u - u_hi.astype(F32)).astype(BF16)
    nt_dims = (((1,), (1,)), ((), ()))
    logits = (lax.dot_general(wr_hi_ref[...], u_hi, nt_dims, preferred_element_type=F32)
              + lax.dot_general(wr_lo_ref[...], u_hi, nt_dims, preferred_element_type=F32)
              + lax.dot_general(wr_hi_ref[...], u_lo, nt_dims, preferred_element_type=F32)) + br_ref[...]
    tm = logits.shape[1]
    sub = lax.broadcasted_iota(I32, (n_exp, tm), 0)
    krow = lax.broadcasted_iota(I32, (top_k, tm), 0)
    vals = logits
    top_v, top_i = [], []
    for _ in range(top_k):
        m = jnp.max(vals, axis=0, keepdims=True)
        sel = jnp.min(jnp.where(vals == m, sub, n_exp), axis=0, keepdims=True)
        top_v.append(m)
        top_i.append(sel)
        vals = jnp.where(sub == sel, -jnp.inf, vals)
    exps = [jnp.exp(v - top_v[0]) for v in top_v]
    denom = exps[0]
    for e in exps[1:]:
        denom = denom + e
    idx = jnp.zeros((top_k, tm), I32)
    gates = jnp.zeros((top_k, tm), F32)
    for k in range(top_k):
        idx = jnp.where(krow == k, top_i[k], idx)
        gates = jnp.where(krow == k, exps[k] / denom, gates)
    idxt_ref[...] = idx
    gatet_ref[...] = gates
    chosen = jnp.where(sub == top_i[0], 1.0, 0.0)
    for k in range(1, top_k):
        chosen = jnp.where(sub == top_i[k], 1.0, chosen)
    for s in range(tm // ts):
        cnt = jnp.sum(chosen[:, s * ts:(s + 1) * ts], axis=1, keepdims=True)
        nch_ref[s] = jnp.floor((cnt + (RUN_ALIGN - 1)) * (1.0 / RUN_ALIGN)).astype(I32)


def _rank_kernel(idx_ref, idxt_ref, nchall_ref, pos_ref, post_ref, blr_ref, bgr_ref, slr_ref, sgr_ref, cnt_ref,
                 bexp_ref, nxt_ref, sblk_ref, spair_ref, meta_ref, run_ref, pstart_ref, *, n_exp, top_k, bm, nbp):
    i = pl.program_id(0)
    tm = idx_ref.shape[0]
    cpb = bm // RUN_ALIGN
    idx = idx_ref[...]
    lane = lax.broadcasted_iota(I32, (tm, n_exp), 1)
    ohs = [(idx[:, k:k + 1] == lane).astype(F32) for k in range(top_k)]
    oh = ohs[0]
    for o in ohs[1:]:
        oh = oh + o
    cnt = jnp.sum(oh, axis=0, keepdims=True)
    nch = jnp.floor((cnt + (RUN_ALIGN - 1)) * (1.0 / RUN_ALIGN))
    er = lax.broadcasted_iota(I32, (n_exp, n_exp), 0)
    ec = lax.broadcasted_iota(I32, (n_exp, n_exp), 1)

    def excl_cumsum_row(v):
        v8 = jnp.broadcast_to(v, (SUBLANES, n_exp)).astype(BF16)
        return jnp.dot(v8, (er < ec).astype(BF16), preferred_element_type=F32)[0:1, :]

    @pl.when(i == 0)
    def _():
        tot = jnp.sum(nchall_ref[...].astype(F32), axis=0)
        nb = jnp.floor((tot + (cpb - 1)) * (1.0 / cpb))
        bstart = excl_cumsum_row(nb)
        bend = bstart + nb
        pstart_ref[...] = bstart * cpb
        nst = jnp.floor((nb + 1.0) * 0.5)
        sstart = excl_cumsum_row(nst)
        send = sstart + nst
        step = lax.broadcasted_iota(I32, (nbp, n_exp), 0).astype(F32)
        se = jnp.sum((send <= step).astype(F32), axis=-1, keepdims=True)
        se = jnp.minimum(se, n_exp - 1)
        bexp_ref[...] = se.astype(I32)
        mine = lax.broadcasted_iota(I32, (nbp, n_exp), 1).astype(F32) == se
        pick = lambda v: jnp.sum(jnp.where(mine, v, 0.0), axis=-1, keepdims=True)
        local = step[:, 0:1] - pick(sstart)
        sblk_ref[...] = (pick(bstart) + 2.0 * local).astype(I32)
        spair_ref[...] = jnp.where(2.0 * local + 1.0 < pick(nb), 1.0, 0.0).astype(I32)
        n_used = jnp.sum(nst, axis=-1, keepdims=True)
        run_end = pick(send)
        nx = jnp.minimum(jnp.sum((send <= run_end).astype(F32), axis=-1, keepdims=True), n_exp - 1)
        nxt_ref[...] = jnp.where(run_end < n_used, nx, -1.0).astype(I32)
        last = jnp.where(nb > 0, bend - 1.0, -1.0)
        mrow = lax.broadcasted_iota(I32, (SUBLANES, n_exp), 0)
        meta = jnp.where(mrow == 0, jnp.broadcast_to(n_used, (SUBLANES, n_exp)),
                         jnp.where(mrow == 1, jnp.broadcast_to(last, (SUBLANES, n_exp)), 0.0))
        meta_ref[...] = meta.astype(I32)
        run_ref[...] = jnp.zeros_like(run_ref)

    tr = lax.broadcasted_iota(I32, (tm, tm), 0)
    tc = lax.broadcasted_iota(I32, (tm, tm), 1)
    off = excl_cumsum_row(nch)
    excl = jnp.dot((tc < tr).astype(BF16), oh.astype(BF16), preferred_element_type=F32)
    base = excl + off * RUN_ALIGN
    kcol = lax.broadcasted_iota(I32, (tm, top_k), 1)
    pos = jnp.zeros((tm, top_k), I32)
    for k in range(top_k):
        pk = jnp.sum(ohs[k] * base, axis=-1, keepdims=True).astype(I32)
        pos = jnp.where(kcol == k, pk, pos)
    pos_ref[...] = pos
    idxt = idxt_ref[...]
    sub = lax.broadcasted_iota(I32, (n_exp, tm), 0)
    ohts = [(idxt[k:k + 1, :] == sub).astype(F32) for k in range(top_k)]
    oht = ohts[0]
    for o in ohts[1:]:
        oht = oht + o
    cnt_col = jnp.sum(oht, axis=1, keepdims=True)
    nch_col = jnp.floor((cnt_col + (RUN_ALIGN - 1)) * (1.0 / RUN_ALIGN))
    off_col = jnp.dot((ec < er).astype(BF16), jnp.broadcast_to(nch_col, (n_exp, LANES)).astype(BF16),
                      preferred_element_type=F32)[:, 0:1]
    exclt = jnp.dot(oht.astype(BF16), (tr < tc).astype(BF16), preferred_element_type=F32)
    baset = exclt + off_col * RUN_ALIGN
    krow = lax.broadcasted_iota(I32, (top_k, tm), 0)
    post = jnp.zeros((top_k, tm), I32)
    for k in range(top_k):
        pk = jnp.sum(ohts[k] * baset, axis=0, keepdims=True).astype(I32)
        post = jnp.where(krow == k, pk, post)
    post_ref[...] = post

    gst_row = pstart_ref[...] + run_ref[...]
    gst_col = jnp.sum(jnp.where(er == ec, jnp.broadcast_to(gst_row, (n_exp, n_exp)), 0.0), axis=1, keepdims=True)
    per_big = BIG_CHUNK // RUN_ALIGN
    nbig = jnp.floor(nch_col * (1.0 / per_big))
    nsm = nch_col - nbig * per_big

    def excl_cumsum_col(v):
        return jnp.dot((ec < er).astype(BF16), jnp.broadcast_to(v, (n_exp, LANES)).astype(BF16),
                       preferred_element_type=F32)[:, 0:1]

    def copy_list(count, loc0, glob0, stride, n):
        before = excl_cumsum_col(count)
        j = lax.broadcasted_iota(I32, (n_exp, n), 1).astype(F32)
        owner = jnp.sum(((before + count) <= j).astype(F32), axis=0, keepdims=True)
        mine = lax.broadcasted_iota(I32, (n_exp, n), 0).astype(F32) == owner
        pick = lambda v: jnp.sum(jnp.where(mine, v, 0.0), axis=0, keepdims=True)
        piece = (j[0:1, :] - pick(before)) * stride
        return (pick(loc0) + piece).astype(I32), (pick(glob0) + piece).astype(I32)

    blr_ref[0], bgr_ref[0] = copy_list(nbig, off_col, gst_col, per_big, blr_ref.shape[2])
    slr_ref[0], sgr_ref[0] = copy_list(nsm, off_col + nbig * per_big, gst_col + nbig * per_big, 1, slr_ref.shape[2])
    clane = lax.broadcasted_iota(I32, (1, cnt_ref.shape[2]), 1)
    cnt_ref[0] = jnp.where(clane == 0, jnp.sum(nbig, axis=0, keepdims=True),
                           jnp.where(clane == 1, jnp.sum(nsm, axis=0, keepdims=True), 0.0)).astype(I32)
    run_ref[...] = run_ref[...] + nch


def _copy_units(lists):
    cnt_ref = lists[0]
    return cnt_ref[0, 0, 0] * (BIG_CHUNK // RUN_ALIGN) + cnt_ref[0, 0, 1]


def _start_copies(lists, make_copy):
    cnt_ref, blr_ref, bgr_ref, slr_ref, sgr_ref = lists

    def start(nrows, loc_ref, glob_ref):
        def body(j, c):
            make_copy(nrows, pl.multiple_of(loc_ref[0, 0, j] * RUN_ALIGN, RUN_ALIGN),
                      pl.multiple_of(glob_ref[0, 0, j] * RUN_ALIGN, RUN_ALIGN)).start()
            return c
        return body

    lax.fori_loop(0, cnt_ref[0, 0, 0], start(BIG_CHUNK, blr_ref, bgr_ref), 0)
    lax.fori_loop(0, cnt_ref[0, 0, 1], start(RUN_ALIGN, slr_ref, sgr_ref), 0)


def _wait_units(n, make_copy):
    per_wait = WAIT_CHUNK // RUN_ALIGN

    def wait_big(j, c):
        make_copy(WAIT_CHUNK, 0, 0).wait()
        return c

    def wait_small(j, c):
        make_copy(RUN_ALIGN, 0, 0).wait()
        return c

    nbig = lax.shift_right_logical(n, per_wait.bit_length() - 1)
    lax.fori_loop(0, nbig, wait_big, 0)
    lax.fori_loop(nbig * per_wait, n, wait_small, 0)


def _dispatch_kernel(last_ref, cnt_ref, blr_ref, bgr_ref, slr_ref, sgr_ref, h_ref, g_ref, post_ref, xs_ref, sbuf,
                     zbuf, pending, sem, zsem, *, n_exp, top_k, bm):
    i = pl.program_id(0)
    nt = pl.num_programs(0)
    slot = i % 2
    rows = sbuf.shape[1]

    @pl.when(i == 0)
    def _():
        zbuf[...] = jnp.zeros_like(zbuf)

        def zero_copy(e):
            return pltpu.make_async_copy(zbuf, xs_ref.at[pl.ds(last_ref[e] * bm, bm)], zsem)

        def start(e, c):
            @pl.when(last_ref[e] >= 0)
            def _():
                zero_copy(e).start()
            return c

        def wait(e, c):
            @pl.when(last_ref[e] >= 0)
            def _():
                zero_copy(e).wait()
            return c

        lax.fori_loop(0, n_exp, start, 0)
        lax.fori_loop(0, n_exp, wait, 0)

    u = _rms(h_ref[...], g_ref[...]).astype(BF16)
    tm = u.shape[0]
    post = post_ref[...]
    q = lax.broadcasted_iota(I32, (rows, tm), 0)
    perm = jnp.zeros((rows, tm), F32)
    for k in range(top_k):
        perm = jnp.where(q == post[k:k + 1, :], 1.0, perm)
    sbuf[slot] = _pack_pairs(jnp.dot(perm.astype(BF16), u, preferred_element_type=F32))

    def make_copy(s, nrows, src_row, dst_row):
        return pltpu.make_async_copy(sbuf.at[s, pl.ds(src_row, nrows)], xs_ref.at[pl.ds(dst_row, nrows)], sem.at[s])

    lists = (cnt_ref, blr_ref, bgr_ref, slr_ref, sgr_ref)
    _start_copies(lists, functools.partial(make_copy, slot))
    n = _copy_units(lists)

    @pl.when(i > 0)
    def _():
        _wait_units(pending[0], functools.partial(make_copy, 1 - slot))

    pending[0] = n

    @pl.when(i == nt - 1)
    def _():
        _wait_units(n, functools.partial(make_copy, slot))


def _ffn_kernel(bexp_ref, nxt_ref, sblk_ref, spair_ref, nused_ref, xa_ref, xb_ref, wgu_hbm, bgu_ref, wd_hbm, bd_ref,
                ys_ref, wgu_st, wd_st, wgu_bf, wd_bf, ybuf, pend, sem, ysem, *, f, layer):
    b = pl.program_id(0)
    last_step = pl.num_programs(0) - 1
    bm = xa_ref.shape[0]
    slot = b % 2
    prev = bexp_ref[jnp.maximum(b - 1, 0)]
    active = b < nused_ref[0]
    here = jnp.minimum(b, nused_ref[0] - 1)
    pair = spair_ref[here] == 1
    row0 = pl.multiple_of(sblk_ref[here] * bm, bm)

    def out_copy(s, nrows):
        return pltpu.make_async_copy(ybuf.at[s, pl.ds(0, nrows)], ys_ref.at[pl.ds(row0, nrows)], ysem.at[s])

    def drain(s):
        for nblk in (1, 2):
            @pl.when(pend[s] == nblk)
            def _():
                out_copy(s, nblk * bm).wait()
        pend[s] = 0

    @pl.when(b == 0)
    def _():
        pend[0] = 0
        pend[1] = 0

    def fetch(e):
        return (pltpu.make_async_copy(wgu_hbm.at[layer, e], wgu_st, sem.at[0]),
                pltpu.make_async_copy(wd_hbm.at[layer, e], wd_st, sem.at[1]))

    @pl.when(b == 0)
    def _():
        for c in fetch(bexp_ref[0]):
            c.start()

    @pl.when(active & ((b == 0) | (bexp_ref[b] != prev)))
    def _():
        for c in fetch(bexp_ref[b]):
            c.wait()
        wgu_bf[...] = wgu_st[...].astype(BF16)
        wd_bf[...] = wd_st[...].astype(BF16)

        @pl.when(nxt_ref[b] >= 0)
        def _():
            for c in fetch(nxt_ref[b]):
                c.start()

    def ffn(xw):
        x = _unpack_pairs(xw)
        gu = jnp.dot(x, wgu_bf[...], preferred_element_type=F32) + bgu_ref[0]
        gate = jnp.minimum(gu[:, :f], SWIGLU_LIMIT)
        up = jnp.clip(gu[:, f:], -SWIGLU_LIMIT, SWIGLU_LIMIT)
        glu = gate * _sigmoid(gate * SWIGLU_ALPHA)
        act = ((up + 1.0) * glu).astype(BF16)
        y = jnp.dot(act, wd_bf[...], preferred_element_type=F32) + bd_ref[0]
        return _pack_pairs(y.astype(BF16).astype(F32))

    @pl.when(active)
    def _():
        drain(slot)

    @pl.when(active & pair)
    def _():
        ybuf[slot] = ffn(jnp.concatenate([xa_ref[...], xb_ref[...]], axis=0))
        out_copy(slot, 2 * bm).start()
        pend[slot] = 2

    @pl.when(active & jnp.logical_not(pair))
    def _():
        ybuf[slot, 0:bm] = ffn(xa_ref[...])
        out_copy(slot, bm).start()
        pend[slot] = 1

    @pl.when(b == last_step)
    def _():
        drain(0)
        drain(1)


def _combine_kernel(*refs, top_k, final):
    lists, lists_nx = refs[0:5], refs[5:10]
    h_ref, gate_ref, pos_ref, ys_ref, fg_ref, o_ref, ybuf, sem = refs[10:]
    i = pl.program_id(0)
    nt = pl.num_programs(0)
    slot = i % 2
    rows = ybuf.shape[1]

    def make_copy(s, nrows, dst_row, src_row):
        return pltpu.make_async_copy(ys_ref.at[pl.ds(src_row, nrows)], ybuf.at[s, pl.ds(dst_row, nrows)], sem.at[s])

    @pl.when(i == 0)
    def _():
        ybuf[...] = jnp.zeros_like(ybuf)
        _start_copies(lists, functools.partial(make_copy, 0))

    @pl.when(i + 1 < nt)
    def _():
        _start_copies(lists_nx, functools.partial(make_copy, 1 - slot))

    _wait_units(_copy_units(lists), functools.partial(make_copy, slot))

    gates = gate_ref[...]
    pos = pos_ref[...]
    tm, d = h_ref.shape
    q = lax.broadcasted_iota(I32, (tm, rows), 1)
    wsel = jnp.zeros((tm, rows), F32)
    for k in range(top_k):
        wsel = jnp.where(q == pos[:, k:k + 1], gates[:, k:k + 1], wsel)
    out = h_ref[...] + jnp.dot(wsel.astype(BF16), _unpack_pairs(ybuf[slot]), preferred_element_type=F32)
    if final:
        out = _rms(out, fg_ref[...])
    o_ref[...] = out


def _moe_layer(h, g, w_router, b_router, w_gate_up, b_gate_up, w_down, b_down, final_g, final, layer):
    t, d = h.shape
    n_exp = w_router.shape[1]
    f = w_down.shape[2]
    top_k = TOP_K
    dp = d // 2
    bm = MOE_BLOCK
    ts = _pick(t, MOE_SORT_TILE)
    nt = t // ts
    srows = -(-(ts * top_k + n_exp * RUN_ALIGN) // LANES) * LANES
    n_blocks = -(-(t * top_k + nt * n_exp * (RUN_ALIGN - 1)) // bm) + n_exp
    n_slots = n_blocks * bm
    n_steps = (n_blocks + n_exp + 1) // 2
    nbp = -(-n_steps // SUBLANES) * SUBLANES

    tm = ts * max(1, _pick(nt, 8))
    wrt = w_router.T
    wr_hi = wrt.astype(BF16)
    wr_lo = (wrt - wr_hi.astype(F32)).astype(BF16)
    idxt, gatest, ncht = pl.pallas_call(
        functools.partial(_route_kernel, n_exp=n_exp, top_k=top_k, ts=ts),
        grid=(t // tm,),
        in_specs=[pl.BlockSpec((tm, d), lambda i: (i, 0)), _full((1, d)), _full((n_exp, d)),
                  _full((n_exp, d)), _full((n_exp, 1))],
        out_specs=[pl.BlockSpec((top_k, tm), lambda i: (0, i)), pl.BlockSpec((top_k, tm), lambda i: (0, i)),
                   pl.BlockSpec((tm // ts, n_exp, 1), lambda i: (i, 0, 0))],
        out_shape=[jax.ShapeDtypeStruct((top_k, t), I32), jax.ShapeDtypeStruct((top_k, t), F32),
                   jax.ShapeDtypeStruct((nt, n_exp, 1), I32)],
        compiler_params=_params(("arbitrary",)),
        name="moe_route",
    )(h, g.reshape(1, d), wr_hi, wr_lo, b_router.reshape(n_exp, 1))
    idx, gates, nch = idxt.T, gatest.T, ncht.reshape(nt, 1, n_exp)

    per_big = BIG_CHUNK // RUN_ALIGN
    n_big = srows // BIG_CHUNK
    n_small = n_exp * (per_big - 1)
    list_len = (n_big, n_big, n_small, n_small, SUBLANES)
    rank_out = pl.pallas_call(
        functools.partial(_rank_kernel, n_exp=n_exp, top_k=top_k, bm=bm, nbp=nbp),
        grid=(nt,),
        in_specs=[pl.BlockSpec((ts, top_k), lambda i: (i, 0)), pl.BlockSpec((top_k, ts), lambda i: (0, i)),
                  _full((nt, 1, n_exp))],
        out_specs=[pl.BlockSpec((ts, top_k), lambda i: (i, 0)), pl.BlockSpec((top_k, ts), lambda i: (0, i))]
        + [pl.BlockSpec((1, 1, r), lambda i: (i, 0, 0)) for r in list_len]
        + [_full((nbp, 1))] * 4 + [_full((SUBLANES, n_exp))],
        out_shape=[jax.ShapeDtypeStruct((t, top_k), I32), jax.ShapeDtypeStruct((top_k, t), I32)]
        + [jax.ShapeDtypeStruct((nt, 1, r), I32) for r in list_len]
        + [jax.ShapeDtypeStruct((nbp, 1), I32)] * 4 + [jax.ShapeDtypeStruct((SUBLANES, n_exp), I32)],
        scratch_shapes=[pltpu.VMEM((1, n_exp), F32), pltpu.VMEM((1, n_exp), F32)],
        compiler_params=_params(("arbitrary",)),
        name="moe_rank",
    )(idx, idxt, nch)
    pos, post = rank_out[0], rank_out[1]
    blr, bgr, slr, sgr, cnt = rank_out[2:7]
    bexp, nxt, sblk, spair = (a.reshape(nbp) for a in rank_out[7:11])
    meta = rank_out[11]
    lists = [cnt, blr, bgr, slr, sgr]
    n_used = meta[0, 0:1]
    last_blk = meta[1]

    def smem_lists(step=0):
        return [pl.BlockSpec((1, 1, a.shape[2]), lambda i, *_: (jnp.minimum(i + step, nt - 1), 0, 0),
                             memory_space=pltpu.SMEM) for a in lists]

    xs = pl.pallas_call(
        functools.partial(_dispatch_kernel, n_exp=n_exp, top_k=top_k, bm=bm),
        grid_spec=pltpu.PrefetchScalarGridSpec(
            num_scalar_prefetch=1,
            grid=(nt,),
            in_specs=smem_lists() + [
                pl.BlockSpec((ts, d), lambda i, last: (i, 0)),
                pl.BlockSpec((1, d), lambda i, last: (0, 0)),
                pl.BlockSpec((top_k, ts), lambda i, last: (0, i)),
            ],
            out_specs=pl.BlockSpec(memory_space=pl.ANY),
            scratch_shapes=[pltpu.VMEM((2, srows, dp), U32), pltpu.VMEM((bm, dp), U32), pltpu.SMEM((1,), I32),
                            pltpu.SemaphoreType.DMA((2,)), pltpu.SemaphoreType.DMA(())],
        ),
        out_shape=jax.ShapeDtypeStruct((n_slots, dp), U32),
        compiler_params=_params(("arbitrary",)),
        name="moe_dispatch",
    )(last_blk, *lists, h, g.reshape(1, d), post)

    def stp(s, nu):
        return jnp.minimum(s, nu[0] - 1)

    def blk_a(s, be, nx, sb, sp, nu):
        return (sb[stp(s, nu)], 0)

    def blk_b(s, be, nx, sb, sp, nu):
        return (sb[stp(s, nu)] + sp[stp(s, nu)], 0)

    def of_expert(shape):
        return pl.BlockSpec(shape, lambda s, be, nx, sb, sp, nu: (be[stp(s, nu)],) + (0,) * (len(shape) - 1))

    ys = pl.pallas_call(
        functools.partial(_ffn_kernel, f=f, layer=layer),
        grid_spec=pltpu.PrefetchScalarGridSpec(
            num_scalar_prefetch=5,
            grid=(n_steps,),
            in_specs=[
                pl.BlockSpec((bm, dp), blk_a), pl.BlockSpec((bm, dp), blk_b),
                pl.BlockSpec(memory_space=pl.ANY), of_expert((1, 1, 2 * f)),
                pl.BlockSpec(memory_space=pl.ANY), of_expert((1, 1, d)),
            ],
            out_specs=pl.BlockSpec(memory_space=pl.ANY),
            scratch_shapes=[pltpu.VMEM((d, 2 * f), F32), pltpu.VMEM((f, d), F32),
                            pltpu.VMEM((d, 2 * f), BF16), pltpu.VMEM((f, d), BF16),
                            pltpu.VMEM((2, 2 * bm, dp), U32), pltpu.SMEM((2,), I32),
                            pltpu.SemaphoreType.DMA((2,)), pltpu.SemaphoreType.DMA((2,))],
        ),
        out_shape=jax.ShapeDtypeStruct((n_slots, dp), U32),
        compiler_params=_params(("arbitrary",)),
        name="moe_ffn",
    )(bexp, nxt, sblk, spair, n_used, xs, xs, w_gate_up, b_gate_up.reshape(n_exp, 1, 2 * f), w_down,
      b_down.reshape(n_exp, 1, d))

    return pl.pallas_call(
        functools.partial(_combine_kernel, top_k=top_k, final=final),
        grid=(nt,),
        in_specs=smem_lists() + smem_lists(step=1) + [
            pl.BlockSpec((ts, d), lambda i: (i, 0)),
            pl.BlockSpec((ts, top_k), lambda i: (i, 0)),
            pl.BlockSpec((ts, top_k), lambda i: (i, 0)),
            pl.BlockSpec(memory_space=pl.ANY),
            _full((1, d)),
        ],
        out_specs=pl.BlockSpec((ts, d), lambda i: (i, 0)),
        out_shape=jax.ShapeDtypeStruct((t, d), F32),
        scratch_shapes=[pltpu.VMEM((2, srows, dp), U32), pltpu.SemaphoreType.DMA((2,))],
        compiler_params=_params(("arbitrary",)),
        name="moe_combine",
    )(*lists, *lists, h, gates, pos, ys, final_g.reshape(1, d))


def kernel(x, mix_norm, ffn_norm, final_norm, lru_w_in, lru_conv_w, lru_conv_b, lru_w_a, lru_b_a, lru_w_x, lru_b_x, lru_a_param, lru_w_out, pool_w_in, pool_w_group, pool_scale, pool_w_out, sb_w_qkv, sb_w_out, moe_w_router, moe_b_router, moe_w_gate_up, moe_b_gate_up, moe_w_down, moe_b_down):
    batch, seq, d = x.shape
    depth = mix_norm.shape[0]
    h = x.reshape(batch * seq, d)
    for layer in range(depth):
        kind = layer % N_MIXERS
        slot = layer // N_MIXERS
        if kind == 0:
            h = _lru_layer(h, batch, seq, mix_norm[layer], lru_w_in[slot], lru_conv_w[slot], lru_conv_b[slot],
                           lru_w_a[slot], lru_b_a[slot], lru_w_x[slot], lru_b_x[slot], lru_a_param[slot],
                           lru_w_out[slot])
        elif kind == 1:
            h = _pool_layer(h, batch, seq, mix_norm[layer], pool_w_in[slot], pool_w_group[slot], pool_scale[slot],
                            pool_w_out[slot])
        else:
            h = _sb_layer(h, batch, seq, mix_norm[layer], sb_w_qkv[slot], sb_w_out[slot])
        h = _moe_layer(h, ffn_norm[layer], moe_w_router[layer], moe_b_router[layer], moe_w_gate_up,
                       moe_b_gate_up[layer], moe_w_down, moe_b_down[layer], final_norm,
                       final=(layer == depth - 1), layer=layer)
    return h.reshape(batch, seq, d)
```

```python
import functools
import math

import jax
import jax.numpy as jnp
from jax import lax
from jax.experimental import pallas as pl
from jax.experimental.pallas import tpu as pltpu

F32 = jnp.float32
BF16 = jnp.bfloat16
I32 = jnp.int32
U32 = jnp.uint32

RMS_EPS = 1e-6
LRU_C = 8.0
POOL_WINDOWS = (2, 4, 8, 16)
SB_HEADS = 16
LOG2E = 1.4426950408889634
TOP_K = 4
SWIGLU_LIMIT = 7.0
SWIGLU_ALPHA = 1.702
N_MIXERS = 3

V7X_VMEM_BYTES = 64 * 1024 * 1024
VMEM_LIMIT_BYTES = V7X_VMEM_BYTES - 8 * 1024 * 1024
SUBLANES = 8
LANES = 128

MOE_BLOCK = 256
MOE_SORT_TILE = 256
RUN_ALIGN = SUBLANES
BIG_CHUNK = 4 * RUN_ALIGN
WAIT_CHUNK = 16 * RUN_ALIGN


def _params(semantics, vmem=VMEM_LIMIT_BYTES):
    return pltpu.CompilerParams(dimension_semantics=semantics, vmem_limit_bytes=vmem)


def _pick(n, pref):
    t = min(n, pref)
    while n % t:
        t //= 2
    return t


def _rms(h, g):
    ms = jnp.mean(h * h, axis=-1, keepdims=True)
    return h * lax.rsqrt(ms + RMS_EPS) * g


def _sigmoid(x):
    return 1.0 / (1.0 + jnp.exp(-x))


def _softplus(x):
    return jnp.maximum(x, 0.0) + jnp.log(1.0 + jnp.exp(-jnp.abs(x)))


def _gelu_tanh(x):
    c = math.sqrt(2.0 / math.pi)
    return 0.5 * x * (1.0 + jnp.tanh(c * (x + 0.044715 * (x * x * x))))


def _full(shape):
    n = len(shape)
    return pl.BlockSpec(shape, lambda *_: (0,) * n)


def _pack_pairs(x):
    n = x.shape[1] // 2
    lo = lax.bitcast_convert_type(x[:, :n], U32)
    hi = lax.bitcast_convert_type(x[:, n:], U32)
    return lax.shift_right_logical(lo, jnp.uint32(16)) | (hi & jnp.uint32(0xFFFF0000))


def _unpack_pairs(w):
    lo = lax.bitcast_convert_type(lax.shift_left(w, jnp.uint32(16)), F32)
    hi = lax.bitcast_convert_type(w & jnp.uint32(0xFFFF0000), F32)
    return jnp.concatenate([lo.astype(BF16), hi.astype(BF16)], axis=1)


def _lru_kernel(h_ref, g_ref, win_ref, cw_ref, cb_ref, wax_ref, ba_ref, bx_ref, ap_ref, wout_ref,
                o_ref, hcar_ref, ext_ref, y_ref, *, ts, w, bw):
    s = pl.program_id(1)

    @pl.when(s == 0)
    def _():
        hcar_ref[...] = jnp.zeros_like(hcar_ref)
        ext_ref[0:SUBLANES, :] = jnp.zeros((SUBLANES, w), F32)

    h = h_ref[...]
    u = _rms(h, g_ref[...]).astype(BF16)
    proj = jnp.dot(u, win_ref[...], preferred_element_type=F32)
    row8 = lax.broadcasted_iota(I32, (SUBLANES, bw), 0)
    kw = cw_ref.shape[0]

    for c in range(w // bw):
        lo, hi = c * bw, (c + 1) * bw
        gate = proj[:, lo:hi]
        xb = proj[:, w + lo:w + hi]
        ext_ref[SUBLANES:SUBLANES + ts, lo:hi] = xb
        xc = cb_ref[:, lo:hi] + xb * cw_ref[kw - 1:kw, lo:hi]
        for j in range(1, kw):
            xc = xc + ext_ref[SUBLANES - j:SUBLANES - j + ts, lo:hi] * cw_ref[kw - 1 - j:kw - j, lo:hi]
        ext_ref[0:SUBLANES, lo:hi] = xb[ts - SUBLANES:ts]

        res = jnp.dot(xc.astype(BF16), wax_ref[c], preferred_element_type=F32)
        r = _sigmoid(res[:, :bw] + ba_ref[:, lo:hi])
        i = _sigmoid(res[:, bw:] + bx_ref[:, lo:hi])
        log_a = (-LRU_C * _softplus(-ap_ref[:, lo:hi])) * r
        a = jnp.exp(log_a)
        b = jnp.sqrt(-jnp.tanh(log_a) * (a * a + 1.0)) * (i * xc)
        hprev = hcar_ref[:, lo:hi]
        groups = []
        for gi in range(ts // SUBLANES):
            ag = a[gi * SUBLANES:(gi + 1) * SUBLANES]
            bg = b[gi * SUBLANES:(gi + 1) * SUBLANES]
            d = 1
            while d < SUBLANES:
                keep = row8 >= d
                a_sh = jnp.where(keep, pltpu.roll(ag, d, axis=0), 1.0)
                b_sh = jnp.where(keep, pltpu.roll(bg, d, axis=0), 0.0)
                bg = ag * b_sh + bg
                ag = ag * a_sh
                d *= 2
            hg = ag * hprev + bg
            hprev = hg[SUBLANES - 1:SUBLANES]
            groups.append(hg)
        hs = jnp.concatenate(groups, axis=0)
        hcar_ref[:, lo:hi] = hprev
        y_ref[:, lo:hi] = (hs * _gelu_tanh(gate)).astype(BF16)

    o_ref[...] = h + jnp.dot(y_ref[...], wout_ref[...], preferred_element_type=F32)


def _lru_layer(h, batch, seq, g, w_in, conv_w, conv_b, w_a, b_a, w_x, b_x, a_param, w_out):
    t, d = h.shape
    w = w_in.shape[1] // 2
    nblk, bw, _ = w_a.shape
    ts = _pick(seq, 512)
    ns = seq // ts
    wax = jnp.concatenate([w_a, w_x], axis=-1).astype(BF16)
    row = lambda v: v.reshape(1, -1)
    kern = functools.partial(_lru_kernel, ts=ts, w=w, bw=bw)
    return pl.pallas_call(
        kern,
        grid=(batch, ns),
        in_specs=[
            pl.BlockSpec((ts, d), lambda b, s: (b * ns + s, 0)),
            _full((1, d)), _full((d, 2 * w)), _full(conv_w.shape), _full((1, w)),
            _full(wax.shape), _full((1, w)), _full((1, w)), _full((1, w)), _full((w, d)),
        ],
        out_specs=pl.BlockSpec((ts, d), lambda b, s: (b * ns + s, 0)),
        out_shape=jax.ShapeDtypeStruct((t, d), F32),
        scratch_shapes=[pltpu.VMEM((1, w), F32), pltpu.VMEM((SUBLANES + ts, w), F32), pltpu.VMEM((ts, w), BF16)],
        compiler_params=_params(("arbitrary", "arbitrary")),
        name="lru_mixer",
    )(h, row(g), w_in.astype(BF16), conv_w, row(conv_b), wax, row(b_a), row(b_x), row(a_param),
      w_out.astype(BF16))


def _pool_kernel(h_ref, g_ref, win_ref, wg_ref, sc_ref, wout_ref, o_ref, ext_ref, m_ref, *, ts, d, wins):
    s = pl.program_id(1)
    pad = wins[-1]
    gw = d // len(wins)

    @pl.when(s == 0)
    def _():
        ext_ref[0:pad, :] = jnp.zeros((pad, d), F32)

    h = h_ref[...]
    u = _rms(h, g_ref[...]).astype(BF16)
    v = jnp.dot(u, win_ref[...], preferred_element_type=F32)
    ext_ref[pad:pad + ts, :] = v
    n_avail = (s * ts + lax.broadcasted_iota(I32, (ts, 1), 0) + 1).astype(F32)

    cur = ext_ref[...]
    width = 1
    for gi, win in enumerate(wins):
        while width < win:
            cur = cur + pltpu.roll(cur, width, axis=0)
            width *= 2
        lo, hi = gi * gw, (gi + 1) * gw
        mean = cur[pad:, 0:gw] / jnp.minimum(n_avail, float(win))
        pooled = mean - v[:, lo:hi]
        mixed = jnp.dot(pooled.astype(BF16), wg_ref[gi], preferred_element_type=F32)
        m_ref[:, lo:hi] = (mixed * sc_ref[:, lo:hi]).astype(BF16)
        cur = cur[:, gw:]
    ext_ref[0:pad, :] = v[ts - pad:ts]
    o_ref[...] = h + jnp.dot(m_ref[...], wout_ref[...], preferred_element_type=F32)


def _pool_layer(h, batch, seq, g, w_in, w_group, scale, w_out):
    t, d = h.shape
    ts = _pick(seq, 512)
    ns = seq // ts
    pad = POOL_WINDOWS[-1]
    kern = functools.partial(_pool_kernel, ts=ts, d=d, wins=POOL_WINDOWS)
    return pl.pallas_call(
        kern,
        grid=(batch, ns),
        in_specs=[
            pl.BlockSpec((ts, d), lambda b, s: (b * ns + s, 0)),
            _full((1, d)), _full((d, d)), _full(w_group.shape), _full((1, d)), _full((d, d)),
        ],
        out_specs=pl.BlockSpec((ts, d), lambda b, s: (b * ns + s, 0)),
        out_shape=jax.ShapeDtypeStruct((t, d), F32),
        scratch_shapes=[pltpu.VMEM((ts + pad, d), F32), pltpu.VMEM((ts, d), BF16)],
        compiler_params=_params(("arbitrary", "arbitrary")),
        name="pool_mixer",
    )(h, g.reshape(1, d), w_in.astype(BF16), w_group.astype(BF16), scale.reshape(1, d), w_out.astype(BF16))


def _sb_proj_kernel(h_ref, g_ref, wqt_ref, wk_ref, wvt_ref, qt_ref, k_ref, vt_ref):
    u = _rms(h_ref[...], g_ref[...]).astype(BF16)
    nt = (((1,), (1,)), ((), ()))
    qt_ref[...] = lax.dot_general(wqt_ref[...], u, nt, preferred_element_type=F32).astype(BF16)
    k_ref[...] = jnp.dot(u, wk_ref[...], preferred_element_type=F32).astype(BF16)
    vt_ref[...] = lax.dot_general(wvt_ref[...], u, nt, preferred_element_type=F32).astype(BF16)


def _sb_attn_kernel(qt_ref, k_ref, vt_ref, h_ref, wout_ref, o_ref, lb_ref, zs_ref, a_ref, *, bq, dh, nh):
    qb = pl.program_id(1)
    krow = lax.broadcasted_iota(I32, (bq, bq), 0)
    qcol = lax.broadcasted_iota(I32, (bq, bq), 1)
    upper = (qcol >= krow).astype(BF16)
    causal = krow < qcol

    def step(chains, masked, st):
        for c, (hh, j) in enumerate(chains):
            qh = qt_ref[hh * dh:(hh + 1) * dh, :]
            kh = k_ref[pl.ds(j * bq, bq), hh * dh:(hh + 1) * dh]
            z = jnp.dot(kh, qh, preferred_element_type=F32)
            t = jnp.exp2(jnp.abs(z) * (-LOG2E))
            l1m = jnp.minimum(-z, 0.0) - jnp.log(1.0 + t)
            zs_ref[c] = z
            lm = jnp.where(causal, l1m, 0.0) if masked else l1m
            lb_ref[c] = lm.astype(BF16)
        st = list(st)
        for c, (hh, j) in enumerate(chains):
            acc, carry = st[hh]
            suffix = jnp.dot(upper, lb_ref[c], preferred_element_type=F32)
            a = jnp.exp(zs_ref[c] + suffix + carry)
            if masked:
                a = jnp.where(causal, a, 0.0)
            a_ref[c] = a.astype(BF16)
            st[hh] = (acc, carry + suffix[0:1, :])
        for c, (hh, j) in enumerate(chains):
            acc, carry = st[hh]
            vth = vt_ref[hh * dh:(hh + 1) * dh, pl.ds(j * bq, bq)]
            st[hh] = (acc + jnp.dot(vth, a_ref[c], preferred_element_type=F32), carry)
        return st

    heads = range(nh)
    zero = (jnp.zeros((dh, bq), F32), jnp.zeros((1, bq), F32))
    st = step([(hh, qb) for hh in heads], True, [zero] * nh)

    def pair_body(i, st):
        j = qb - 1 - 2 * i
        return step([(hh, j) for hh in heads] + [(hh, j - 1) for hh in heads], False, st)

    def single_body(i, st):
        return step([(hh, 0) for hh in heads], False, st)

    st = lax.fori_loop(0, lax.shift_right_logical(qb, 1), pair_body, st)
    st = lax.fori_loop(0, qb & 1, single_body, st)
    o = jnp.concatenate([st[hh][0].T for hh in heads], axis=1).astype(BF16)
    o_ref[...] = h_ref[...] + jnp.dot(o, wout_ref[...], preferred_element_type=F32)


def _sb_layer(h, batch, seq, g, w_qkv, w_out):
    t, d = h.shape
    dh = d // SB_HEADS
    scale = 1.0 / math.sqrt(dh)
    wqt = (w_qkv[:, :d] * scale).T.astype(BF16)
    wk = w_qkv[:, d:2 * d].astype(BF16)
    wvt = w_qkv[:, 2 * d:].T.astype(BF16)
    tp = _pick(t, 512)
    qt, k, vt = pl.pallas_call(
        _sb_proj_kernel,
        grid=(t // tp,),
        in_specs=[pl.BlockSpec((tp, d), lambda i: (i, 0)), _full((1, d)), _full((d, d)), _full((d, d)),
                  _full((d, d))],
        out_specs=[pl.BlockSpec((d, tp), lambda i: (0, i)), pl.BlockSpec((tp, d), lambda i: (i, 0)),
                   pl.BlockSpec((d, tp), lambda i: (0, i))],
        out_shape=[jax.ShapeDtypeStruct((d, t), BF16), jax.ShapeDtypeStruct((t, d), BF16),
                   jax.ShapeDtypeStruct((d, t), BF16)],
        compiler_params=_params(("arbitrary",)),
        name="sb_qkv_proj",
    )(h, g.reshape(1, d), wqt, wk, wvt)

    bq = _pick(seq, 256)
    nq = seq // bq
    nh = SB_HEADS
    nchain = 2 * nh
    return pl.pallas_call(
        functools.partial(_sb_attn_kernel, bq=bq, dh=dh, nh=nh),
        grid=(batch, nq),
        in_specs=[
            pl.BlockSpec((d, bq), lambda b, q: (0, b * nq + q)),
            pl.BlockSpec((seq, d), lambda b, q: (b, 0)),
            pl.BlockSpec((d, seq), lambda b, q: (0, b)),
            pl.BlockSpec((bq, d), lambda b, q: (b * nq + q, 0)),
            _full((d, d)),
        ],
        out_specs=pl.BlockSpec((bq, d), lambda b, q: (b * nq + q, 0)),
        out_shape=jax.ShapeDtypeStruct((t, d), F32),
        scratch_shapes=[pltpu.VMEM((nchain, bq, bq), BF16), pltpu.VMEM((nchain, bq, bq), F32),
                        pltpu.VMEM((nchain, bq, bq), BF16)],
        compiler_params=_params(("arbitrary", "arbitrary")),
        name="sb_attention",
    )(qt, k, vt, h, w_out.astype(BF16))


def _route_kernel(h_ref, g_ref, wr_hi_ref, wr_lo_ref, br_ref, idxt_ref, gatet_ref, nch_ref, *, n_exp, top_k, ts):
    u = _rms(h_ref[...], g_ref[...])
    u_hi = u.astype(BF16)
    u_lo = (u - u_hi.astype(F32)).astype(BF16)
    nt_dims = (((1,), (1,)), ((), ()))
    logits = (lax.dot_general(wr_hi_ref[...], u_hi, nt_dims, preferred_element_type=F32)
              + lax.dot_general(wr_lo_ref[...], u_hi, nt_dims, preferred_element_type=F32)
              + lax.dot_general(wr_hi_ref[...], u_lo, nt_dims, preferred_element_type=F32)) + br_ref[...]
    tm = logits.shape[1]
    sub = lax.broadcasted_iota(I32, (n_exp, tm), 0)
    krow = lax.broadcasted_iota(I32, (top_k, tm), 0)
    vals = logits
    top_v, top_i = [], []
    for _ in range(top_k):
        m = jnp.max(vals, axis=0, keepdims=True)
        sel = jnp.min(jnp.where(vals == m, sub, n_exp), axis=0, keepdims=True)
        top_v.append(m)
        top_i.append(sel)
        vals = jnp.where(sub == sel, -jnp.inf, vals)
    exps = [jnp.exp(v - top_v[0]) for v in top_v]
    denom = exps[0]
    for e in exps[1:]:
        denom = denom + e
    idx = jnp.zeros((top_k, tm), I32)
    gates = jnp.zeros((top_k, tm), F32)
    for k in range(top_k):
        idx = jnp.where(krow == k, top_i[k], idx)
        gates = jnp.where(krow == k, exps[k] / denom, gates)
    idxt_ref[...] = idx
    gatet_ref[...] = gates
    chosen = jnp.where(sub == top_i[0], 1.0, 0.0)
    for k in range(1, top_k):
        chosen = jnp.where(sub == top_i[k], 1.0, chosen)
    for s in range(tm // ts):
        cnt = jnp.sum(chosen[:, s * ts:(s + 1) * ts], axis=1, keepdims=True)
        nch_ref[s] = jnp.floor((cnt + (RUN_ALIGN - 1)) * (1.0 / RUN_ALIGN)).astype(I32)


def _rank_kernel(idx_ref, idxt_ref, nchall_ref, pos_ref, post_ref, blr_ref, bgr_ref, slr_ref, sgr_ref, cnt_ref,
                 bexp_ref, nxt_ref, sblk_ref, spair_ref, meta_ref, run_ref, pstart_ref, *, n_exp, top_k, bm, nbp):
    i = pl.program_id(0)
    tm = idx_ref.shape[0]
    cpb = bm // RUN_ALIGN
    idx = idx_ref[...]
    lane = lax.broadcasted_iota(I32, (tm, n_exp), 1)
    ohs = [(idx[:, k:k + 1] == lane).astype(F32) for k in range(top_k)]
    oh = ohs[0]
    for o in ohs[1:]:
        oh = oh + o
    cnt = jnp.sum(oh, axis=0, keepdims=True)
    nch = jnp.floor((cnt + (RUN_ALIGN - 1)) * (1.0 / RUN_ALIGN))
    er = lax.broadcasted_iota(I32, (n_exp, n_exp), 0)
    ec = lax.broadcasted_iota(I32, (n_exp, n_exp), 1)

    def excl_cumsum_row(v):
        v8 = jnp.broadcast_to(v, (SUBLANES, n_exp)).astype(BF16)
        return jnp.dot(v8, (er < ec).astype(BF16), preferred_element_type=F32)[0:1, :]

    @pl.when(i == 0)
    def _():
        tot = jnp.sum(nchall_ref[...].astype(F32), axis=0)
        nb = jnp.floor((tot + (cpb - 1)) * (1.0 / cpb))
        bstart = excl_cumsum_row(nb)
        bend = bstart + nb
        pstart_ref[...] = bstart * cpb
        nst = jnp.floor((nb + 1.0) * 0.5)
        sstart = excl_cumsum_row(nst)
        send = sstart + nst
        step = lax.broadcasted_iota(I32, (nbp, n_exp), 0).astype(F32)
        se = jnp.sum((send <= step).astype(F32), axis=-1, keepdims=True)
        se = jnp.minimum(se, n_exp - 1)
        bexp_ref[...] = se.astype(I32)
        mine = lax.broadcasted_iota(I32, (nbp, n_exp), 1).astype(F32) == se
        pick = lambda v: jnp.sum(jnp.where(mine, v, 0.0), axis=-1, keepdims=True)
        local = step[:, 0:1] - pick(sstart)
        sblk_ref[...] = (pick(bstart) + 2.0 * local).astype(I32)
        spair_ref[...] = jnp.where(2.0 * local + 1.0 < pick(nb), 1.0, 0.0).astype(I32)
        n_used = jnp.sum(nst, axis=-1, keepdims=True)
        run_end = pick(send)
        nx = jnp.minimum(jnp.sum((send <= run_end).astype(F32), axis=-1, keepdims=True), n_exp - 1)
        nxt_ref[...] = jnp.where(run_end < n_used, nx, -1.0).astype(I32)
        last = jnp.where(nb > 0, bend - 1.0, -1.0)
        mrow = lax.broadcasted_iota(I32, (SUBLANES, n_exp), 0)
        meta = jnp.where(mrow == 0, jnp.broadcast_to(n_used, (SUBLANES, n_exp)),
                         jnp.where(mrow == 1, jnp.broadcast_to(last, (SUBLANES, n_exp)), 0.0))
        meta_ref[...] = meta.astype(I32)
        run_ref[...] = jnp.zeros_like(run_ref)

    tr = lax.broadcasted_iota(I32, (tm, tm), 0)
    tc = lax.broadcasted_iota(I32, (tm, tm), 1)
    off = excl_cumsum_row(nch)
    excl = jnp.dot((tc < tr).astype(BF16), oh.astype(BF16), preferred_element_type=F32)
    base = excl + off * RUN_ALIGN
    kcol = lax.broadcasted_iota(I32, (tm, top_k), 1)
    pos = jnp.zeros((tm, top_k), I32)
    for k in range(top_k):
        pk = jnp.sum(ohs[k] * base, axis=-1, keepdims=True).astype(I32)
        pos = jnp.where(kcol == k, pk, pos)
    pos_ref[...] = pos
    idxt = idxt_ref[...]
    sub = lax.broadcasted_iota(I32, (n_exp, tm), 0)
    ohts = [(idxt[k:k + 1, :] == sub).astype(F32) for k in range(top_k)]
    oht = ohts[0]
    for o in ohts[1:]:
        oht = oht + o
    cnt_col = jnp.sum(oht, axis=1, keepdims=True)
    nch_col = jnp.floor((cnt_col + (RUN_ALIGN - 1)) * (1.0 / RUN_ALIGN))
    off_col = jnp.dot((ec < er).astype(BF16), jnp.broadcast_to(nch_col, (n_exp, LANES)).astype(BF16),
                      preferred_element_type=F32)[:, 0:1]
    exclt = jnp.dot(oht.astype(BF16), (tr < tc).astype(BF16), preferred_element_type=F32)
    baset = exclt + off_col * RUN_ALIGN
    krow = lax.broadcasted_iota(I32, (top_k, tm), 0)
    post = jnp.zeros((top_k, tm), I32)
    for k in range(top_k):
        pk = jnp.sum(ohts[k] * baset, axis=0, keepdims=True).astype(I32)
        post = jnp.where(krow == k, pk, post)
    post_ref[...] = post

    gst_row = pstart_ref[...] + run_ref[...]
    gst_col = jnp.sum(jnp.where(er == ec, jnp.broadcast_to(gst_row, (n_exp, n_exp)), 0.0), axis=1, keepdims=True)
    per_big = BIG_CHUNK // RUN_ALIGN
    nbig = jnp.floor(nch_col * (1.0 / per_big))
    nsm = nch_col - nbig * per_big

    def excl_cumsum_col(v):
        return jnp.dot((ec < er).astype(BF16), jnp.broadcast_to(v, (n_exp, LANES)).astype(BF16),
                       preferred_element_type=F32)[:, 0:1]

    def copy_list(count, loc0, glob0, stride, n):
        before = excl_cumsum_col(count)
        j = lax.broadcasted_iota(I32, (n_exp, n), 1).astype(F32)
        owner = jnp.sum(((before + count) <= j).astype(F32), axis=0, keepdims=True)
        mine = lax.broadcasted_iota(I32, (n_exp, n), 0).astype(F32) == owner
        pick = lambda v: jnp.sum(jnp.where(mine, v, 0.0), axis=0, keepdims=True)
        piece = (j[0:1, :] - pick(before)) * stride
        return (pick(loc0) + piece).astype(I32), (pick(glob0) + piece).astype(I32)

    blr_ref[0], bgr_ref[0] = copy_list(nbig, off_col, gst_col, per_big, blr_ref.shape[2])
    slr_ref[0], sgr_ref[0] = copy_list(nsm, off_col + nbig * per_big, gst_col + nbig * per_big, 1, slr_ref.shape[2])
    clane = lax.broadcasted_iota(I32, (1, cnt_ref.shape[2]), 1)
    cnt_ref[0] = jnp.where(clane == 0, jnp.sum(nbig, axis=0, keepdims=True),
                           jnp.where(clane == 1, jnp.sum(nsm, axis=0, keepdims=True), 0.0)).astype(I32)
    run_ref[...] = run_ref[...] + nch


def _copy_units(lists):
    cnt_ref = lists[0]
    return cnt_ref[0, 0, 0] * (BIG_CHUNK // RUN_ALIGN) + cnt_ref[0, 0, 1]


def _start_copies(lists, make_copy):
    cnt_ref, blr_ref, bgr_ref, slr_ref, sgr_ref = lists

    def start(nrows, loc_ref, glob_ref, priority):
        def body(j, c):
            make_copy(nrows, pl.multiple_of(loc_ref[0, 0, j] * RUN_ALIGN, RUN_ALIGN),
                      pl.multiple_of(glob_ref[0, 0, j] * RUN_ALIGN, RUN_ALIGN)).start(priority=priority)
            return c
        return body

    lax.fori_loop(0, cnt_ref[0, 0, 0], start(BIG_CHUNK, blr_ref, bgr_ref, 0), 0)
    lax.fori_loop(0, cnt_ref[0, 0, 1], start(RUN_ALIGN, slr_ref, sgr_ref, 1), 0)


def _wait_units(n, make_copy):
    per_wait = WAIT_CHUNK // RUN_ALIGN

    def wait_big(j, c):
        make_copy(WAIT_CHUNK, 0, 0).wait()
        return c

    def wait_small(j, c):
        make_copy(RUN_ALIGN, 0, 0).wait()
        return c

    nbig = lax.shift_right_logical(n, per_wait.bit_length() - 1)
    lax.fori_loop(0, nbig, wait_big, 0)
    lax.fori_loop(nbig * per_wait, n, wait_small, 0)


def _dispatch_kernel(last_ref, cnt_ref, blr_ref, bgr_ref, slr_ref, sgr_ref, h_ref, g_ref, post_ref, xs_ref, sbuf,
                     zbuf, pending, sem, zsem, *, n_exp, top_k, bm):
    i = pl.program_id(0)
    nt = pl.num_programs(0)
    slot = i % 2
    rows = sbuf.shape[1]

    @pl.when(i == 0)
    def _():
        zbuf[...] = jnp.zeros_like(zbuf)

        def zero_copy(e):
            return pltpu.make_async_copy(zbuf, xs_ref.at[pl.ds(last_ref[e] * bm, bm)], zsem)

        def start(e, c):
            @pl.when(last_ref[e] >= 0)
            def _():
                zero_copy(e).start()
            return c

        def wait(e, c):
            @pl.when(last_ref[e] >= 0)
            def _():
                zero_copy(e).wait()
            return c

        lax.fori_loop(0, n_exp, start, 0)
        lax.fori_loop(0, n_exp, wait, 0)

    u = _rms(h_ref[...], g_ref[...]).astype(BF16)
    tm = u.shape[0]
    post = post_ref[...]
    q = lax.broadcasted_iota(I32, (rows, tm), 0)
    perm = jnp.zeros((rows, tm), F32)
    for k in range(top_k):
        perm = jnp.where(q == post[k:k + 1, :], 1.0, perm)
    sbuf[slot] = _pack_pairs(jnp.dot(perm.astype(BF16), u, preferred_element_type=F32))

    def make_copy(s, nrows, src_row, dst_row):
        return pltpu.make_async_copy(sbuf.at[s, pl.ds(src_row, nrows)], xs_ref.at[pl.ds(dst_row, nrows)], sem.at[s])

    lists = (cnt_ref, blr_ref, bgr_ref, slr_ref, sgr_ref)
    _start_copies(lists, functools.partial(make_copy, slot))
    n = _copy_units(lists)

    @pl.when(i > 0)
    def _():
        _wait_units(pending[0], functools.partial(make_copy, 1 - slot))

    pending[0] = n

    @pl.when(i == nt - 1)
    def _():
        _wait_units(n, functools.partial(make_copy, slot))


def _ffn_kernel(bexp_ref, nxt_ref, sblk_ref, spair_ref, nused_ref, xa_ref, xb_ref, wgu_hbm, bgu_ref, wd_hbm, bd_ref,
                ys_ref, wgu_st, wd_st, wgu_bf, wd_bf, ybuf, pend, sem, ysem, *, f, layer):
    b = pl.program_id(0)
    last_step = pl.num_programs(0) - 1
    bm = xa_ref.shape[0]
    slot = b % 2
    prev = bexp_ref[jnp.maximum(b - 1, 0)]
    active = b < nused_ref[0]
    here = jnp.minimum(b, nused_ref[0] - 1)
    pair = spair_ref[here] == 1
    row0 = pl.multiple_of(sblk_ref[here] * bm, bm)

    def out_copy(s, nrows):
        return pltpu.make_async_copy(ybuf.at[s, pl.ds(0, nrows)], ys_ref.at[pl.ds(row0, nrows)], ysem.at[s])

    def drain(s):
        for nblk in (1, 2):
            @pl.when(pend[s] == nblk)
            def _():
                out_copy(s, nblk * bm).wait()
        pend[s] = 0

    @pl.when(b == 0)
    def _():
        pend[0] = 0
        pend[1] = 0

    def fetch(e):
        return (pltpu.make_async_copy(wgu_hbm.at[layer, e], wgu_st, sem.at[0]),
                pltpu.make_async_copy(wd_hbm.at[layer, e], wd_st, sem.at[1]))

    @pl.when(b == 0)
    def _():
        for c in fetch(bexp_ref[0]):
            c.start()

    @pl.when(active & ((b == 0) | (bexp_ref[b] != prev)))
    def _():
        for c in fetch(bexp_ref[b]):
            c.wait()
        wgu_bf[...] = wgu_st[...].astype(BF16)
        wd_bf[...] = wd_st[...].astype(BF16)

        @pl.when(nxt_ref[b] >= 0)
        def _():
            for c in fetch(nxt_ref[b]):
                c.start()

    def ffn(xw):
        x = _unpack_pairs(xw)
        gu = jnp.dot(x, wgu_bf[...], preferred_element_type=F32) + bgu_ref[0]
        gate = jnp.minimum(gu[:, :f], SWIGLU_LIMIT)
        up = jnp.clip(gu[:, f:], -SWIGLU_LIMIT, SWIGLU_LIMIT)
        glu = gate * _sigmoid(gate * SWIGLU_ALPHA)
        act = ((up + 1.0) * glu).astype(BF16)
        y = jnp.dot(act, wd_bf[...], preferred_element_type=F32) + bd_ref[0]
        return _pack_pairs(y.astype(BF16).astype(F32))

    @pl.when(active)
    def _():
        drain(slot)

    @pl.when(active & pair)
    def _():
        ybuf[slot] = ffn(jnp.concatenate([xa_ref[...], xb_ref[...]], axis=0))
        out_copy(slot, 2 * bm).start()
        pend[slot] = 2

    @pl.when(active & jnp.logical_not(pair))
    def _():
        ybuf[slot, 0:bm] = ffn(xa_ref[...])
        out_copy(slot, bm).start()
        pend[slot] = 1

    @pl.when(b == last_step)
    def _():
        drain(0)
        drain(1)


def _combine_kernel(*refs, top_k, final):
    lists, lists_nx = refs[0:5], refs[5:10]
    h_ref, gate_ref, pos_ref, ys_ref, fg_ref, o_ref, ybuf, sem = refs[10:]
    i = pl.program_id(0)
    nt = pl.num_programs(0)
    slot = i % 2
    rows = ybuf.shape[1]

    def make_copy(s, nrows, dst_row, src_row):
        return pltpu.make_async_copy(ys_ref.at[pl.ds(src_row, nrows)], ybuf.at[s, pl.ds(dst_row, nrows)], sem.at[s])

    @pl.when(i == 0)
    def _():
        ybuf[...] = jnp.zeros_like(ybuf)
        _start_copies(lists, functools.partial(make_copy, 0))

    @pl.when(i + 1 < nt)
    def _():
        _start_copies(lists_nx, functools.partial(make_copy, 1 - slot))

    _wait_units(_copy_units(lists), functools.partial(make_copy, slot))

    gates = gate_ref[...]
    pos = pos_ref[...]
    tm, d = h_ref.shape
    q = lax.broadcasted_iota(I32, (tm, rows), 1)
    wsel = jnp.zeros((tm, rows), F32)
    for k in range(top_k):
        wsel = jnp.where(q == pos[:, k:k + 1], gates[:, k:k + 1], wsel)
    out = h_ref[...] + jnp.dot(wsel.astype(BF16), _unpack_pairs(ybuf[slot]), preferred_element_type=F32)
    if final:
        out = _rms(out, fg_ref[...])
    o_ref[...] = out


def _moe_layer(h, g, w_router, b_router, w_gate_up, b_gate_up, w_down, b_down, final_g, final, layer):
    t, d = h.shape
    n_exp = w_router.shape[1]
    f = w_down.shape[2]
    top_k = TOP_K
    dp = d // 2
    bm = MOE_BLOCK
    ts = _pick(t, MOE_SORT_TILE)
    nt = t // ts
    srows = -(-(ts * top_k + n_exp * RUN_ALIGN) // LANES) * LANES
    n_blocks = -(-(t * top_k + nt * n_exp * (RUN_ALIGN - 1)) // bm) + n_exp
    n_slots = n_blocks * bm
    n_steps = (n_blocks + n_exp + 1) // 2
    nbp = -(-n_steps // SUBLANES) * SUBLANES

    tm = ts * max(1, _pick(nt, 4))
    wrt = w_router.T
    wr_hi = wrt.astype(BF16)
    wr_lo = (wrt - wr_hi.astype(F32)).astype(BF16)
    idxt, gatest, ncht = pl.pallas_call(
        functools.partial(_route_kernel, n_exp=n_exp, top_k=top_k, ts=ts),
        grid=(t // tm,),
        in_specs=[pl.BlockSpec((tm, d), lambda i: (i, 0)), _full((1, d)), _full((n_exp, d)),
                  _full((n_exp, d)), _full((n_exp, 1))],
        out_specs=[pl.BlockSpec((top_k, tm), lambda i: (0, i)), pl.BlockSpec((top_k, tm), lambda i: (0, i)),
                   pl.BlockSpec((tm // ts, n_exp, 1), lambda i: (i, 0, 0))],
        out_shape=[jax.ShapeDtypeStruct((top_k, t), I32), jax.ShapeDtypeStruct((top_k, t), F32),
                   jax.ShapeDtypeStruct((nt, n_exp, 1), I32)],
        compiler_params=_params(("arbitrary",)),
        name="moe_route",
    )(h, g.reshape(1, d), wr_hi, wr_lo, b_router.reshape(n_exp, 1))
    idx, gates, nch = idxt.T, gatest.T, ncht.reshape(nt, 1, n_exp)

    per_big = BIG_CHUNK // RUN_ALIGN
    n_big = srows // BIG_CHUNK
    n_small = n_exp * (per_big - 1)
    list_len = (n_big, n_big, n_small, n_small, SUBLANES)
    rank_out = pl.pallas_call(
        functools.partial(_rank_kernel, n_exp=n_exp, top_k=top_k, bm=bm, nbp=nbp),
        grid=(nt,),
        in_specs=[pl.BlockSpec((ts, top_k), lambda i: (i, 0)), pl.BlockSpec((top_k, ts), lambda i: (0, i)),
                  _full((nt, 1, n_exp))],
        out_specs=[pl.BlockSpec((ts, top_k), lambda i: (i, 0)), pl.BlockSpec((top_k, ts), lambda i: (0, i))]
        + [pl.BlockSpec((1, 1, r), lambda i: (i, 0, 0)) for r in list_len]
        + [_full((nbp, 1))] * 4 + [_full((SUBLANES, n_exp))],
        out_shape=[jax.ShapeDtypeStruct((t, top_k), I32), jax.ShapeDtypeStruct((top_k, t), I32)]
        + [jax.ShapeDtypeStruct((nt, 1, r), I32) for r in list_len]
        + [jax.ShapeDtypeStruct((nbp, 1), I32)] * 4 + [jax.ShapeDtypeStruct((SUBLANES, n_exp), I32)],
        scratch_shapes=[pltpu.VMEM((1, n_exp), F32), pltpu.VMEM((1, n_exp), F32)],
        compiler_params=_params(("arbitrary",)),
        name="moe_rank",
    )(idx, idxt, nch)
    pos, post = rank_out[0], rank_out[1]
    blr, bgr, slr, sgr, cnt = rank_out[2:7]
    bexp, nxt, sblk, spair = (a.reshape(nbp) for a in rank_out[7:11])
    meta = rank_out[11]
    lists = [cnt, blr, bgr, slr, sgr]
    n_used = meta[0, 0:1]
    last_blk = meta[1]

    def smem_lists(step=0):
        return [pl.BlockSpec((1, 1, a.shape[2]), lambda i, *_: (jnp.minimum(i + step, nt - 1), 0, 0),
                             memory_space=pltpu.SMEM) for a in lists]

    xs = pl.pallas_call(
        functools.partial(_dispatch_kernel, n_exp=n_exp, top_k=top_k, bm=bm),
        grid_spec=pltpu.PrefetchScalarGridSpec(
            num_scalar_prefetch=1,
            grid=(nt,),
            in_specs=smem_lists() + [
                pl.BlockSpec((ts, d), lambda i, last: (i, 0)),
                pl.BlockSpec((1, d), lambda i, last: (0, 0)),
                pl.BlockSpec((top_k, ts), lambda i, last: (0, i)),
            ],
            out_specs=pl.BlockSpec(memory_space=pl.ANY),
            scratch_shapes=[pltpu.VMEM((2, srows, dp), U32), pltpu.VMEM((bm, dp), U32), pltpu.SMEM((1,), I32),
                            pltpu.SemaphoreType.DMA((2,)), pltpu.SemaphoreType.DMA(())],
        ),
        out_shape=jax.ShapeDtypeStruct((n_slots, dp), U32),
        compiler_params=_params(("arbitrary",)),
        name="moe_dispatch",
    )(last_blk, *lists, h, g.reshape(1, d), post)

    def stp(s, nu):
        return jnp.minimum(s, nu[0] - 1)

    def blk_a(s, be, nx, sb, sp, nu):
        return (sb[stp(s, nu)], 0)

    def blk_b(s, be, nx, sb, sp, nu):
        return (sb[stp(s, nu)] + sp[stp(s, nu)], 0)

    def of_expert(shape):
        return pl.BlockSpec(shape, lambda s, be, nx, sb, sp, nu: (be[stp(s, nu)],) + (0,) * (len(shape) - 1))

    ys = pl.pallas_call(
        functools.partial(_ffn_kernel, f=f, layer=layer),
        grid_spec=pltpu.PrefetchScalarGridSpec(
            num_scalar_prefetch=5,
            grid=(n_steps,),
            in_specs=[
                pl.BlockSpec((bm, dp), blk_a), pl.BlockSpec((bm, dp), blk_b),
                pl.BlockSpec(memory_space=pl.ANY), of_expert((1, 1, 2 * f)),
                pl.BlockSpec(memory_space=pl.ANY), of_expert((1, 1, d)),
            ],
            out_specs=pl.BlockSpec(memory_space=pl.ANY),
            scratch_shapes=[pltpu.VMEM((d, 2 * f), F32), pltpu.VMEM((f, d), F32),
                            pltpu.VMEM((d, 2 * f), BF16), pltpu.VMEM((f, d), BF16),
                            pltpu.VMEM((2, 2 * bm, dp), U32), pltpu.SMEM((2,), I32),
                            pltpu.SemaphoreType.DMA((2,)), pltpu.SemaphoreType.DMA((2,))],
        ),
        out_shape=jax.ShapeDtypeStruct((n_slots, dp), U32),
        compiler_params=_params(("arbitrary",)),
        name="moe_ffn",
    )(bexp, nxt, sblk, spair, n_used, xs, xs, w_gate_up, b_gate_up.reshape(n_exp, 1, 2 * f), w_down,
      b_down.reshape(n_exp, 1, d))

    return pl.pallas_call(
        functools.partial(_combine_kernel, top_k=top_k, final=final),
        grid=(nt,),
        in_specs=smem_lists() + smem_lists(step=1) + [
            pl.BlockSpec((ts, d), lambda i: (i, 0)),
            pl.BlockSpec((ts, top_k), lambda i: (i, 0)),
            pl.BlockSpec((ts, top_k), lambda i: (i, 0)),
            pl.BlockSpec(memory_space=pl.ANY),
            _full((1, d)),
        ],
        out_specs=pl.BlockSpec((ts, d), lambda i: (i, 0)),
        out_shape=jax.ShapeDtypeStruct((t, d), F32),
        scratch_shapes=[pltpu.VMEM((2, srows, dp), U32), pltpu.SemaphoreType.DMA((2,))],
        compiler_params=_params(("arbitrary",)),
        name="moe_combine",
    )(*lists, *lists, h, gates, pos, ys, final_g.reshape(1, d))


def kernel(x, mix_norm, ffn_norm, final_norm, lru_w_in, lru_conv_w, lru_conv_b, lru_w_a, lru_b_a, lru_w_x, lru_b_x, lru_a_param, lru_w_out, pool_w_in, pool_w_group, pool_scale, pool_w_out, sb_w_qkv, sb_w_out, moe_w_router, moe_b_router, moe_w_gate_up, moe_b_gate_up, moe_w_down, moe_b_down):
    batch, seq, d = x.shape
    depth = mix_norm.shape[0]
    h = x.reshape(batch * seq, d)
    for layer in range(depth):
        kind = layer % N_MIXERS
        slot = layer // N_MIXERS
        if kind == 0:
            h = _lru_layer(h, batch, seq, mix_norm[layer], lru_w_in[slot], lru_conv_w[slot], lru_conv_b[slot],
                           lru_w_a[slot], lru_b_a[slot], lru_w_x[slot], lru_b_x[slot], lru_a_param[slot],
                           lru_w_out[slot])
        elif kind == 1:
            h = _pool_layer(h, batch, seq, mix_norm[layer], pool_w_in[slot], pool_w_group[slot], pool_scale[slot],
                            pool_w_out[slot])
        else:
            h = _sb_layer(h, batch, seq, mix_norm[layer], sb_w_qkv[slot], sb_w_out[slot])
        h = _moe_layer(h, ffn_norm[layer], moe_w_router[layer], moe_b_router[layer], moe_w_gate_up,
                       moe_b_gate_up[layer], moe_w_down, moe_b_down[layer], final_norm,
                       final=(layer == depth - 1), layer=layer)
    return h.reshape(batch, seq, d)
```
